```python
import math
import jax, jax.numpy as jnp
from jax import lax
import numpy as np

D_MODEL = 4096
BATCH = 4
SEQ = 2048
DEPTH = 4
DEC_BATCH = 8
DEC_SEQ = 4
PAST_LEN = 8192
PAGE_SIZE = 128

N_BRANCH = 4
BRANCH_W = D_MODEL // 4
CONV_A_W = 3
GDN_HEADS = 8
GDN_DK = BRANCH_W // GDN_HEADS
GDN_DV = BRANCH_W // GDN_HEADS
GDN_CONV_W = 4
GDN_CHUNK = 64
DIFF_HEADS = 8
DIFF_DQK = BRANCH_W // (2 * DIFF_HEADS)
DIFF_DV = BRANCH_W // DIFF_HEADS
ROPE_DIM = DIFF_DQK // 4
ROPE_THETA = 500000.0
Q_BLOCK = 128
GLA_HEADS = 4
GLA_DK = BRANCH_W // (2 * GLA_HEADS)
GLA_DV = BRANCH_W // GLA_HEADS
GLA_RANK = 16
GLA_NORMALIZER = 16.0
GLA_CHUNK = 32
EPS = 1e-6

SPLIT_SIZES = (
    BRANCH_W, BRANCH_W, BRANCH_W, BRANCH_W,
    GDN_HEADS * GDN_DK, GDN_HEADS * GDN_DK, GDN_HEADS * GDN_DV, BRANCH_W, GDN_HEADS, GDN_HEADS,
    DIFF_HEADS * 2 * DIFF_DQK, DIFF_HEADS * 2 * DIFF_DQK, DIFF_HEADS * DIFF_DV, BRANCH_W,
    GLA_HEADS * GLA_DK, GLA_HEADS * GLA_DK, GLA_HEADS * GLA_DV, BRANCH_W, GLA_RANK,
    N_BRANCH * D_MODEL,
)
D_IN = sum(SPLIT_SIZES)

kernel_name = 'hybrid_gated_branch_decoder_step'


def rms_norm(x, w=None):
    xf = x.astype(jnp.float32)
    y = xf * lax.rsqrt(jnp.mean(xf * xf, axis=-1, keepdims=True) + EPS)
    if w is not None:
        y = y * w.astype(jnp.float32)
    return y.astype(x.dtype)


def l2_normalize(x):
    xf = x.astype(jnp.float32)
    return xf * lax.rsqrt(jnp.sum(xf * xf, axis=-1, keepdims=True) + EPS)


def split_cols(z, sizes):
    idx = np.cumsum(sizes)[:-1].tolist()
    return jnp.split(z, idx, axis=-1)


def causal_conv(u, buf, w):
    K = w.shape[0]
    T = u.shape[1]
    full = jnp.concatenate([buf.astype(u.dtype), u], axis=1)
    y = full[:, 0:T] * w[0]
    for j in range(1, K):
        y = y + full[:, j:j + T] * w[j]
    return y, full[:, T:]


def rope_partial(x, pos):
    half = ROPE_DIM // 2
    inv_freq = ROPE_THETA ** (-jnp.arange(half, dtype=jnp.float32) * (2.0 / ROPE_DIM))
    ang = pos.astype(jnp.float32)[:, None] * inv_freq[None, :]
    cos = jnp.cos(ang)[None, :, None, None, :]
    sin = jnp.sin(ang)[None, :, None, None, :]
    x1 = x[..., :half].astype(jnp.float32)
    x2 = x[..., half:ROPE_DIM].astype(jnp.float32)
    rot = jnp.concatenate([x1 * cos - x2 * sin, x2 * cos + x1 * sin], axis=-1).astype(x.dtype)
    return jnp.concatenate([rot, x[..., ROPE_DIM:]], axis=-1)


def lambda_init_for(layer_idx):
    return 0.8 - 0.6 * math.exp(-0.3 * layer_idx)


def pad_time(a, pad):
    return jnp.pad(a, [(0, 0), (0, pad)] + [(0, 0)] * (a.ndim - 2))


def to_chunks(a, n, c):
    B, T, H = a.shape[:3]
    a = a.reshape((B, n, c, H) + a.shape[3:])
    return jnp.moveaxis(jnp.moveaxis(a, 1, 0), 3, 2)


def from_chunks(o):
    n, B, H, c, d = o.shape
    return jnp.swapaxes(jnp.moveaxis(o, 0, 1), 2, 3).reshape(B, n * c, H, d)


def gated_delta_chunked(q, k, v, g, beta, S0, chunk):
    B, T, H, dk = q.shape
    dv = v.shape[-1]
    c = min(chunk, T)
    n = -(-T // c)
    pad = n * c - T
    f32 = jnp.float32
    q, k, v, g, beta = [pad_time(a.astype(f32), pad) for a in (q, k, v, g, beta)]
    qc, kc, vc = to_chunks(q, n, c), to_chunks(k, n, c), to_chunks(v, n, c)
    gc, bc = to_chunks(g, n, c), to_chunks(beta, n, c)
    G = jnp.cumsum(gc, axis=-1)
    incl = jnp.tril(jnp.ones((c, c), dtype=bool))
    strict = jnp.tril(jnp.ones((c, c), dtype=bool), -1)
    decay = jnp.where(incl, jnp.exp(jnp.where(incl, G[..., :, None] - G[..., None, :], 0.0)), 0.0)
    kb = kc * bc[..., None]
    L = jnp.where(strict, jnp.einsum('nbhid,nbhjd->nbhij', kb, kc) * decay, 0.0)
    A = L + jnp.eye(c, dtype=f32)
    rhs = jnp.concatenate([vc * bc[..., None], kb * jnp.exp(G)[..., None]], axis=-1)
    sol = lax.linalg.triangular_solve(A, rhs, left_side=True, lower=True, unit_diagonal=True)
    u, w = sol[..., :dv], sol[..., dv:]
    qk = jnp.where(incl, jnp.einsum('nbhid,nbhjd->nbhij', qc, kc) * decay, 0.0)
    q_dec = qc * jnp.exp(G)[..., None]
    k_dec = kc * jnp.exp(G[..., -1:] - G)[..., None]
    g_last = jnp.exp(G[..., -1])

    def step(S, inp):
        q_i, k_i, u_i, w_i, qk_i, gl_i = inp
        v_new = u_i - jnp.einsum('bhcd,bhde->bhce', w_i, S)
        o = jnp.einsum('bhcd,bhde->bhce', q_i, S) + jnp.einsum('bhij,bhje->bhie', qk_i, v_new)
        S = S * gl_i[..., None, None] + jnp.einsum('bhcd,bhce->bhde', k_i, v_new)
        return S, o

    S, o = lax.scan(step, S0.astype(f32), (q_dec, k_dec, u, w, qk, g_last))
    return from_chunks(o)[:, :T], S


def gla_chunked(q, k, v, gk, S0, chunk):
    B, T, H, dk = q.shape
    c = min(chunk, T)
    n = -(-T // c)
    pad = n * c - T
    f32 = jnp.float32
    q, k, v, gk = [pad_time(a.astype(f32), pad) for a in (q, k, v, gk)]
    qc, kc, vc, gc = to_chunks(q, n, c), to_chunks(k, n, c), to_chunks(v, n, c), to_chunks(gk, n, c)
    G = jnp.cumsum(gc, axis=-2)
    incl = jnp.tril(jnp.ones((c, c), dtype=bool))
    q_dec = qc * jnp.exp(G)
    attn = jnp.where(incl, jnp.einsum('nbhid,nbhjd->nbhij', q_dec, kc * jnp.exp(-G)), 0.0)
    o_intra = jnp.einsum('nbhij,nbhje->nbhie', attn, vc)
    k_dec = kc * jnp.exp(G[..., -1:, :] - G)
    g_last = jnp.exp(G[..., -1, :])

    def step(S, inp):
        q_i, k_i, v_i, gl_i = inp
        o = jnp.einsum('bhcd,bhde->bhce', q_i, S)
        S = S * gl_i[..., None] + jnp.einsum('bhcd,bhce->bhde', k_i, v_i)
        return S, o

    S, o_inter = lax.scan(step, S0.astype(f32), (q_dec, k_dec, vc, g_last))
    return from_chunks(o_inter + o_intra)[:, :T], S


def diff_attention(q, k, v, lam, q_pos, k_pos):
    s = jnp.einsum('bqhcd,bkhcd->bhcqk', q, k).astype(jnp.float32) * (DIFF_DQK ** -0.5)
    s = jnp.where(k_pos[None, :] <= q_pos[:, None], s, -jnp.inf)
    p = jax.nn.softmax(s, axis=-1)
    p = p[:, :, 0] - lam * p[:, :, 1]
    return jnp.einsum('bhqk,bkhd->bqhd', p.astype(v.dtype), v)


def blocked_causal_diff_attention(q, k, v, lam, pos):
    B, T, H, _, d = q.shape
    nb = T // Q_BLOCK
    qb = jnp.moveaxis(q.reshape(B, nb, Q_BLOCK, H, 2, d), 1, 0)
    pb = pos.reshape(nb, Q_BLOCK)
    ob = lax.map(lambda a: diff_attention(a[0], k, v, lam, a[1], pos), (qb, pb))
    return jnp.moveaxis(ob, 0, 1).reshape(B, T, H, v.shape[-1])


def hybrid_layer(x, c, pos0, conv_a_buf, gdn_buf, gdn_S, gla_S, past_k, past_v, layer_idx,
                 w_ada, b_ada, w_in, conv_a_w, gdn_conv_w, gdn_a_log, gdn_dt_bias, gdn_norm_w,
                 diff_lambda, diff_norm_w, gla_w_gk2, gla_b_gk, gla_norm_w, w_branch, w_out):
    B, T, _ = x.shape
    f32 = jnp.float32
    pos = pos0 + jnp.arange(T, dtype=jnp.int32)
    mod = jnp.einsum('bd,de->be', jax.nn.silu(c), w_ada) + b_ada
    shift, scale, gate = jnp.split(mod, 3, axis=-1)
    h = (rms_norm(x) * (1.0 + scale[:, None]) + shift[:, None]).astype(x.dtype)
    z = jnp.einsum('btd,de->bte', h, w_in)
    (a_h, a_b, a_c, a_z,
     g_q, g_k, g_v, g_z, g_a, g_b,
     d_q, d_k, d_v, d_z,
     l_q, l_k, l_v, l_z, l_lr,
     merge_logits) = split_cols(z, SPLIT_SIZES)

    u_a, new_conv_a = causal_conv(a_c * a_h, conv_a_buf, conv_a_w)
    out_a = a_b * u_a * jax.nn.silu(a_z)

    qkv, new_gdn_buf = causal_conv(jnp.concatenate([g_q, g_k, g_v], axis=-1), gdn_buf, gdn_conv_w)
    qkv = jax.nn.silu(qkv)
    gq, gk, gv = jnp.split(qkv, [GDN_HEADS * GDN_DK, 2 * GDN_HEADS * GDN_DK], axis=-1)
    gq = l2_normalize(gq.reshape(B, T, GDN_HEADS, GDN_DK)) * (GDN_DK ** -0.5)
    gk = l2_normalize(gk.reshape(B, T, GDN_HEADS, GDN_DK))
    gv = gv.reshape(B, T, GDN_HEADS, GDN_DV)
    beta = jax.nn.sigmoid(g_b.astype(f32))
    g_log = -jnp.exp(gdn_a_log.astype(f32)) * jax.nn.softplus(g_a.astype(f32) + gdn_dt_bias.astype(f32))
    o_b, new_gdn_S = gated_delta_chunked(gq, gk, gv, g_log, beta, gdn_S, GDN_CHUNK)
    out_b = rms_norm(o_b.astype(x.dtype), gdn_norm_w).reshape(B, T, BRANCH_W) * jax.nn.silu(g_z)

    dq = rope_partial(d_q.reshape(B, T, DIFF_HEADS, 2, DIFF_DQK), pos)
    dk = rope_partial(d_k.reshape(B, T, DIFF_HEADS, 2, DIFF_DQK), pos)
    dv = d_v.reshape(B, T, DIFF_HEADS, DIFF_DV)
    lam_init = lambda_init_for(layer_idx)
    lf = diff_lambda.astype(f32)
    lam = jnp.exp(jnp.sum(lf[0] * lf[1])) - jnp.exp(jnp.sum(lf[2] * lf[3])) + lam_init
    if past_k is None:
        o_c = blocked_causal_diff_attention(dq, dk, dv, lam, pos)
    else:
        P = past_k.shape[1]
        keys = jnp.concatenate([past_k.astype(dk.dtype), dk], axis=1)
        vals = jnp.concatenate([past_v.astype(dv.dtype), dv], axis=1)
        o_c = diff_attention(dq, keys, vals, lam, pos, jnp.arange(P + T, dtype=jnp.int32))
    out_c = (rms_norm(o_c, diff_norm_w) * (1.0 - lam_init)).astype(x.dtype)
    out_c = out_c.reshape(B, T, BRANCH_W) * jax.nn.silu(d_z)
    new_k = dk.reshape(B, T, DIFF_HEADS, 2 * DIFF_DQK)
    new_v = dv

    lq = l_q.reshape(B, T, GLA_HEADS, GLA_DK) * (GLA_DK ** -0.5)
    lk = l_k.reshape(B, T, GLA_HEADS, GLA_DK)
    lv = l_v.reshape(B, T, GLA_HEADS, GLA_DV)
    gk_log = jax.nn.log_sigmoid((jnp.einsum('btr,re->bte', l_lr, gla_w_gk2) + gla_b_gk).astype(f32)) / GLA_NORMALIZER
    gk_log = gk_log.reshape(B, T, GLA_HEADS, GLA_DK)
    o_d, new_gla_S = gla_chunked(lq, lk, lv, gk_log, gla_S, GLA_CHUNK)
    out_d = rms_norm(o_d.astype(x.dtype), gla_norm_w).reshape(B, T, BRANCH_W) * jax.nn.silu(l_z)

    branches = jnp.stack([out_a, out_b, out_c, out_d], axis=2)
    proj = jnp.einsum('btnw,nwd->btnd', branches, w_branch)
    gates = jax.nn.sigmoid(merge_logits.reshape(B, T, N_BRANCH, D_MODEL))
    y = jnp.einsum('btd,de->bte', jnp.sum(gates * proj, axis=2), w_out)
    x = (x + gate[:, None] * y).astype(x.dtype)
    return x, new_conv_a, new_gdn_buf, new_gdn_S, new_gla_S, new_k, new_v


def run_trunk(x, c, pos0, conv_a0, gdn_conv0, gdn_S0, gla_S0, cache_k, cache_v, page_table, weights):
    B = x.shape[0]
    collected = [[] for _ in range(6)]
    for l in range(DEPTH):
        if cache_k is None:
            past_k = None
            past_v = None
        else:
            past_k = cache_k[l, page_table].reshape(B, -1, DIFF_HEADS, 2, DIFF_DQK)
            past_v = cache_v[l, page_table].reshape(B, -1, DIFF_HEADS, DIFF_DV)
        x, *new = hybrid_layer(x, c, pos0, conv_a0[l], gdn_conv0[l], gdn_S0[l], gla_S0[l],
                               past_k, past_v, l, *[w[l] for w in weights])
        for lst, a in zip(collected, new):
            lst.append(a)
    return (x, *[jnp.stack(lst) for lst in collected])


def setup_inputs(seed: int = 0) -> dict:
    key = jax.random.key(seed)
    ks = jax.random.split(key, 32)
    f32 = jnp.float32

    def nrm(k, shape, s=1.0):
        return jax.random.normal(k, shape, f32) * s

    n_pages = PAST_LEN // PAGE_SIZE
    n_used = DEC_BATCH * n_pages
    n_pool = n_used + max(1, n_used // 4)
    page_table = jax.random.permutation(ks[0], n_pool)[:n_used].reshape(DEC_BATCH, n_pages).astype(jnp.int32)
    dt = jnp.exp(jax.random.uniform(ks[17], (DEPTH, GDN_HEADS), f32, math.log(1e-3), math.log(1e-1)))
    return {
        'x_prompt': nrm(ks[1], (BATCH, SEQ, D_MODEL)),
        'x_sample': nrm(ks[2], (DEC_BATCH, DEC_SEQ, D_MODEL)),
        'c_prompt': nrm(ks[3], (BATCH, D_MODEL)),
        'c_sample': nrm(ks[4], (DEC_BATCH, D_MODEL)),
        'state_conv_a': nrm(ks[5], (DEPTH, DEC_BATCH, CONV_A_W - 1, BRANCH_W)),
        'state_gdn_conv': nrm(ks[6], (DEPTH, DEC_BATCH, GDN_CONV_W - 1, 3 * BRANCH_W)),
        'state_gdn': nrm(ks[7], (DEPTH, DEC_BATCH, GDN_HEADS, GDN_DK, GDN_DV), 0.1),
        'state_gla': nrm(ks[8], (DEPTH, DEC_BATCH, GLA_HEADS, GLA_DK, GLA_DV), 0.1),
        'cache_k': nrm(ks[9], (DEPTH, n_pool, PAGE_SIZE, DIFF_HEADS, 2 * DIFF_DQK)),
        'cache_v': nrm(ks[10], (DEPTH, n_pool, PAGE_SIZE, DIFF_HEADS, DIFF_DV)),
        'page_table': page_table,
        'w_ada': nrm(ks[11], (DEPTH, D_MODEL, 3 * D_MODEL), D_MODEL ** -0.5),
        'b_ada': nrm(ks[12], (DEPTH, 3 * D_MODEL), 0.02),
        'w_in': nrm(ks[13], (DEPTH, D_MODEL, D_IN), D_MODEL ** -0.5),
        'conv_a_w': nrm(ks[14], (DEPTH, CONV_A_W, BRANCH_W), CONV_A_W ** -0.5),
        'gdn_conv_w': nrm(ks[15], (DEPTH, GDN_CONV_W, 3 * BRANCH_W), GDN_CONV_W ** -0.5),
        'gdn_a_log': jnp.log(jax.random.uniform(ks[16], (DEPTH, GDN_HEADS), f32, 1.0, 16.0)),
        'gdn_dt_bias': dt + jnp.log(-jnp.expm1(-dt)),
        'gdn_norm_w': 1.0 + nrm(ks[18], (DEPTH, GDN_DV), 0.02),
        'diff_lambda': nrm(ks[19], (DEPTH, 4, DIFF_DQK), 0.1),
        'diff_norm_w': 1.0 + nrm(ks[20], (DEPTH, DIFF_DV), 0.02),
        'gla_w_gk2': nrm(ks[21], (DEPTH, GLA_RANK, GLA_HEADS * GLA_DK), GLA_RANK ** -0.5),
        'gla_b_gk': nrm(ks[22], (DEPTH, GLA_HEADS * GLA_DK), 0.1),
        'gla_norm_w': 1.0 + nrm(ks[23], (DEPTH, GLA_DV), 0.02),
        'w_branch': nrm(ks[24], (DEPTH, N_BRANCH, BRANCH_W, D_MODEL), BRANCH_W ** -0.5),
        'w_out': nrm(ks[25], (DEPTH, D_MODEL, D_MODEL), D_MODEL ** -0.5),
        'final_norm_w': 1.0 + nrm(ks[26], (D_MODEL,), 0.02),
    }


def reference(x_prompt, x_sample, c_prompt, c_sample, state_conv_a, state_gdn_conv, state_gdn, state_gla,
              cache_k, cache_v, page_table, w_ada, b_ada, w_in, conv_a_w, gdn_conv_w, gdn_a_log, gdn_dt_bias,
              gdn_norm_w, diff_lambda, diff_norm_w, gla_w_gk2, gla_b_gk, gla_norm_w, w_branch, w_out, final_norm_w):
    weights = (w_ada, b_ada, w_in, conv_a_w, gdn_conv_w, gdn_a_log, gdn_dt_bias, gdn_norm_w,
               diff_lambda, diff_norm_w, gla_w_gk2, gla_b_gk, gla_norm_w, w_branch, w_out)
    Bp = x_prompt.shape[0]
    zero_conv_a = jnp.zeros((DEPTH, Bp, CONV_A_W - 1, BRANCH_W), x_prompt.dtype)
    zero_gdn_conv = jnp.zeros((DEPTH, Bp, GDN_CONV_W - 1, 3 * BRANCH_W), x_prompt.dtype)
    zero_gdn = jnp.zeros((DEPTH, Bp, GDN_HEADS, GDN_DK, GDN_DV), jnp.float32)
    zero_gla = jnp.zeros((DEPTH, Bp, GLA_HEADS, GLA_DK, GLA_DV), jnp.float32)
    yp, p_conv_a, p_gdn_conv, p_gdn, p_gla, p_k, p_v = run_trunk(
        x_prompt, c_prompt, 0, zero_conv_a, zero_gdn_conv, zero_gdn, zero_gla, None, None, None, weights)
    past_len = page_table.shape[1] * cache_k.shape[2]
    ys, s_conv_a, s_gdn_conv, s_gdn, s_gla, s_k, s_v = run_trunk(
        x_sample, c_sample, past_len, state_conv_a, state_gdn_conv, state_gdn, state_gla,
        cache_k, cache_v, page_table, weights)
    y_prompt = rms_norm(yp, final_norm_w)
    y_sample = rms_norm(ys, final_norm_w)
    return (y_prompt, y_sample, p_conv_a, p_gdn_conv, p_gdn, p_gla, p_k, p_v,
            s_conv_a, s_gdn_conv, s_gdn, s_gla, s_k, s_v)
```

```python
import functools
import math

import jax
import jax.numpy as jnp
import numpy as np
from jax import lax
from jax.experimental import pallas as pl
from jax.experimental.pallas import tpu as pltpu

F32 = jnp.float32
BF16 = jnp.bfloat16
EPS = 1e-6
GDN_CHUNK = 64
GLA_CHUNK = 32
GLA_NORMALIZER = 16.0
ROPE_THETA = 500000.0
SUBLANES = 8
LANES = 128
VMEM_LIMIT = 56 * 1024 * 1024
HIGHEST = lax.Precision.HIGHEST

_NT = (((1,), (1,)), ((), ()))
_TN = (((0,), (0,)), ((), ()))


def _params(*sem):
    return pltpu.CompilerParams(dimension_semantics=sem, vmem_limit_bytes=VMEM_LIMIT)


def _silu(x):
    return x * jax.nn.sigmoid(x)


def _softplus(x):
    return jnp.maximum(x, 0.0) + jnp.log(1.0 + jnp.exp(-jnp.abs(x)))


def _dot(a, b):
    return jnp.dot(a.astype(BF16), b.astype(BF16), preferred_element_type=F32)


def _dot_nt(a, b):
    return lax.dot_general(a.astype(BF16), b.astype(BF16), _NT, preferred_element_type=F32)


def _dot_tn(a, b):
    return lax.dot_general(a.astype(BF16), b.astype(BF16), _TN, preferred_element_type=F32)


def _tile(n, pref):
    if n <= pref:
        return n
    t = pref
    while n % t:
        t //= 2
    return t


def _ada_kernel(c_ref, w_ref, b_ref, o_ref):
    c = c_ref[...]
    o_ref[0] = _dot(_silu(c), w_ref[0]) + b_ref[0]


def _ada(c_all, w_ada, b_ada):
    depth, d, n = w_ada.shape
    rows = c_all.shape[0]
    tn = _tile(n, 1024)
    return pl.pallas_call(
        _ada_kernel,
        grid=(depth, n // tn),
        in_specs=[
            pl.BlockSpec((rows, d), lambda l, j: (0, 0)),
            pl.BlockSpec((1, d, tn), lambda l, j: (l, 0, j)),
            pl.BlockSpec((1, 1, tn), lambda l, j: (l, 0, j)),
        ],
        out_specs=pl.BlockSpec((1, rows, tn), lambda l, j: (l, 0, j)),
        out_shape=jax.ShapeDtypeStruct((depth, rows, n), F32),
        compiler_params=_params("parallel", "parallel"),
        name="ada_mod",
    )(c_all, w_ada, b_ada.reshape(depth, 1, n))


def _prenorm_kernel(x_ref, sc_ref, sh_ref, h_ref):
    x = x_ref[...]
    y = x * lax.rsqrt(jnp.mean(x * x, axis=-1, keepdims=True) + EPS)
    h_ref[...] = (y * (1.0 + sc_ref[0]) + sh_ref[0]).astype(BF16)


def _prenorm(x, scale, shift, rows_per_mod):
    m, d = x.shape
    r = scale.shape[1]
    tr = r if r > 1 else _tile(rows_per_mod, 256)
    per = rows_per_mod // tr if r == 1 else 1
    return pl.pallas_call(
        _prenorm_kernel,
        grid=(m // tr,),
        in_specs=[
            pl.BlockSpec((tr, d), lambda i: (i, 0)),
            pl.BlockSpec((1, r, d), lambda i: (i // per, 0, 0)),
            pl.BlockSpec((1, r, d), lambda i: (i // per, 0, 0)),
        ],
        out_specs=pl.BlockSpec((tr, d), lambda i: (i, 0)),
        out_shape=jax.ShapeDtypeStruct((m, d), BF16),
        compiler_params=_params("parallel"),
        name="prenorm",
    )(x, scale, shift)


def _mm_kernel(a_ref, w_ref, o_ref):
    o_ref[...] = jnp.dot(a_ref[...], w_ref[...], preferred_element_type=F32).astype(o_ref.dtype)


def _matmul(a, w, out_dtype=F32, tm_pref=1024, tn_pref=512):
    m, k = a.shape
    n = w.shape[1]
    tm = _tile(m, tm_pref)
    tn = _tile(n, tn_pref)
    return pl.pallas_call(
        _mm_kernel,
        grid=(m // tm, n // tn),
        in_specs=[
            pl.BlockSpec((tm, k), lambda i, j: (i, 0)),
            pl.BlockSpec((k, tn), lambda i, j: (0, j)),
        ],
        out_specs=pl.BlockSpec((tm, tn), lambda i, j: (i, j)),
        out_shape=jax.ShapeDtypeStruct((m, n), out_dtype),
        compiler_params=_params("parallel", "arbitrary"),
        name="in_proj",
    )(a, w)


def _shifted(ext, k, n):
    return pltpu.roll(ext, k, 0)[SUBLANES:SUBLANES + n]


def _valid_rows(c, t_valid):
    t = pl.program_id(1)
    row = lax.broadcasted_iota(jnp.int32, (c, 1), 0) + t * c
    return row < t_valid


def _tri(c):
    row = lax.broadcasted_iota(jnp.int32, (c, c), 0)
    col = lax.broadcasted_iota(jnp.int32, (c, c), 1)
    return row, col


def _conva_kernel(z_ref, st_ref, w_ref, o_ref, tail_ref, carry, *, bw):
    t = pl.program_id(1)

    @pl.when(t == 0)
    def _():
        carry[...] = st_ref[0]

    z = z_ref[...]
    n = z.shape[0]
    a_h, a_b, a_c, a_z = (z[:, i * bw:(i + 1) * bw] for i in range(4))
    u = a_c * a_h
    ext = jnp.concatenate([carry[...], u], axis=0)
    w = w_ref[...]
    taps = w.shape[0]
    y = w[taps - 1:taps] * u
    for k in range(1, taps):
        y = y + w[taps - 1 - k:taps - k] * _shifted(ext, k, n)
    o_ref[...] = (a_b * y * _silu(a_z)).astype(BF16)
    tail = u[n - SUBLANES:n]
    carry[...] = tail
    tail_ref[0] = tail


def _conva(z, state_tile, w, nb, tp, bw):
    tt = _tile(tp, 256)
    nt = tp // tt
    return pl.pallas_call(
        functools.partial(_conva_kernel, bw=bw),
        grid=(nb, nt),
        in_specs=[
            pl.BlockSpec((tt, 4 * bw), lambda b, t: (b * nt + t, 0)),
            pl.BlockSpec((1, SUBLANES, bw), lambda b, t: (b, 0, 0)),
            pl.BlockSpec(w.shape, lambda b, t: (0, 0)),
        ],
        out_specs=[
            pl.BlockSpec((tt, bw), lambda b, t: (b * nt + t, 0)),
            pl.BlockSpec((1, SUBLANES, bw), lambda b, t: (b, 0, 0)),
        ],
        out_shape=[
            jax.ShapeDtypeStruct((nb * tp, bw), BF16),
            jax.ShapeDtypeStruct((nb, SUBLANES, bw), F32),
        ],
        scratch_shapes=[pltpu.VMEM((SUBLANES, bw), F32)],
        compiler_params=_params("parallel", "arbitrary"),
        name="branch_conv",
    )(z, state_tile, w)


INVERSE_BASE = 8


def _inverse_masks(row, col, c):
    base = min(INVERSE_BASE, c)
    masks = [(row // base) == (col // base)]
    b = base
    while b < c:
        masks.append(((row // (2 * b)) == (col // (2 * b))) & ((row // b) != (col // b)))
        b *= 2
    return masks


def _unit_lower_inverse(lmat, eye, masks):
    base = min(INVERSE_BASE, lmat.shape[0])
    p = jnp.where(masks[0], lmat, 0.0)
    x = eye - p
    for _ in range(max(0, int(math.log2(base)) - 1)):
        p = _dot(p, p)
        x = x + _dot(x, p)
    for m in masks[1:]:
        x = x - _dot(_dot(x, jnp.where(m, lmat, 0.0)), x)
    return x


def _gdn_kernel(z_ref, zs_ref, cst_ref, cw_ref, alog_ref, dtb_ref, nw_ref, s0_ref,
                o_ref, tail_ref, sout_ref, carry, s_scr, *, bw, heads, t_valid):
    t = pl.program_id(1)
    nt = pl.num_programs(1)

    @pl.when(t == 0)
    def _():
        carry[...] = cst_ref[0]
        s_scr[...] = s0_ref[0]

    z = z_ref[...]
    c = z.shape[0]
    dk = bw // heads
    x = z[:, :3 * bw]
    gz = z[:, 3 * bw:4 * bw]
    ext = jnp.concatenate([carry[...], x], axis=0)
    w = cw_ref[...]
    taps = w.shape[0]
    y = w[taps - 1:taps] * x
    for k in range(1, taps):
        y = y + w[taps - 1 - k:taps - k] * _shifted(ext, k, c)
    tail = x[c - SUBLANES:c]
    carry[...] = tail
    tail_ref[0] = tail
    qkv = _silu(y)

    valid = _valid_rows(c, t_valid)
    zs = zs_ref[...]
    g = jnp.where(valid, -jnp.exp(alog_ref[...]) * _softplus(zs + dtb_ref[...]), 0.0)
    beta = jnp.where(valid, jax.nn.sigmoid(zs), 0.0)
    row, col = _tri(c)
    incl = row >= col
    strict = row > col
    eye = (row == col).astype(F32)
    gsum = jnp.dot(incl.astype(F32), g, precision=HIGHEST, preferred_element_type=F32)
    r128, c128 = _tri(LANES)
    gsum_t = lax.dot_general((r128 == c128).astype(F32), gsum, _NT, precision=HIGHEST,
                             preferred_element_type=F32)
    inv_masks = _inverse_masks(row, col, c)

    for h in range(heads):
        sl = slice(h * dk, (h + 1) * dk)
        q = qkv[:, sl]
        k = qkv[:, bw + h * dk:bw + (h + 1) * dk]
        v = qkv[:, 2 * bw + h * dk:2 * bw + (h + 1) * dk]
        qn = q * lax.rsqrt(jnp.sum(q * q, axis=-1, keepdims=True) + EPS) * (dk ** -0.5)
        kn = jnp.where(valid, k * lax.rsqrt(jnp.sum(k * k, axis=-1, keepdims=True) + EPS), 0.0)
        gcol = jnp.broadcast_to(gsum[:, h:h + 1], (c, dk))
        bcol = jnp.broadcast_to(beta[:, heads + h:heads + h + 1], (c, dk))
        eg = jnp.exp(gcol)
        dmat = gcol[:, :c] - gsum_t[h:h + 1, :]
        decay = jnp.exp(jnp.where(incl, dmat, 0.0))
        kb = kn * bcol
        kq = _dot_nt(jnp.concatenate([kb, qn], axis=0), kn)
        lmat = jnp.where(strict, kq[:c] * decay, 0.0)
        qk = jnp.where(incl, kq[c:] * decay, 0.0)
        tinv = _unit_lower_inverse(lmat, eye, inv_masks)
        sol = _dot(tinv, jnp.concatenate([v * bcol, kb * eg], axis=1))
        u = sol[:, :dk]
        wm = sol[:, dk:]
        g_last = gcol[c - 1:c, :]
        k_dec = kn * jnp.exp(g_last - gcol)
        s_h = s_scr[h]
        ws = _dot(jnp.concatenate([wm, qn * eg], axis=0), s_h)
        v_new = u - ws[:c]
        o = ws[c:] + _dot(qk, v_new)
        s_scr[h] = s_h * jnp.exp(g_last) + _dot_tn(k_dec, v_new)
        on = o * lax.rsqrt(jnp.mean(o * o, axis=-1, keepdims=True) + EPS) * nw_ref[...]
        o_ref[:, sl] = (on * _silu(gz[:, sl])).astype(BF16)

    @pl.when(t == nt - 1)
    def _():
        sout_ref[0] = s_scr[...]


def _gdn(z, zs, conv_tile, conv_w, alog, dtb, norm_w, s0, nb, tp, t_valid, bw, heads, col_block):
    c = min(GDN_CHUNK, tp)
    nt = tp // c
    dk = bw // heads
    return pl.pallas_call(
        functools.partial(_gdn_kernel, bw=bw, heads=heads, t_valid=t_valid),
        grid=(nb, nt),
        in_specs=[
            pl.BlockSpec((c, 4 * bw), lambda b, t: (b * nt + t, col_block)),
            pl.BlockSpec((c, LANES), lambda b, t: (b * nt + t, 0)),
            pl.BlockSpec((1, SUBLANES, 3 * bw), lambda b, t: (b, 0, 0)),
            pl.BlockSpec(conv_w.shape, lambda b, t: (0, 0)),
            pl.BlockSpec((1, LANES), lambda b, t: (0, 0)),
            pl.BlockSpec((1, LANES), lambda b, t: (0, 0)),
            pl.BlockSpec((1, dk), lambda b, t: (0, 0)),
            pl.BlockSpec((1, heads, dk, dk), lambda b, t: (b, 0, 0, 0)),
        ],
        out_specs=[
            pl.BlockSpec((c, bw), lambda b, t: (b * nt + t, 0)),
            pl.BlockSpec((1, SUBLANES, 3 * bw), lambda b, t: (b, 0, 0)),
            pl.BlockSpec((1, heads, dk, dk), lambda b, t: (b, 0, 0, 0)),
        ],
        out_shape=[
            jax.ShapeDtypeStruct((nb * tp, bw), BF16),
            jax.ShapeDtypeStruct((nb, SUBLANES, 3 * bw), F32),
            jax.ShapeDtypeStruct((nb, heads, dk, dk), F32),
        ],
        scratch_shapes=[pltpu.VMEM((SUBLANES, 3 * bw), F32), pltpu.VMEM((heads, dk, dk), F32)],
        compiler_params=_params("parallel", "arbitrary"),
        name="branch_gdn",
    )(z, zs, conv_tile, conv_w, alog, dtb, norm_w, s0)


def _gla_kernel(z_ref, zs_ref, wgk_ref, bgk_ref, nw_ref, s0_ref, o_ref, sout_ref, s_scr,
                *, bw, heads, t_valid):
    t = pl.program_id(1)
    nt = pl.num_programs(1)

    @pl.when(t == 0)
    def _():
        s_scr[...] = s0_ref[0]

    z = z_ref[...]
    c = z.shape[0]
    dk = bw // (2 * heads)
    dv = bw // heads
    valid = _valid_rows(c, t_valid)
    gk = -_softplus(-(_dot(zs_ref[...], wgk_ref[...]) + bgk_ref[...])) / GLA_NORMALIZER
    gk = jnp.where(valid, gk, 0.0)
    row, col = _tri(c)
    incl = row >= col
    gsum = jnp.dot(incl.astype(F32), gk, precision=HIGHEST, preferred_element_type=F32)
    lz = z[:, 2 * bw:3 * bw]
    for h in range(heads):
        q = z[:, h * dk:(h + 1) * dk] * (dk ** -0.5)
        k = jnp.where(valid, z[:, bw // 2 + h * dk:bw // 2 + (h + 1) * dk], 0.0)
        v = z[:, bw + h * dv:bw + (h + 1) * dv]
        gh = gsum[:, h * dk:(h + 1) * dk]
        q_dec = q * jnp.exp(gh)
        attn = jnp.where(incl, _dot_nt(q_dec, k * jnp.exp(-gh)), 0.0)
        s_t = s_scr[h]
        o = _dot(attn, v) + _dot_nt(q_dec, s_t)
        g_last = gh[c - 1:c, :]
        s_scr[h] = s_t * jnp.exp(g_last) + _dot_tn(v, k * jnp.exp(g_last - gh))
        on = o * lax.rsqrt(jnp.mean(o * o, axis=-1, keepdims=True) + EPS) * nw_ref[...]
        o_ref[:, h * dv:(h + 1) * dv] = (on * _silu(lz[:, h * dv:(h + 1) * dv])).astype(BF16)

    @pl.when(t == nt - 1)
    def _():
        sout_ref[0] = s_scr[...]


def _gla(z, zs, wgk_pad, bgk, norm_w, s0_t, nb, tp, t_valid, bw, heads, col_block):
    c = min(GLA_CHUNK, tp)
    nt = tp // c
    dk = bw // (2 * heads)
    dv = bw // heads
    return pl.pallas_call(
        functools.partial(_gla_kernel, bw=bw, heads=heads, t_valid=t_valid),
        grid=(nb, nt),
        in_specs=[
            pl.BlockSpec((c, 3 * bw), lambda b, t: (b * nt + t, col_block)),
            pl.BlockSpec((c, LANES), lambda b, t: (b * nt + t, 0)),
            pl.BlockSpec(wgk_pad.shape, lambda b, t: (0, 0)),
            pl.BlockSpec(bgk.shape, lambda b, t: (0, 0)),
            pl.BlockSpec((1, dv), lambda b, t: (0, 0)),
            pl.BlockSpec((1, heads, dv, dk), lambda b, t: (b, 0, 0, 0)),
        ],
        out_specs=[
            pl.BlockSpec((c, bw), lambda b, t: (b * nt + t, 0)),
            pl.BlockSpec((1, heads, dv, dk), lambda b, t: (b, 0, 0, 0)),
        ],
        out_shape=[
            jax.ShapeDtypeStruct((nb * tp, bw), BF16),
            jax.ShapeDtypeStruct((nb, heads, dv, dk), F32),
        ],
        scratch_shapes=[pltpu.VMEM((heads, dv, dk), F32)],
        compiler_params=_params("parallel", "arbitrary"),
        name="branch_gla",
    )(z, zs, wgk_pad, bgk, norm_w, s0_t)


def _rope_kernel(z_ref, cos_ref, sa_ref, sb_ref, q_ref, k_ref, v_ref, nk_ref, nv_ref,
                 *, bw, heads, half, scale):
    z = z_ref[...]
    cos = cos_ref[...]
    sa = sa_ref[...]
    sb = sb_ref[...]
    dh = bw // heads

    def rope(x):
        return x * cos + pltpu.roll(x, dh - half, 1) * sa + pltpu.roll(x, half, 1) * sb

    for h in range(heads):
        sl = slice(h * dh, (h + 1) * dh)
        q = rope(z[:, sl])
        k = rope(z[:, bw + h * dh:bw + (h + 1) * dh])
        q_ref[:, sl] = (q * scale).astype(BF16)
        k_ref[:, sl] = k.astype(BF16)
        nk_ref[:, sl] = k
    v = z[:, 2 * bw:3 * bw]
    v_ref[...] = v.astype(BF16)
    nv_ref[...] = v


def _rope(z, tabs, nb, tp, bw, heads, col_block):
    tt = _tile(tp, 256)
    nt = tp // tt
    dh = bw // heads
    dqk = dh // 2
    kern = functools.partial(_rope_kernel, bw=bw, heads=heads, half=dqk // 8, scale=dqk ** -0.5)
    rows = nb * tp
    spec = pl.BlockSpec((tt, bw), lambda i: (i, 0))
    tspec = pl.BlockSpec((tt, dh), lambda i: (i % nt, 0))
    return pl.pallas_call(
        kern,
        grid=(rows // tt,),
        in_specs=[pl.BlockSpec((tt, 4 * bw), lambda i: (i, col_block)), tspec, tspec, tspec],
        out_specs=[spec] * 5,
        out_shape=[jax.ShapeDtypeStruct((rows, bw), BF16)] * 3 + [jax.ShapeDtypeStruct((rows, bw), F32)] * 2,
        compiler_params=_params("parallel"),
        name="diff_rope",
    )(z, *tabs)


def _lambda(lam_ref, lam_init):
    lf = lam_ref[...]
    a = jnp.sum(lf[0:1] * lf[1:2], axis=-1, keepdims=True)
    b = jnp.sum(lf[2:3] * lf[3:4], axis=-1, keepdims=True)
    return jnp.exp(a) - jnp.exp(b) + lam_init


def _stack_components(q, dqk):
    lane = lax.broadcasted_iota(jnp.int32, q.shape, 1)
    zero = jnp.zeros_like(q)
    return jnp.concatenate([jnp.where(lane < dqk, q, zero), jnp.where(lane >= dqk, q, zero)], axis=0)


def _online_update(s, v, m, l, acc):
    m_new = jnp.maximum(m, jnp.max(s, axis=-1, keepdims=True))
    alpha = jnp.exp(m - m_new)
    p = jnp.exp(s - m_new)
    return m_new, alpha * l + jnp.sum(p, axis=-1, keepdims=True), alpha * acc + _dot(p, v)


def _attn_finish(acc, l, n, lam, lam_init, nw, dz):
    o = acc[:n] / l[:n] - lam * (acc[n:] / l[n:])
    on = o * lax.rsqrt(jnp.mean(o * o, axis=-1, keepdims=True) + EPS) * nw * (1.0 - lam_init)
    return (on * _silu(dz)).astype(BF16)


def _flash_kernel(q_ref, k_ref, v_ref, dz_ref, lam_ref, nw_ref, o_ref, *, dqk, lam_init):
    i = pl.program_id(2)
    tq = q_ref.shape[0]
    qs = _stack_components(q_ref[...], dqk)
    dv = v_ref.shape[1]

    def block(j, carry):
        start = pl.multiple_of(j * tq, tq)
        s = _dot_nt(qs, k_ref[pl.ds(start, tq), :])
        return s, v_ref[pl.ds(start, tq), :]

    def body(j, carry):
        s, v = block(j, carry)
        return _online_update(s, v, *carry)

    init = (jnp.full((2 * tq, 1), -jnp.inf, F32), jnp.zeros((2 * tq, 1), F32), jnp.zeros((2 * tq, dv), F32))
    carry = lax.fori_loop(0, i, body, init)
    s, v = block(i, carry)
    row = lax.broadcasted_iota(jnp.int32, s.shape, 0) % tq
    col = lax.broadcasted_iota(jnp.int32, s.shape, 1)
    m, l, acc = _online_update(jnp.where(col <= row, s, -jnp.inf), v, *carry)
    o_ref[...] = _attn_finish(acc, l, tq, _lambda(lam_ref, lam_init), lam_init, nw_ref[...], dz_ref[...])


def _flash(q, k, v, z, lam, norm_w, nb, tp, bw, heads, lam_init, dz_col0):
    dh = bw // heads
    tq = _tile(tp, 256)
    nq = tp // tq
    kern = functools.partial(_flash_kernel, dqk=dh // 2, lam_init=lam_init)
    return pl.pallas_call(
        kern,
        grid=(nb, heads, nq),
        in_specs=[
            pl.BlockSpec((tq, dh), lambda b, h, i: (b * nq + i, h)),
            pl.BlockSpec((tp, dh), lambda b, h, i: (b, h)),
            pl.BlockSpec((tp, dh), lambda b, h, i: (b, h)),
            pl.BlockSpec((tq, dh), lambda b, h, i: (b * nq + i, dz_col0 + h)),
            pl.BlockSpec(lam.shape, lambda b, h, i: (0, 0)),
            pl.BlockSpec((1, dh), lambda b, h, i: (0, 0)),
        ],
        out_specs=pl.BlockSpec((tq, dh), lambda b, h, i: (b * nq + i, h)),
        out_shape=jax.ShapeDtypeStruct((nb * tp, bw), BF16),
        compiler_params=_params("parallel", "parallel", "arbitrary"),
        name="diff_flash",
    )(q, k, v, z, lam, norm_w)


def _decode_kernel(pt_ref, q_ref, kc_ref, vc_ref, kn_ref, vn_ref, dz_ref, lam_ref, nw_ref, o_ref,
                   m_scr, l_scr, acc_scr, *, heads, dqk, t_valid, lam_init):
    j = pl.program_id(1)
    nj = pl.num_programs(1)
    n = q_ref.shape[0]
    dh = 2 * dqk

    @pl.when(j == 0)
    def _():
        m_scr[...] = jnp.full(m_scr.shape, -jnp.inf, F32)
        l_scr[...] = jnp.zeros(l_scr.shape, F32)
        acc_scr[...] = jnp.zeros(acc_scr.shape, F32)

    for h in range(heads):
        sl = slice(h * dh, (h + 1) * dh)
        qs = _stack_components(q_ref[:, sl].astype(F32), dqk)
        s = _dot_nt(qs, kc_ref[:, h, :])
        m_scr[h], l_scr[h], acc_scr[h] = _online_update(s, vc_ref[:, h, :], m_scr[h], l_scr[h], acc_scr[h])

    @pl.when(j == nj - 1)
    def _():
        lam = _lambda(lam_ref, lam_init)
        row = lax.broadcasted_iota(jnp.int32, (2 * n, n), 0) % n
        col = lax.broadcasted_iota(jnp.int32, (2 * n, n), 1)
        visible = (col <= row) & (col < t_valid)
        for h in range(heads):
            sl = slice(h * dh, (h + 1) * dh)
            qs = _stack_components(q_ref[:, sl].astype(F32), dqk)
            s = jnp.where(visible, _dot_nt(qs, kn_ref[:, sl]), -jnp.inf)
            m, l, acc = _online_update(s, vn_ref[:, sl], m_scr[h], l_scr[h], acc_scr[h])
            o_ref[:, sl] = _attn_finish(acc, l, n, lam, lam_init, nw_ref[...], dz_ref[:, sl])


def _decode(q, k_new, v_new, z, cache_k, cache_v, page_table, layer, lam, norm_w, nb, t_valid, bw, heads,
            lam_init, dz_col_block):
    n_pages = page_table.shape[1]
    page = cache_k.shape[2]
    dh = bw // heads
    kern = functools.partial(_decode_kernel, heads=heads, dqk=dh // 2, t_valid=t_valid, lam_init=lam_init)
    row_spec = pl.BlockSpec((SUBLANES, bw), lambda b, j, pt: (b, 0))
    cache_spec = pl.BlockSpec((None, None, page, heads, dh), lambda b, j, pt: (layer, pt[b * n_pages + j], 0, 0, 0))
    grid_spec = pltpu.PrefetchScalarGridSpec(
        num_scalar_prefetch=1,
        grid=(nb, n_pages),
        in_specs=[
            row_spec, cache_spec, cache_spec, row_spec, row_spec,
            pl.BlockSpec((SUBLANES, bw), lambda b, j, pt: (b, dz_col_block)),
            pl.BlockSpec(lam.shape, lambda b, j, pt: (0, 0)),
            pl.BlockSpec((1, dh), lambda b, j, pt: (0, 0)),
        ],
        out_specs=row_spec,
        scratch_shapes=[
            pltpu.VMEM((heads, 2 * SUBLANES, 1), F32),
            pltpu.VMEM((heads, 2 * SUBLANES, 1), F32),
            pltpu.VMEM((heads, 2 * SUBLANES, dh), F32),
        ],
    )
    return pl.pallas_call(
        kern,
        grid_spec=grid_spec,
        out_shape=jax.ShapeDtypeStruct((nb * SUBLANES, bw), BF16),
        compiler_params=_params("parallel", "arbitrary"),
        name="diff_decode",
    )(page_table.reshape(-1), q, cache_k, cache_v, k_new, v_new, z, lam, norm_w)


def _merge_kernel(h_ref, a_ref, b_ref, c_ref, d_ref, m0_ref, m1_ref, m2_ref, m3_ref, wb_ref, o_ref):
    h = h_ref[...]
    acc = None
    for n, (br, wm) in enumerate(zip((a_ref, b_ref, c_ref, d_ref), (m0_ref, m1_ref, m2_ref, m3_ref))):
        gate = jax.nn.sigmoid(jnp.dot(h, wm[...], preferred_element_type=F32))
        term = gate * jnp.dot(br[...], wb_ref[n], preferred_element_type=F32)
        acc = term if acc is None else acc + term
    o_ref[...] = acc.astype(BF16)


def _merge(h, branches, w_merge, w_branch):
    m, d = h.shape
    bw = branches[0].shape[1]
    tm = _tile(m, 512)
    tn = _tile(d, 256)
    nj = d // tn
    br_spec = pl.BlockSpec((tm, bw), lambda i, j: (i, 0))
    return pl.pallas_call(
        _merge_kernel,
        grid=(m // tm, nj),
        in_specs=[pl.BlockSpec((tm, d), lambda i, j: (i, 0))] + [br_spec] * 4 + [
            pl.BlockSpec((d, tn), lambda i, j, n=n: (0, n * nj + j)) for n in range(4)
        ] + [pl.BlockSpec((4, bw, tn), lambda i, j: (0, 0, j))],
        out_specs=pl.BlockSpec((tm, tn), lambda i, j: (i, j)),
        out_shape=jax.ShapeDtypeStruct((m, d), BF16),
        compiler_params=_params("parallel", "arbitrary"),
        name="merge",
    )(h, *branches, w_merge, w_merge, w_merge, w_merge, w_branch)


def _outproj_kernel(a_ref, w_ref, x_ref, g_ref, o_ref):
    y = jnp.dot(a_ref[...], w_ref[...], preferred_element_type=F32)
    o_ref[...] = x_ref[...] + g_ref[0] * y


def _outproj(a, w, x, gate, rows_per_mod):
    m, d = a.shape
    r = gate.shape[1]
    tm = r if r > 1 else _tile(rows_per_mod, 1024)
    per = rows_per_mod // tm if r == 1 else 1
    tn = _tile(d, 512)
    return pl.pallas_call(
        _outproj_kernel,
        grid=(m // tm, d // tn),
        in_specs=[
            pl.BlockSpec((tm, d), lambda i, j: (i, 0)),
            pl.BlockSpec((d, tn), lambda i, j: (0, j)),
            pl.BlockSpec((tm, tn), lambda i, j: (i, j)),
            pl.BlockSpec((1, r, tn), lambda i, j: (i // per, 0, j)),
        ],
        out_specs=pl.BlockSpec((tm, tn), lambda i, j: (i, j)),
        out_shape=jax.ShapeDtypeStruct((m, d), F32),
        compiler_params=_params("parallel", "arbitrary"),
        name="out_proj",
    )(a, w, x, gate)


def _final_norm_kernel(x_ref, w_ref, o_ref):
    x = x_ref[...]
    o_ref[...] = x * lax.rsqrt(jnp.mean(x * x, axis=-1, keepdims=True) + EPS) * w_ref[...]


def _final_norm(x, w):
    m, d = x.shape
    tr = _tile(m, 256)
    return pl.pallas_call(
        _final_norm_kernel,
        grid=(m // tr,),
        in_specs=[pl.BlockSpec((tr, d), lambda i: (i, 0)), pl.BlockSpec((1, d), lambda i: (0, 0))],
        out_specs=pl.BlockSpec((tr, d), lambda i: (i, 0)),
        out_shape=jax.ShapeDtypeStruct((m, d), F32),
        compiler_params=_params("parallel"),
        name="final_norm",
    )(x, w.reshape(1, d))


def _rope_tables(pos, dh, dqk):
    rope_dim = dqk // 4
    half = rope_dim // 2
    inv_freq = ROPE_THETA ** (-jnp.arange(half, dtype=F32) * (2.0 / rope_dim))
    ang = pos.astype(F32)[:, None] * inv_freq[None, :]
    cos, sin = jnp.cos(ang), jnp.sin(ang)
    n = pos.shape[0]
    pad = jnp.zeros((n, dqk - rope_dim), F32)
    comp_cos = jnp.concatenate([cos, cos, pad + 1.0], axis=1)
    comp_sa = jnp.concatenate([-sin, jnp.zeros_like(sin), pad], axis=1)
    comp_sb = jnp.concatenate([jnp.zeros_like(sin), sin, pad], axis=1)
    reps = dh // dqk
    return tuple(jnp.tile(t, (1, reps)) for t in (comp_cos, comp_sa, comp_sb))


def _tail_tile(state, width):
    nb, k, _ = state.shape
    return jnp.concatenate([jnp.zeros((nb, SUBLANES - k, width), F32), state.astype(F32)], axis=1)


def _pad_time(x, tp):
    nb, t = x.shape[:2]
    return jnp.pad(x, [(0, 0), (0, tp - t)] + [(0, 0)] * (x.ndim - 2))


def kernel(x_prompt, x_sample, c_prompt, c_sample, state_conv_a, state_gdn_conv, state_gdn, state_gla, cache_k, cache_v, page_table, w_ada, b_ada, w_in, conv_a_w, gdn_conv_w, gdn_a_log, gdn_dt_bias, gdn_norm_w, diff_lambda, diff_norm_w, gla_w_gk2, gla_b_gk, gla_norm_w, w_branch, w_out, final_norm_w):
    nbp, tpp, d = x_prompt.shape
    nbs, tvs, _ = x_sample.shape
    depth = w_in.shape[0]
    bw = d // 4
    gdn_h = gdn_a_log.shape[1]
    diff_h = cache_k.shape[3]
    dh = cache_k.shape[4]
    dqk = dh // 2
    gla_h = state_gla.shape[2]
    gla_dk = state_gla.shape[3]
    rank = gla_w_gk2.shape[1]
    conv_a_k = conv_a_w.shape[1]
    gdn_k = gdn_conv_w.shape[1]
    past_len = page_table.shape[1] * cache_k.shape[2]
    tps = SUBLANES
    assert tvs <= tps and 2 * gdn_h + rank <= LANES and bw // gdn_h == LANES and dh == LANES

    off_gdn = 4 * bw
    off_side = off_gdn + 4 * bw
    off_diff = off_side + 2 * gdn_h
    off_gla = off_diff + 4 * bw
    off_lr = off_gla + 3 * bw
    off_merge = off_lr + rank
    w_main = jnp.concatenate(
        [w_in[:, :, :off_side], w_in[:, :, off_diff:off_gla], w_in[:, :, off_gla:off_lr]], axis=2).astype(BF16)
    w_side = jnp.concatenate(
        [w_in[:, :, off_side:off_diff], w_in[:, :, off_lr:off_merge],
         jnp.zeros((depth, d, LANES - 2 * gdn_h - rank), F32)], axis=2).astype(BF16)
    w_merge = w_in[:, :, off_merge:].astype(BF16)
    w_branch_b = w_branch.astype(BF16)
    w_out_b = w_out.astype(BF16)
    wgk_pad = jnp.concatenate(
        [jnp.zeros((depth, 2 * gdn_h, gla_h * gla_dk), F32), gla_w_gk2,
         jnp.zeros((depth, LANES - 2 * gdn_h - rank, gla_h * gla_dk), F32)], axis=1).astype(BF16)
    lane_pad = jnp.zeros((depth, LANES - gdn_h), F32)
    alog_pad = jnp.concatenate([gdn_a_log, lane_pad], axis=1)
    dtb_pad = jnp.concatenate([gdn_dt_bias, lane_pad], axis=1)

    n_c = nbp + nbs
    c_rows = -(-n_c // SUBLANES) * SUBLANES
    c_all = jnp.concatenate([c_prompt, c_sample, jnp.zeros((c_rows - n_c, d), F32)], axis=0)
    mod = _ada(c_all, w_ada, b_ada)

    groups = []
    groups.append(dict(
        nb=nbp, tp=tpp, tv=tpp, x=x_prompt.reshape(nbp * tpp, d), mod_rows=slice(0, nbp), per_row=False,
        conv_a=jnp.zeros((depth, nbp, conv_a_k - 1, bw), F32),
        gdn_conv=jnp.zeros((depth, nbp, gdn_k - 1, 3 * bw), F32),
        gdn_s=jnp.zeros((depth, nbp) + state_gdn.shape[2:], F32),
        gla_s=jnp.zeros((depth, nbp) + state_gla.shape[2:], F32),
        tabs=_rope_tables(jnp.arange(tpp, dtype=jnp.int32), dh, dqk), paged=False))
    groups.append(dict(
        nb=nbs, tp=tps, tv=tvs, x=_pad_time(x_sample, tps).reshape(nbs * tps, d), mod_rows=slice(nbp, n_c),
        per_row=True, conv_a=state_conv_a, gdn_conv=state_gdn_conv, gdn_s=state_gdn, gla_s=state_gla,
        tabs=_rope_tables(past_len + jnp.arange(tps, dtype=jnp.int32), dh, dqk), paged=True))

    results = []
    for g in groups:
        nb, tp, tv = g["nb"], g["tp"], g["tv"]
        x = g["x"]
        outs = [[] for _ in range(6)]
        for l in range(depth):
            m_l = mod[l, g["mod_rows"]]
            if g["per_row"]:
                m_l = jnp.repeat(m_l, tp, axis=0).reshape(nb, tp, 3 * d)
            else:
                m_l = m_l.reshape(nb, 1, 3 * d)
            shift, scale, gate = m_l[..., :d], m_l[..., d:2 * d], m_l[..., 2 * d:]
            h = _prenorm(x, scale, shift, tp)
            z = _matmul(h, w_main[l])
            zs = _matmul(h, w_side[l], tn_pref=LANES)

            out_a, tail_a = _conva(z, _tail_tile(g["conv_a"][l], bw), conv_a_w[l], nb, tp, bw)
            out_b, tail_b, s_gdn = _gdn(
                z, zs, _tail_tile(g["gdn_conv"][l], 3 * bw), gdn_conv_w[l], alog_pad[l:l + 1], dtb_pad[l:l + 1],
                gdn_norm_w[l:l + 1], g["gdn_s"][l].astype(F32), nb, tp, tv, bw, gdn_h, 1)
            q_r, k_r, v_r, new_k, new_v = _rope(z, g["tabs"], nb, tp, bw, diff_h, 2)
            lam_init = 0.8 - 0.6 * math.exp(-0.3 * l)
            if g["paged"]:
                out_c = _decode(q_r, k_r, v_r, z, cache_k, cache_v, page_table, l, diff_lambda[l],
                                diff_norm_w[l:l + 1], nb, tv, bw, diff_h, lam_init, 2 * 4 + 3)
            else:
                out_c = _flash(q_r, k_r, v_r, z, diff_lambda[l], diff_norm_w[l:l + 1], nb, tp, bw, diff_h,
                               lam_init, (2 * 4 + 3) * diff_h)
            out_d, s_gla_t = _gla(
                z, zs, wgk_pad[l], gla_b_gk[l:l + 1], gla_norm_w[l:l + 1],
                jnp.swapaxes(g["gla_s"][l].astype(F32), -1, -2), nb, tp, tv, bw, gla_h, 4)

            merged = _merge(h, (out_a, out_b, out_c, out_d), w_merge[l], w_branch_b[l])
            x = _outproj(merged, w_out_b[l], x, gate, tp)

            last = tv - (tp - SUBLANES)
            outs[0].append(tail_a[:, last - (conv_a_k - 1):last])
            outs[1].append(tail_b[:, last - (gdn_k - 1):last])
            outs[2].append(s_gdn)
            outs[3].append(jnp.swapaxes(s_gla_t, -1, -2))
            outs[4].append(new_k.reshape(nb, tp, diff_h, dh)[:, :tv])
            outs[5].append(new_v.reshape(nb, tp, diff_h, dh)[:, :tv])
        y = _final_norm(x, final_norm_w).reshape(nb, tp, d)[:, :tv]
        results.append((y, *[jnp.stack(o) for o in outs]))

    (yp, *sp), (ys, *ss) = results
    return (yp, ys, *sp, *ss)
```

```python
import functools
import math

import jax
import jax.numpy as jnp
import numpy as np
from jax import lax
from jax.experimental import pallas as pl
from jax.experimental.pallas import tpu as pltpu

F32 = jnp.float32
BF16 = jnp.bfloat16
EPS = 1e-6
GDN_CHUNK = 64
GDN_CHUNKS_PER_STEP = 4
DECODE_PAGES_PER_STEP = 8
GLA_CHUNK = 32
GLA_NORMALIZER = 16.0
ROPE_THETA = 500000.0
SUBLANES = 8
LANES = 128
VMEM_LIMIT = 56 * 1024 * 1024
HIGHEST = lax.Precision.HIGHEST

_NT = (((1,), (1,)), ((), ()))
_TN = (((0,), (0,)), ((), ()))
_BNN = (((2,), (1,)), ((0,), (0,)))
_BNT = (((2,), (2,)), ((0,), (0,)))
_BTN = (((1,), (1,)), ((0,), (0,)))


def _params(*sem):
    return pltpu.CompilerParams(dimension_semantics=sem, vmem_limit_bytes=VMEM_LIMIT)


def _silu(x):
    return x * jax.nn.sigmoid(x)


def _softplus(x):
    return jnp.maximum(x, 0.0) + jnp.log(1.0 + jnp.exp(-jnp.abs(x)))


def _dot(a, b):
    return jnp.dot(a.astype(BF16), b.astype(BF16), preferred_element_type=F32)


def _dot_nt(a, b):
    return lax.dot_general(a.astype(BF16), b.astype(BF16), _NT, preferred_element_type=F32)


def _dot_tn(a, b):
    return lax.dot_general(a.astype(BF16), b.astype(BF16), _TN, preferred_element_type=F32)


def _bdot(a, b, dims=_BNN):
    return lax.dot_general(a.astype(BF16), b.astype(BF16), dims, preferred_element_type=F32)


def _tile(n, pref):
    if n <= pref:
        return n
    t = pref
    while n % t:
        t //= 2
    return t


def _ada_kernel(c_ref, w_ref, b_ref, o_ref):
    c = c_ref[...]
    o_ref[0] = _dot(_silu(c), w_ref[0]) + b_ref[0]


def _ada(c_all, w_ada, b_ada):
    depth, d, n = w_ada.shape
    rows = c_all.shape[0]
    tn = _tile(n, 1024)
    return pl.pallas_call(
        _ada_kernel,
        grid=(depth, n // tn),
        in_specs=[
            pl.BlockSpec((rows, d), lambda l, j: (0, 0)),
            pl.BlockSpec((1, d, tn), lambda l, j: (l, 0, j)),
            pl.BlockSpec((1, 1, tn), lambda l, j: (l, 0, j)),
        ],
        out_specs=pl.BlockSpec((1, rows, tn), lambda l, j: (l, 0, j)),
        out_shape=jax.ShapeDtypeStruct((depth, rows, n), F32),
        compiler_params=_params("parallel", "parallel"),
        name="ada_mod",
    )(c_all, w_ada, b_ada.reshape(depth, 1, n))


def _prenorm_kernel(x_ref, sc_ref, sh_ref, h_ref):
    x = x_ref[...]
    y = x * lax.rsqrt(jnp.mean(x * x, axis=-1, keepdims=True) + EPS)
    h_ref[...] = (y * (1.0 + sc_ref[0]) + sh_ref[0]).astype(BF16)


def _prenorm(x, scale, shift, rows_per_mod):
    m, d = x.shape
    r = scale.shape[1]
    tr = r if r > 1 else _tile(rows_per_mod, 256)
    per = rows_per_mod // tr if r == 1 else 1
    return pl.pallas_call(
        _prenorm_kernel,
        grid=(m // tr,),
        in_specs=[
            pl.BlockSpec((tr, d), lambda i: (i, 0)),
            pl.BlockSpec((1, r, d), lambda i: (i // per, 0, 0)),
            pl.BlockSpec((1, r, d), lambda i: (i // per, 0, 0)),
        ],
        out_specs=pl.BlockSpec((tr, d), lambda i: (i, 0)),
        out_shape=jax.ShapeDtypeStruct((m, d), BF16),
        compiler_params=_params("parallel"),
        name="prenorm",
    )(x, scale, shift)


def _mm_kernel(a_ref, w_ref, o_ref):
    o_ref[...] = jnp.dot(a_ref[...], w_ref[...], preferred_element_type=F32).astype(o_ref.dtype)


def _matmul(a, w, layer, out_dtype=F32, tm_pref=1024, tn_pref=512):
    m, k = a.shape
    n = w.shape[2]
    tm = _tile(m, tm_pref)
    tn = _tile(n, tn_pref)
    return pl.pallas_call(
        _mm_kernel,
        grid=(m // tm, n // tn),
        in_specs=[
            pl.BlockSpec((tm, k), lambda i, j: (i, 0)),
            pl.BlockSpec((None, k, tn), lambda i, j: (layer, 0, j)),
        ],
        out_specs=pl.BlockSpec((tm, tn), lambda i, j: (i, j)),
        out_shape=jax.ShapeDtypeStruct((m, n), out_dtype),
        compiler_params=_params("parallel", "arbitrary"),
        name="in_proj",
    )(a, w)


def _causal_conv(carry_ref, st_ref, w_ref, u):
    n = u.shape[0]

    @pl.when(pl.program_id(1) == 0)
    def _():
        carry_ref[...] = st_ref[0]

    ext = jnp.concatenate([carry_ref[...], u], axis=0)
    w = w_ref[...]
    taps = w.shape[0]
    y = w[taps - 1:taps] * u
    for k in range(1, taps):
        y = y + w[taps - 1 - k:taps - k] * pltpu.roll(ext, k, 0)[SUBLANES:SUBLANES + n]
    tail = u[n - SUBLANES:n]
    carry_ref[...] = tail
    return y, tail


def _valid_rows(c, t_valid):
    t = pl.program_id(1)
    row = lax.broadcasted_iota(jnp.int32, (c, 1), 0) + t * c
    return row < t_valid


def _tri(c):
    row = lax.broadcasted_iota(jnp.int32, (c, c), 0)
    col = lax.broadcasted_iota(jnp.int32, (c, c), 1)
    return row, col


def _conva_kernel(z_ref, st_ref, w_ref, o_ref, tail_ref, carry, *, bw):
    z = z_ref[...]
    a_h, a_b, a_c, a_z = (z[:, i * bw:(i + 1) * bw] for i in range(4))
    y, tail = _causal_conv(carry, st_ref, w_ref, a_c * a_h)
    o_ref[...] = (a_b * y * _silu(a_z)).astype(BF16)
    tail_ref[0] = tail


def _conva(z, state_tile, w, nb, tp, bw):
    tt = _tile(tp, 256)
    nt = tp // tt
    return pl.pallas_call(
        functools.partial(_conva_kernel, bw=bw),
        grid=(nb, nt),
        in_specs=[
            pl.BlockSpec((tt, 4 * bw), lambda b, t: (b * nt + t, 0)),
            pl.BlockSpec((1, SUBLANES, bw), lambda b, t: (b, 0, 0)),
            pl.BlockSpec(w.shape, lambda b, t: (0, 0)),
        ],
        out_specs=[
            pl.BlockSpec((tt, bw), lambda b, t: (b * nt + t, 0)),
            pl.BlockSpec((1, SUBLANES, bw), lambda b, t: (b, 0, 0)),
        ],
        out_shape=[
            jax.ShapeDtypeStruct((nb * tp, bw), BF16),
            jax.ShapeDtypeStruct((nb, SUBLANES, bw), F32),
        ],
        scratch_shapes=[pltpu.VMEM((SUBLANES, bw), F32)],
        compiler_params=_params("parallel", "arbitrary"),
        name="branch_conv",
    )(z, state_tile, w)


INVERSE_BASE = 8


def _inverse_masks(row, col, c):
    base = min(INVERSE_BASE, c)
    masks = [(row // base) == (col // base)]
    b = base
    while b < c:
        masks.append(((row // (2 * b)) == (col // (2 * b))) & ((row // b) != (col // b)))
        b *= 2
    return masks


def _unit_lower_inverse(lmat, eye, masks, mm):
    base = min(INVERSE_BASE, lmat.shape[-1])
    p = jnp.where(masks[0], lmat, 0.0)
    x = eye - p
    for _ in range(max(0, int(math.log2(base)) - 1)):
        p = mm(p, p)
        x = x + mm(x, p)
    for m in masks[1:]:
        x = x - mm(mm(x, jnp.where(m, lmat, 0.0)), x)
    return x


def _gdn_kernel(z_ref, zs_ref, cst_ref, cw_ref, alog_ref, dtb_ref, nw_ref, s0_ref,
                o_ref, tail_ref, sout_ref, carry, s_scr, *, bw, heads, t_valid, c):
    t = pl.program_id(1)
    nt = pl.num_programs(1)

    @pl.when(t == 0)
    def _():
        s_scr[...] = s0_ref[0]

    z = z_ref[...]
    r = z.shape[0]
    n_chunks = r // c
    dk = bw // heads
    gz = z[:, 3 * bw:4 * bw]
    y, tail = _causal_conv(carry, cst_ref, cw_ref, z[:, :3 * bw])
    tail_ref[0] = tail
    qkv = _silu(y)

    valid = _valid_rows(r, t_valid).astype(F32)
    zs = zs_ref[...]
    g = valid * (-jnp.exp(alog_ref[...]) * _softplus(zs + dtb_ref[...]))
    beta = valid * jax.nn.sigmoid(zs)
    row_r, col_r = _tri(r)
    same_chunk = (row_r // c) == (col_r // c)
    gsum = jnp.dot(((row_r >= col_r) & same_chunk).astype(F32), g, precision=HIGHEST,
                   preferred_element_type=F32)
    row, col = _tri(c)
    incl = row >= col
    strict = row > col
    eye = (row == col).astype(F32)
    inv_masks = _inverse_masks(row, col, c)
    r128, c128 = _tri(LANES)
    eye128 = (r128 == c128).astype(F32)
    rows = [slice(ci * c, (ci + 1) * c) for ci in range(n_chunks)]
    gsum_t = [lax.dot_general(eye128, gsum[rs], _NT, precision=HIGHEST, preferred_element_type=F32)
              for rs in rows]

    def per(fn):
        return jnp.stack([fn(ci, rows[ci], h) for ci in range(n_chunks) for h in range(heads)])

    q = per(lambda ci, rs, h: qkv[rs, h * dk:(h + 1) * dk])
    k = per(lambda ci, rs, h: qkv[rs, bw + h * dk:bw + (h + 1) * dk])
    v = per(lambda ci, rs, h: qkv[rs, 2 * bw + h * dk:2 * bw + (h + 1) * dk])
    gcol = per(lambda ci, rs, h: jnp.broadcast_to(gsum[rs, h:h + 1], (c, dk)))
    bcol = per(lambda ci, rs, h: jnp.broadcast_to(beta[rs, heads + h:heads + h + 1], (c, dk)))
    vcol = per(lambda ci, rs, h: jnp.broadcast_to(valid[rs], (c, dk)))
    grow = per(lambda ci, rs, h: gsum_t[ci][h:h + 1, :])

    qn = q * lax.rsqrt(jnp.sum(q * q, axis=-1, keepdims=True) + EPS) * (dk ** -0.5)
    kn = vcol * (k * lax.rsqrt(jnp.sum(k * k, axis=-1, keepdims=True) + EPS))
    eg = jnp.exp(gcol)
    decay = jnp.exp(jnp.where(incl, gcol[:, :, :c] - grow, 0.0))
    kb = kn * bcol
    kq = _bdot(jnp.concatenate([kb, qn], axis=1), kn, _BNT)
    lmat = jnp.where(strict, kq[:, :c] * decay, 0.0)
    qk = jnp.where(incl, kq[:, c:] * decay, 0.0)
    tinv = _unit_lower_inverse(lmat, eye, inv_masks, _bdot)
    sol = _bdot(tinv, jnp.concatenate([v * bcol, kb * eg], axis=2))
    u = sol[:, :, :dk]
    wq = jnp.concatenate([sol[:, :, dk:], qn * eg], axis=1)
    g_last = gcol[:, c - 1:c, :]
    k_dec = kn * jnp.exp(g_last - gcol)
    s_decay = jnp.exp(g_last)

    s = s_scr[...]
    for ci in range(n_chunks):
        sel = slice(ci * heads, (ci + 1) * heads)
        ws = _bdot(wq[sel], s)
        v_new = u[sel] - ws[:, :c]
        o = ws[:, c:] + _bdot(qk[sel], v_new)
        s = s * s_decay[sel] + _bdot(k_dec[sel], v_new, _BTN)
        on = o * lax.rsqrt(jnp.mean(o * o, axis=-1, keepdims=True) + EPS) * nw_ref[...]
        for h in range(heads):
            sl = slice(h * dk, (h + 1) * dk)
            o_ref[rows[ci], sl] = (on[h] * _silu(gz[rows[ci], sl])).astype(BF16)
    s_scr[...] = s

    @pl.when(t == nt - 1)
    def _():
        sout_ref[0] = s


def _gdn(z, zs, conv_tile, conv_w, alog, dtb, norm_w, s0, nb, tp, t_valid, bw, heads, col_block):
    c = min(GDN_CHUNK, tp)
    r = min(GDN_CHUNK * GDN_CHUNKS_PER_STEP, tp)
    nt = tp // r
    dk = bw // heads
    return pl.pallas_call(
        functools.partial(_gdn_kernel, bw=bw, heads=heads, t_valid=t_valid, c=c),
        grid=(nb, nt),
        in_specs=[
            pl.BlockSpec((r, 4 * bw), lambda b, t: (b * nt + t, col_block)),
            pl.BlockSpec((r, LANES), lambda b, t: (b * nt + t, 0)),
            pl.BlockSpec((1, SUBLANES, 3 * bw), lambda b, t: (b, 0, 0)),
            pl.BlockSpec(conv_w.shape, lambda b, t: (0, 0)),
            pl.BlockSpec((1, LANES), lambda b, t: (0, 0)),
            pl.BlockSpec((1, LANES), lambda b, t: (0, 0)),
            pl.BlockSpec((1, dk), lambda b, t: (0, 0)),
            pl.BlockSpec((1, heads, dk, dk), lambda b, t: (b, 0, 0, 0)),
        ],
        out_specs=[
            pl.BlockSpec((r, bw), lambda b, t: (b * nt + t, 0)),
            pl.BlockSpec((1, SUBLANES, 3 * bw), lambda b, t: (b, 0, 0)),
            pl.BlockSpec((1, heads, dk, dk), lambda b, t: (b, 0, 0, 0)),
        ],
        out_shape=[
            jax.ShapeDtypeStruct((nb * tp, bw), BF16),
            jax.ShapeDtypeStruct((nb, SUBLANES, 3 * bw), F32),
            jax.ShapeDtypeStruct((nb, heads, dk, dk), F32),
        ],
        scratch_shapes=[pltpu.VMEM((SUBLANES, 3 * bw), F32), pltpu.VMEM((heads, dk, dk), F32)],
        compiler_params=_params("parallel", "arbitrary"),
        name="branch_gdn",
    )(z, zs, conv_tile, conv_w, alog, dtb, norm_w, s0)


def _gla_kernel(z_ref, zs_ref, wgk_ref, bgk_ref, nw_ref, s0_ref, o_ref, sout_ref, s_scr,
                *, bw, heads, t_valid):
    t = pl.program_id(1)
    nt = pl.num_programs(1)

    @pl.when(t == 0)
    def _():
        s_scr[...] = s0_ref[0]

    z = z_ref[...]
    c = z.shape[0]
    dk = bw // (2 * heads)
    dv = bw // heads
    valid = _valid_rows(c, t_valid)
    gk = -_softplus(-(_dot(zs_ref[...], wgk_ref[...]) + bgk_ref[...])) / GLA_NORMALIZER
    gk = jnp.where(valid, gk, 0.0)
    row, col = _tri(c)
    incl = row >= col
    gsum = jnp.dot(incl.astype(F32), gk, precision=HIGHEST, preferred_element_type=F32)
    lz = z[:, 2 * bw:3 * bw]
    for h in range(heads):
        q = z[:, h * dk:(h + 1) * dk] * (dk ** -0.5)
        k = jnp.where(valid, z[:, bw // 2 + h * dk:bw // 2 + (h + 1) * dk], 0.0)
        v = z[:, bw + h * dv:bw + (h + 1) * dv]
        gh = gsum[:, h * dk:(h + 1) * dk]
        q_dec = q * jnp.exp(gh)
        attn = jnp.where(incl, _dot_nt(q_dec, k * jnp.exp(-gh)), 0.0)
        s_t = s_scr[h]
        o = _dot(attn, v) + _dot_nt(q_dec, s_t)
        g_last = gh[c - 1:c, :]
        s_scr[h] = s_t * jnp.exp(g_last) + _dot_tn(v, k * jnp.exp(g_last - gh))
        on = o * lax.rsqrt(jnp.mean(o * o, axis=-1, keepdims=True) + EPS) * nw_ref[...]
        o_ref[:, h * dv:(h + 1) * dv] = (on * _silu(lz[:, h * dv:(h + 1) * dv])).astype(BF16)

    @pl.when(t == nt - 1)
    def _():
        sout_ref[0] = s_scr[...]


def _gla(z, zs, wgk_pad, bgk, norm_w, s0_t, nb, tp, t_valid, bw, heads, col_block):
    c = min(GLA_CHUNK, tp)
    nt = tp // c
    dk = bw // (2 * heads)
    dv = bw // heads
    return pl.pallas_call(
        functools.partial(_gla_kernel, bw=bw, heads=heads, t_valid=t_valid),
        grid=(nb, nt),
        in_specs=[
            pl.BlockSpec((c, 3 * bw), lambda b, t: (b * nt + t, col_block)),
            pl.BlockSpec((c, LANES), lambda b, t: (b * nt + t, 0)),
            pl.BlockSpec(wgk_pad.shape, lambda b, t: (0, 0)),
            pl.BlockSpec(bgk.shape, lambda b, t: (0, 0)),
            pl.BlockSpec((1, dv), lambda b, t: (0, 0)),
            pl.BlockSpec((1, heads, dv, dk), lambda b, t: (b, 0, 0, 0)),
        ],
        out_specs=[
            pl.BlockSpec((c, bw), lambda b, t: (b * nt + t, 0)),
            pl.BlockSpec((1, heads, dv, dk), lambda b, t: (b, 0, 0, 0)),
        ],
        out_shape=[
            jax.ShapeDtypeStruct((nb * tp, bw), BF16),
            jax.ShapeDtypeStruct((nb, heads, dv, dk), F32),
        ],
        scratch_shapes=[pltpu.VMEM((heads, dv, dk), F32)],
        compiler_params=_params("parallel", "arbitrary"),
        name="branch_gla",
    )(z, zs, wgk_pad, bgk, norm_w, s0_t)


def _rope_kernel(z_ref, cos_ref, sa_ref, sb_ref, q_ref, k_ref, v_ref, nk_ref, nv_ref,
                 *, bw, heads, half, scale):
    z = z_ref[...]
    cos = cos_ref[...]
    sa = sa_ref[...]
    sb = sb_ref[...]
    dh = bw // heads

    def rope(x):
        return x * cos + pltpu.roll(x, dh - half, 1) * sa + pltpu.roll(x, half, 1) * sb

    for h in range(heads):
        sl = slice(h * dh, (h + 1) * dh)
        q = rope(z[:, sl])
        k = rope(z[:, bw + h * dh:bw + (h + 1) * dh])
        q_ref[:, sl] = (q * scale).astype(BF16)
        k_ref[:, sl] = k.astype(BF16)
        nk_ref[:, sl] = k
    v = z[:, 2 * bw:3 * bw]
    v_ref[...] = v.astype(BF16)
    nv_ref[...] = v


def _rope(z, tabs, nb, tp, bw, heads, col_block):
    tt = _tile(tp, 256)
    nt = tp // tt
    dh = bw // heads
    dqk = dh // 2
    kern = functools.partial(_rope_kernel, bw=bw, heads=heads, half=dqk // 8, scale=dqk ** -0.5)
    rows = nb * tp
    spec = pl.BlockSpec((tt, bw), lambda i: (i, 0))
    tspec = pl.BlockSpec((tt, dh), lambda i: (i % nt, 0))
    return pl.pallas_call(
        kern,
        grid=(rows // tt,),
        in_specs=[pl.BlockSpec((tt, 4 * bw), lambda i: (i, col_block)), tspec, tspec, tspec],
        out_specs=[spec] * 5,
        out_shape=[jax.ShapeDtypeStruct((rows, bw), BF16)] * 3 + [jax.ShapeDtypeStruct((rows, bw), F32)] * 2,
        compiler_params=_params("parallel"),
        name="diff_rope",
    )(z, *tabs)


def _lambda(lam_ref, lam_init):
    lf = lam_ref[...]
    a = jnp.sum(lf[0:1] * lf[1:2], axis=-1, keepdims=True)
    b = jnp.sum(lf[2:3] * lf[3:4], axis=-1, keepdims=True)
    return jnp.exp(a) - jnp.exp(b) + lam_init


def _stack_components(q, dqk):
    lane = lax.broadcasted_iota(jnp.int32, q.shape, 1)
    zero = jnp.zeros_like(q)
    return jnp.concatenate([jnp.where(lane < dqk, q, zero), jnp.where(lane >= dqk, q, zero)], axis=0)


def _online_update(s, v, m, l, acc, mm=_dot):
    m_new = jnp.maximum(m, jnp.max(s, axis=-1, keepdims=True))
    alpha = jnp.exp(m - m_new)
    p = jnp.exp(s - m_new)
    return m_new, alpha * l + jnp.sum(p, axis=-1, keepdims=True), alpha * acc + mm(p, v)


def _attn_finish(acc, l, n, lam, lam_init, nw, dz):
    o = acc[:n] / l[:n] - lam * (acc[n:] / l[n:])
    on = o * lax.rsqrt(jnp.mean(o * o, axis=-1, keepdims=True) + EPS) * nw * (1.0 - lam_init)
    return (on * _silu(dz)).astype(BF16)


def _flash_kernel(q_ref, k_ref, v_ref, dz_ref, lam_ref, nw_ref, o_ref, *, dqk, lam_init, hb):
    i = pl.program_id(2)
    tq = q_ref.shape[0]
    dh = 2 * dqk
    heads = [slice(h * dh, (h + 1) * dh) for h in range(hb)]
    qs = jnp.stack([_stack_components(q_ref[:, sl], dqk) for sl in heads])

    def block(j):
        start = pl.multiple_of(j * tq, tq)
        kb = k_ref[pl.ds(start, tq), :]
        vb = v_ref[pl.ds(start, tq), :]
        s = _bdot(qs, jnp.stack([kb[:, sl] for sl in heads]), _BNT)
        return s, jnp.stack([vb[:, sl] for sl in heads])

    def body(j, carry):
        s, v = block(j)
        return _online_update(s, v, *carry, mm=_bdot)

    init = (jnp.full((hb, 2 * tq, 1), -jnp.inf, F32), jnp.zeros((hb, 2 * tq, 1), F32),
            jnp.zeros((hb, 2 * tq, dh), F32))
    carry = lax.fori_loop(0, i, body, init)
    s, v = block(i)
    row = lax.broadcasted_iota(jnp.int32, s.shape, 1) % tq
    col = lax.broadcasted_iota(jnp.int32, s.shape, 2)
    m, l, acc = _online_update(jnp.where(col <= row, s, -jnp.inf), v, *carry, mm=_bdot)
    lam = _lambda(lam_ref, lam_init)
    for h, sl in enumerate(heads):
        o_ref[:, sl] = _attn_finish(acc[h], l[h], tq, lam, lam_init, nw_ref[...], dz_ref[:, sl])


def _flash(q, k, v, z, lam, norm_w, nb, tp, bw, heads, lam_init, dz_col0):
    dh = bw // heads
    hb = 2 if heads % 2 == 0 else 1
    tq = _tile(tp, 256)
    nq = tp // tq
    kern = functools.partial(_flash_kernel, dqk=dh // 2, lam_init=lam_init, hb=hb)
    return pl.pallas_call(
        kern,
        grid=(nb, heads // hb, nq),
        in_specs=[
            pl.BlockSpec((tq, hb * dh), lambda b, h, i: (b * nq + i, h)),
            pl.BlockSpec((tp, hb * dh), lambda b, h, i: (b, h)),
            pl.BlockSpec((tp, hb * dh), lambda b, h, i: (b, h)),
            pl.BlockSpec((tq, hb * dh), lambda b, h, i: (b * nq + i, dz_col0 // hb + h)),
            pl.BlockSpec(lam.shape, lambda b, h, i: (0, 0)),
            pl.BlockSpec((1, dh), lambda b, h, i: (0, 0)),
        ],
        out_specs=pl.BlockSpec((tq, hb * dh), lambda b, h, i: (b * nq + i, h)),
        out_shape=jax.ShapeDtypeStruct((nb * tp, bw), BF16),
        compiler_params=_params("parallel", "parallel", "arbitrary"),
        name="diff_flash",
    )(q, k, v, z, lam, norm_w)


def _decode_kernel(pt_ref, q_ref, *refs, heads, dqk, t_valid, lam_init, pps):
    kc_refs, vc_refs = refs[:pps], refs[pps:2 * pps]
    kn_ref, vn_ref, dz_ref, lam_ref, nw_ref, o_ref, m_scr, l_scr, acc_scr = refs[2 * pps:]
    j = pl.program_id(1)
    nj = pl.num_programs(1)
    n = q_ref.shape[0]
    dh = 2 * dqk
    rows = 2 * n * heads
    head_lanes = [slice(h * dh, (h + 1) * dh) for h in range(heads)]

    @pl.when(j == 0)
    def _():
        m_scr[...] = jnp.full(m_scr.shape, -jnp.inf, F32)
        l_scr[...] = jnp.zeros(l_scr.shape, F32)
        acc_scr[...] = jnp.zeros(acc_scr.shape, F32)

    q_all = jnp.concatenate([_stack_components(q_ref[:, sl].astype(F32), dqk) for sl in head_lanes], axis=0)
    keys = kc_refs[0].shape[0]
    row_head = lax.broadcasted_iota(jnp.int32, (rows, keys), 0) // (2 * n)
    col_head = lax.broadcasted_iota(jnp.int32, (rows, keys), 1) % heads
    same_head = row_head == col_head
    state = (m_scr[...], l_scr[...], acc_scr[...])
    for kc_ref, vc_ref in zip(kc_refs, vc_refs):
        s = jnp.where(same_head, _dot_nt(q_all, kc_ref[...]), -jnp.inf)
        state = _online_update(s, vc_ref[...], *state)
    m_scr[...], l_scr[...], acc_scr[...] = state

    @pl.when(j == nj - 1)
    def _():
        k_new = jnp.concatenate([kn_ref[:, sl].astype(F32) for sl in head_lanes], axis=0)
        v_new = jnp.concatenate([vn_ref[:, sl].astype(F32) for sl in head_lanes], axis=0)
        r2 = lax.broadcasted_iota(jnp.int32, (rows, n * heads), 0)
        c2 = lax.broadcasted_iota(jnp.int32, (rows, n * heads), 1)
        visible = ((r2 // (2 * n)) == (c2 // n)) & ((c2 % n) <= (r2 % n)) & ((c2 % n) < t_valid)
        s = jnp.where(visible, _dot_nt(q_all, k_new), -jnp.inf)
        m, l, acc = _online_update(s, v_new, *state)
        lam = _lambda(lam_ref, lam_init)
        for h, sl in enumerate(head_lanes):
            blk = slice(h * 2 * n, (h + 1) * 2 * n)
            o_ref[:, sl] = _attn_finish(acc[blk], l[blk], n, lam, lam_init, nw_ref[...], dz_ref[:, sl])


def _decode(q, k_new, v_new, z, cache_k, cache_v, page_table, layer, lam, norm_w, nb, t_valid, bw, heads,
            lam_init, dz_col_block):
    n_pages = page_table.shape[1]
    dh = bw // heads
    pps = _tile(n_pages, DECODE_PAGES_PER_STEP)
    kern = functools.partial(_decode_kernel, heads=heads, dqk=dh // 2, t_valid=t_valid, lam_init=lam_init, pps=pps)
    row_spec = pl.BlockSpec((SUBLANES, bw), lambda b, j, pt: (b, 0))
    cache_specs = [
        pl.BlockSpec((None, None, cache_k.shape[2], dh),
                     lambda b, j, pt, i=i: (layer, pt[b * n_pages + j * pps + i], 0, 0))
        for i in range(pps)
    ]
    rows = 2 * SUBLANES * heads
    grid_spec = pltpu.PrefetchScalarGridSpec(
        num_scalar_prefetch=1,
        grid=(nb, n_pages // pps),
        in_specs=[row_spec] + cache_specs + cache_specs + [
            row_spec, row_spec,
            pl.BlockSpec((SUBLANES, bw), lambda b, j, pt: (b, dz_col_block)),
            pl.BlockSpec(lam.shape, lambda b, j, pt: (0, 0)),
            pl.BlockSpec((1, dh), lambda b, j, pt: (0, 0)),
        ],
        out_specs=row_spec,
        scratch_shapes=[pltpu.VMEM((rows, 1), F32), pltpu.VMEM((rows, 1), F32), pltpu.VMEM((rows, dh), F32)],
    )
    return pl.pallas_call(
        kern,
        grid_spec=grid_spec,
        out_shape=jax.ShapeDtypeStruct((nb * SUBLANES, bw), BF16),
        compiler_params=_params("parallel", "arbitrary"),
        name="diff_decode",
    )(page_table.reshape(-1), q, *([cache_k] * pps), *([cache_v] * pps), k_new, v_new, z, lam, norm_w)


def _merge_kernel(h_ref, a_ref, b_ref, c_ref, d_ref, m0_ref, m1_ref, m2_ref, m3_ref, wb_ref, o_ref):
    h = h_ref[...]
    acc = None
    for n, (br, wm) in enumerate(zip((a_ref, b_ref, c_ref, d_ref), (m0_ref, m1_ref, m2_ref, m3_ref))):
        gate = jax.nn.sigmoid(jnp.dot(h, wm[...], preferred_element_type=F32))
        term = gate * jnp.dot(br[...], wb_ref[n], preferred_element_type=F32)
        acc = term if acc is None else acc + term
    o_ref[...] = acc.astype(BF16)


def _merge(h, branches, w_merge, w_branch, layer):
    m, d = h.shape
    bw = branches[0].shape[1]
    tm = _tile(m, 512)
    tn = _tile(d, 256)
    nj = d // tn
    br_spec = pl.BlockSpec((tm, bw), lambda i, j: (i, 0))
    return pl.pallas_call(
        _merge_kernel,
        grid=(m // tm, nj),
        in_specs=[pl.BlockSpec((tm, d), lambda i, j: (i, 0))] + [br_spec] * 4 + [
            pl.BlockSpec((None, d, tn), lambda i, j, n=n: (layer, 0, n * nj + j)) for n in range(4)
        ] + [pl.BlockSpec((None, 4, bw, tn), lambda i, j: (layer, 0, 0, j))],
        out_specs=pl.BlockSpec((tm, tn), lambda i, j: (i, j)),
        out_shape=jax.ShapeDtypeStruct((m, d), BF16),
        compiler_params=_params("parallel", "arbitrary"),
        name="merge",
    )(h, *branches, w_merge, w_merge, w_merge, w_merge, w_branch)


def _outproj_kernel(a_ref, w_ref, x_ref, g_ref, o_ref):
    y = jnp.dot(a_ref[...], w_ref[...], preferred_element_type=F32)
    o_ref[...] = x_ref[...] + g_ref[0] * y


def _outproj(a, w, layer, x, gate, rows_per_mod):
    m, d = a.shape
    r = gate.shape[1]
    tm = r if r > 1 else _tile(rows_per_mod, 1024)
    per = rows_per_mod // tm if r == 1 else 1
    tn = _tile(d, 512)
    return pl.pallas_call(
        _outproj_kernel,
        grid=(m // tm, d // tn),
        in_specs=[
            pl.BlockSpec((tm, d), lambda i, j: (i, 0)),
            pl.BlockSpec((None, d, tn), lambda i, j: (layer, 0, j)),
            pl.BlockSpec((tm, tn), lambda i, j: (i, j)),
            pl.BlockSpec((1, r, tn), lambda i, j: (i // per, 0, j)),
        ],
        out_specs=pl.BlockSpec((tm, tn), lambda i, j: (i, j)),
        out_shape=jax.ShapeDtypeStruct((m, d), F32),
        compiler_params=_params("parallel", "arbitrary"),
        name="out_proj",
    )(a, w, x, gate)


def _final_norm_kernel(x_ref, w_ref, o_ref):
    x = x_ref[...]
    o_ref[...] = x * lax.rsqrt(jnp.mean(x * x, axis=-1, keepdims=True) + EPS) * w_ref[...]


def _final_norm(x, w):
    m, d = x.shape
    tr = _tile(m, 256)
    return pl.pallas_call(
        _final_norm_kernel,
        grid=(m // tr,),
        in_specs=[pl.BlockSpec((tr, d), lambda i: (i, 0)), pl.BlockSpec((1, d), lambda i: (0, 0))],
        out_specs=pl.BlockSpec((tr, d), lambda i: (i, 0)),
        out_shape=jax.ShapeDtypeStruct((m, d), F32),
        compiler_params=_params("parallel"),
        name="final_norm",
    )(x, w.reshape(1, d))


def _rope_tables(pos, dh, dqk):
    rope_dim = dqk // 4
    half = rope_dim // 2
    inv_freq = ROPE_THETA ** (-jnp.arange(half, dtype=F32) * (2.0 / rope_dim))
    ang = pos.astype(F32)[:, None] * inv_freq[None, :]
    cos, sin = jnp.cos(ang), jnp.sin(ang)
    n = pos.shape[0]
    pad = jnp.zeros((n, dqk - rope_dim), F32)
    comp_cos = jnp.concatenate([cos, cos, pad + 1.0], axis=1)
    comp_sa = jnp.concatenate([-sin, jnp.zeros_like(sin), pad], axis=1)
    comp_sb = jnp.concatenate([jnp.zeros_like(sin), sin, pad], axis=1)
    reps = dh // dqk
    return tuple(jnp.tile(t, (1, reps)) for t in (comp_cos, comp_sa, comp_sb))


def _tail_tile(state, width):
    nb, k, _ = state.shape
    return jnp.concatenate([jnp.zeros((nb, SUBLANES - k, width), F32), state.astype(F32)], axis=1)


def _pad_time(x, tp):
    nb, t = x.shape[:2]
    return jnp.pad(x, [(0, 0), (0, tp - t)] + [(0, 0)] * (x.ndim - 2))


def kernel(x_prompt, x_sample, c_prompt, c_sample, state_conv_a, state_gdn_conv, state_gdn, state_gla, cache_k, cache_v, page_table, w_ada, b_ada, w_in, conv_a_w, gdn_conv_w, gdn_a_log, gdn_dt_bias, gdn_norm_w, diff_lambda, diff_norm_w, gla_w_gk2, gla_b_gk, gla_norm_w, w_branch, w_out, final_norm_w):
    nbp, tpp, d = x_prompt.shape
    nbs, tvs, _ = x_sample.shape
    depth = w_in.shape[0]
    bw = d // 4
    gdn_h = gdn_a_log.shape[1]
    diff_h = cache_k.shape[3]
    dh = cache_k.shape[4]
    dqk = dh // 2
    gla_h = state_gla.shape[2]
    gla_dk = state_gla.shape[3]
    rank = gla_w_gk2.shape[1]
    conv_a_k = conv_a_w.shape[1]
    gdn_k = gdn_conv_w.shape[1]
    past_len = page_table.shape[1] * cache_k.shape[2]
    tps = SUBLANES
    assert tvs <= tps and 2 * gdn_h + rank <= LANES and bw // gdn_h == LANES and dh == LANES

    off_gdn = 4 * bw
    off_side = off_gdn + 4 * bw
    off_diff = off_side + 2 * gdn_h
    off_gla = off_diff + 4 * bw
    off_lr = off_gla + 3 * bw
    off_merge = off_lr + rank
    w_ab = w_in[:, :, :off_side].astype(BF16)
    w_c = w_in[:, :, off_diff:off_gla].astype(BF16)
    w_d = w_in[:, :, off_gla:off_lr].astype(BF16)
    w_side = jnp.concatenate(
        [w_in[:, :, off_side:off_diff], w_in[:, :, off_lr:off_merge],
         jnp.zeros((depth, d, LANES - 2 * gdn_h - rank), F32)], axis=2).astype(BF16)
    w_merge = w_in[:, :, off_merge:].astype(BF16)
    w_branch_b = w_branch.astype(BF16)
    w_out_b = w_out.astype(BF16)
    wgk_pad = jnp.concatenate(
        [jnp.zeros((depth, 2 * gdn_h, gla_h * gla_dk), F32), gla_w_gk2,
         jnp.zeros((depth, LANES - 2 * gdn_h - rank, gla_h * gla_dk), F32)], axis=1).astype(BF16)
    lane_pad = jnp.zeros((depth, LANES - gdn_h), F32)
    alog_pad = jnp.concatenate([gdn_a_log, lane_pad], axis=1)
    dtb_pad = jnp.concatenate([gdn_dt_bias, lane_pad], axis=1)

    pool, page = cache_k.shape[1:3]
    cache_k_rows = cache_k.reshape(depth, pool, page * diff_h, dh)
    cache_v_rows = cache_v.reshape(depth, pool, page * diff_h, dh)

    n_c = nbp + nbs
    c_rows = -(-n_c // SUBLANES) * SUBLANES
    c_all = jnp.concatenate([c_prompt, c_sample, jnp.zeros((c_rows - n_c, d), F32)], axis=0)
    mod = _ada(c_all, w_ada, b_ada)

    groups = []
    groups.append(dict(
        nb=nbp, tp=tpp, tv=tpp, x=x_prompt.reshape(nbp * tpp, d), mod_rows=slice(0, nbp), per_row=False,
        conv_a=jnp.zeros((depth, nbp, conv_a_k - 1, bw), F32),
        gdn_conv=jnp.zeros((depth, nbp, gdn_k - 1, 3 * bw), F32),
        gdn_s=jnp.zeros((depth, nbp) + state_gdn.shape[2:], F32),
        gla_s=jnp.zeros((depth, nbp) + state_gla.shape[2:], F32),
        tabs=_rope_tables(jnp.arange(tpp, dtype=jnp.int32), dh, dqk), paged=False))
    groups.append(dict(
        nb=nbs, tp=tps, tv=tvs, x=_pad_time(x_sample, tps).reshape(nbs * tps, d), mod_rows=slice(nbp, n_c),
        per_row=True, conv_a=state_conv_a, gdn_conv=state_gdn_conv, gdn_s=state_gdn, gla_s=state_gla,
        tabs=_rope_tables(past_len + jnp.arange(tps, dtype=jnp.int32), dh, dqk), paged=True))

    results = []
    for g in groups:
        nb, tp, tv = g["nb"], g["tp"], g["tv"]
        x = g["x"]
        outs = [[] for _ in range(6)]
        for l in range(depth):
            m_l = mod[l, g["mod_rows"]]
            if g["per_row"]:
                m_l = jnp.repeat(m_l, tp, axis=0).reshape(1, nb * tp, 3 * d)
            else:
                m_l = m_l.reshape(nb, 1, 3 * d)
            shift, scale, gate = m_l[..., :d], m_l[..., d:2 * d], m_l[..., 2 * d:]
            h = _prenorm(x, scale, shift, tp)
            z_ab = _matmul(h, w_ab, l)
            z_c = _matmul(h, w_c, l)
            z_d = _matmul(h, w_d, l)
            zs = _matmul(h, w_side, l, tn_pref=LANES)

            out_a, tail_a = _conva(z_ab, _tail_tile(g["conv_a"][l], bw), conv_a_w[l], nb, tp, bw)
            out_b, tail_b, s_gdn = _gdn(
                z_ab, zs, _tail_tile(g["gdn_conv"][l], 3 * bw), gdn_conv_w[l], alog_pad[l:l + 1],
                dtb_pad[l:l + 1], gdn_norm_w[l:l + 1], g["gdn_s"][l].astype(F32), nb, tp, tv, bw, gdn_h, 1)
            q_r, k_r, v_r, new_k, new_v = _rope(z_c, g["tabs"], nb, tp, bw, diff_h, 0)
            lam_init = 0.8 - 0.6 * math.exp(-0.3 * l)
            if g["paged"]:
                out_c = _decode(q_r, k_r, v_r, z_c, cache_k_rows, cache_v_rows, page_table, l, diff_lambda[l],
                                diff_norm_w[l:l + 1], nb, tv, bw, diff_h, lam_init, 3)
            else:
                out_c = _flash(q_r, k_r, v_r, z_c, diff_lambda[l], diff_norm_w[l:l + 1], nb, tp, bw, diff_h,
                               lam_init, 3 * diff_h)
            out_d, s_gla_t = _gla(
                z_d, zs, wgk_pad[l], gla_b_gk[l:l + 1], gla_norm_w[l:l + 1],
                jnp.swapaxes(g["gla_s"][l].astype(F32), -1, -2), nb, tp, tv, bw, gla_h, 0)

            merged = _merge(h, (out_a, out_b, out_c, out_d), w_merge, w_branch_b, l)
            x = _outproj(merged, w_out_b, l, x, gate, tp)

            last = tv - (tp - SUBLANES)
            outs[0].append(tail_a[:, last - (conv_a_k - 1):last])
            outs[1].append(tail_b[:, last - (gdn_k - 1):last])
            outs[2].append(s_gdn)
            outs[3].append(jnp.swapaxes(s_gla_t, -1, -2))
            outs[4].append(new_k.reshape(nb, tp, diff_h, dh)[:, :tv])
            outs[5].append(new_v.reshape(nb, tp, diff_h, dh)[:, :tv])
        y = _final_norm(x, final_norm_w).reshape(nb, tp, d)[:, :tv]
        results.append((y, *[jnp.stack(o) for o in outs]))

    (yp, *sp), (ys, *ss) = results
    return (yp, ys, *sp, *ss)
```

```python
import functools
import math

import jax
import jax.numpy as jnp
import numpy as np
from jax import lax
from jax.experimental import pallas as pl
from jax.experimental.pallas import tpu as pltpu

F32 = jnp.float32
BF16 = jnp.bfloat16
EPS = 1e-6
GDN_CHUNK = 64
GDN_CHUNKS_PER_STEP = 4
DECODE_PAGES_PER_STEP = 8
FLASH_BLOCK = 256
GLA_CHUNK = 32
GLA_CHUNKS_PER_STEP = 8
GLA_NORMALIZER = 16.0
ROPE_THETA = 500000.0
SUBLANES = 8
LANES = 128
VMEM_LIMIT = 56 * 1024 * 1024
HIGHEST = lax.Precision.HIGHEST

_NT = (((1,), (1,)), ((), ()))
_TN = (((0,), (0,)), ((), ()))
_BNN = (((2,), (1,)), ((0,), (0,)))
_BNT = (((2,), (2,)), ((0,), (0,)))
_BTN = (((1,), (1,)), ((0,), (0,)))


def _params(*sem):
    return pltpu.CompilerParams(dimension_semantics=sem, vmem_limit_bytes=VMEM_LIMIT)


def _silu(x):
    return x * jax.nn.sigmoid(x)


def _softplus(x):
    return jnp.maximum(x, 0.0) + jnp.log(1.0 + jnp.exp(-jnp.abs(x)))


def _dot(a, b):
    return jnp.dot(a.astype(BF16), b.astype(BF16), preferred_element_type=F32)


def _dot_nt(a, b):
    return lax.dot_general(a.astype(BF16), b.astype(BF16), _NT, preferred_element_type=F32)


def _dot_tn(a, b):
    return lax.dot_general(a.astype(BF16), b.astype(BF16), _TN, preferred_element_type=F32)


def _bdot(a, b, dims=_BNN):
    return lax.dot_general(a.astype(BF16), b.astype(BF16), dims, preferred_element_type=F32)


def _tile(n, pref):
    if n <= pref:
        return n
    t = pref
    while n % t:
        t //= 2
    return t


def _ada_kernel(c_ref, w_ref, b_ref, o_ref):
    c = c_ref[...]
    o_ref[0] = _dot(_silu(c), w_ref[0]) + b_ref[0]


def _ada(c_all, w_ada, b_ada):
    depth, d, n = w_ada.shape
    rows = c_all.shape[0]
    tn = _tile(n, 1024)
    return pl.pallas_call(
        _ada_kernel,
        grid=(depth, n // tn),
        in_specs=[
            pl.BlockSpec((rows, d), lambda l, j: (0, 0)),
            pl.BlockSpec((1, d, tn), lambda l, j: (l, 0, j)),
            pl.BlockSpec((1, 1, tn), lambda l, j: (l, 0, j)),
        ],
        out_specs=pl.BlockSpec((1, rows, tn), lambda l, j: (l, 0, j)),
        out_shape=jax.ShapeDtypeStruct((depth, rows, n), F32),
        compiler_params=_params("parallel", "parallel"),
        name="ada_mod",
    )(c_all, w_ada, b_ada.reshape(depth, 1, n))


def _repack_kernel(a_ref, b_ref, o_ref, *, shift):
    a = a_ref[...]
    if shift:
        tn = a.shape[1]
        a = jnp.concatenate([a, b_ref[...]], axis=1)[:, shift:shift + tn]
    o_ref[...] = a.astype(BF16)


def _repack(w, start, width, tn):
    depth, k, _ = w.shape
    shift = start % LANES
    base = start - shift
    assert base % tn == 0 and width % tn == 0 and tn % LANES == 0
    tk = _tile(k, 1024)
    return pl.pallas_call(
        functools.partial(_repack_kernel, shift=shift),
        grid=(depth, k // tk, width // tn),
        in_specs=[
            pl.BlockSpec((None, tk, tn), lambda l, i, j: (l, i, base // tn + j)),
            pl.BlockSpec((None, tk, LANES), lambda l, i, j: (l, i, (base + (j + 1) * tn) // LANES)),
        ],
        out_specs=pl.BlockSpec((None, tk, tn), lambda l, i, j: (l, i, j)),
        out_shape=jax.ShapeDtypeStruct((depth, k, width), BF16),
        compiler_params=_params("parallel", "parallel", "parallel"),
        name="repack",
    )(w, w)


def _side_kernel(a_ref, b_ref, o_ref, *, n_a, n_b):
    lane = lax.broadcasted_iota(jnp.int32, o_ref.shape, 1)
    o_ref[...] = jnp.where(lane < n_a, a_ref[...], jnp.where(lane < n_a + n_b, b_ref[...], 0.0)).astype(BF16)


def _side_weights(w, start_a, n_a, start_b, n_b):
    depth, k, _ = w.shape
    assert start_a % LANES == 0 and start_b % LANES == n_a and n_a + n_b <= LANES
    tk = _tile(k, 1024)
    return pl.pallas_call(
        functools.partial(_side_kernel, n_a=n_a, n_b=n_b),
        grid=(depth, k // tk),
        in_specs=[
            pl.BlockSpec((None, tk, LANES), lambda l, i: (l, i, start_a // LANES)),
            pl.BlockSpec((None, tk, LANES), lambda l, i: (l, i, start_b // LANES)),
        ],
        out_specs=pl.BlockSpec((None, tk, LANES), lambda l, i: (l, i, 0)),
        out_shape=jax.ShapeDtypeStruct((depth, k, LANES), BF16),
        compiler_params=_params("parallel", "parallel"),
        name="repack_side",
    )(w, w)


def _prenorm_kernel(x_ref, sc_ref, sh_ref, h_ref):
    x = x_ref[...]
    y = x * lax.rsqrt(jnp.mean(x * x, axis=-1, keepdims=True) + EPS)
    h_ref[...] = (y * (1.0 + sc_ref[0]) + sh_ref[0]).astype(BF16)


def _prenorm(x, scale, shift, rows_per_mod):
    m, d = x.shape
    r = scale.shape[1]
    tr = r if r > 1 else _tile(rows_per_mod, 256)
    per = rows_per_mod // tr if r == 1 else 1
    return pl.pallas_call(
        _prenorm_kernel,
        grid=(m // tr,),
        in_specs=[
            pl.BlockSpec((tr, d), lambda i: (i, 0)),
            pl.BlockSpec((1, r, d), lambda i: (i // per, 0, 0)),
            pl.BlockSpec((1, r, d), lambda i: (i // per, 0, 0)),
        ],
        out_specs=pl.BlockSpec((tr, d), lambda i: (i, 0)),
        out_shape=jax.ShapeDtypeStruct((m, d), BF16),
        compiler_params=_params("parallel"),
        name="prenorm",
    )(x, scale, shift)


def _mm_kernel(a_ref, w_ref, o_ref):
    o_ref[...] = jnp.dot(a_ref[...], w_ref[...], preferred_element_type=F32).astype(o_ref.dtype)


def _matmul(a, w, layer, col0=0, n=None, out_dtype=F32, tm_pref=1024, tn_pref=512):
    m, k = a.shape
    n = w.shape[2] if n is None else n
    tm = _tile(m, tm_pref)
    tn = _tile(n, tn_pref)
    assert col0 % tn == 0
    return pl.pallas_call(
        _mm_kernel,
        grid=(m // tm, n // tn),
        in_specs=[
            pl.BlockSpec((tm, k), lambda i, j: (i, 0)),
            pl.BlockSpec((None, k, tn), lambda i, j: (layer, 0, col0 // tn + j)),
        ],
        out_specs=pl.BlockSpec((tm, tn), lambda i, j: (i, j)),
        out_shape=jax.ShapeDtypeStruct((m, n), out_dtype),
        compiler_params=_params("parallel", "arbitrary"),
        name="in_proj",
    )(a, w)


def _causal_conv(carry_ref, st_ref, w_ref, u):
    n = u.shape[0]

    @pl.when(pl.program_id(1) == 0)
    def _():
        carry_ref[...] = st_ref[0]

    ext = jnp.concatenate([carry_ref[...], u], axis=0)
    w = w_ref[...]
    taps = w.shape[0]
    y = w[taps - 1:taps] * u
    for k in range(1, taps):
        y = y + w[taps - 1 - k:taps - k] * pltpu.roll(ext, k, 0)[SUBLANES:SUBLANES + n]
    tail = u[n - SUBLANES:n]
    carry_ref[...] = tail
    return y, tail


def _valid_rows(c, t_valid):
    t = pl.program_id(1)
    row = lax.broadcasted_iota(jnp.int32, (c, 1), 0) + t * c
    return row < t_valid


def _tri(c):
    row = lax.broadcasted_iota(jnp.int32, (c, c), 0)
    col = lax.broadcasted_iota(jnp.int32, (c, c), 1)
    return row, col


def _conva_kernel(z_ref, st_ref, w_ref, o_ref, tail_ref, carry, *, bw):
    z = z_ref[...]
    a_h, a_b, a_c, a_z = (z[:, i * bw:(i + 1) * bw] for i in range(4))
    y, tail = _causal_conv(carry, st_ref, w_ref, a_c * a_h)
    o_ref[...] = (a_b * y * _silu(a_z)).astype(BF16)
    tail_ref[0] = tail


def _conva(z, state_tile, w, nb, tp, bw):
    tt = _tile(tp, 256)
    nt = tp // tt
    return pl.pallas_call(
        functools.partial(_conva_kernel, bw=bw),
        grid=(nb, nt),
        in_specs=[
            pl.BlockSpec((tt, 4 * bw), lambda b, t: (b * nt + t, 0)),
            pl.BlockSpec((1, SUBLANES, bw), lambda b, t: (b, 0, 0)),
            pl.BlockSpec(w.shape, lambda b, t: (0, 0)),
        ],
        out_specs=[
            pl.BlockSpec((tt, bw), lambda b, t: (b * nt + t, 0)),
            pl.BlockSpec((1, SUBLANES, bw), lambda b, t: (b, 0, 0)),
        ],
        out_shape=[
            jax.ShapeDtypeStruct((nb * tp, bw), BF16),
            jax.ShapeDtypeStruct((nb, SUBLANES, bw), F32),
        ],
        scratch_shapes=[pltpu.VMEM((SUBLANES, bw), F32)],
        compiler_params=_params("parallel", "arbitrary"),
        name="branch_conv",
    )(z, state_tile, w)


INVERSE_BASE = 8


def _inverse_masks(row, col, c):
    base = min(INVERSE_BASE, c)
    masks = [(row // base) == (col // base)]
    b = base
    while b < c:
        masks.append(((row // (2 * b)) == (col // (2 * b))) & ((row // b) != (col // b)))
        b *= 2
    return masks


def _unit_lower_inverse(lmat, eye, masks, mm):
    base = min(INVERSE_BASE, lmat.shape[-1])
    p = jnp.where(masks[0], lmat, 0.0)
    x = eye - p
    for _ in range(max(0, int(math.log2(base)) - 1)):
        p = mm(p, p)
        x = x + mm(x, p)
    for m in masks[1:]:
        x = x - mm(mm(x, jnp.where(m, lmat, 0.0)), x)
    return x


def _gdn_kernel(z_ref, zs_ref, cst_ref, cw_ref, alog_ref, dtb_ref, nw_ref, s0_ref,
                o_ref, tail_ref, sout_ref, carry, s_scr, *, bw, heads, t_valid, c):
    t = pl.program_id(1)
    nt = pl.num_programs(1)

    @pl.when(t == 0)
    def _():
        s_scr[...] = s0_ref[0]

    z = z_ref[...]
    r = z.shape[0]
    n_chunks = r // c
    dk = bw // heads
    gz = z[:, 3 * bw:4 * bw]
    y, tail = _causal_conv(carry, cst_ref, cw_ref, z[:, :3 * bw])
    tail_ref[0] = tail
    qkv = _silu(y)

    valid = _valid_rows(r, t_valid).astype(F32)
    zs = zs_ref[...]
    g = valid * (-jnp.exp(alog_ref[...]) * _softplus(zs + dtb_ref[...]))
    beta = valid * jax.nn.sigmoid(zs)
    row_r, col_r = _tri(r)
    same_chunk = (row_r // c) == (col_r // c)
    gsum = jnp.dot(((row_r >= col_r) & same_chunk).astype(F32), g, precision=HIGHEST,
                   preferred_element_type=F32)
    row, col = _tri(c)
    incl = row >= col
    strict = row > col
    eye = (row == col).astype(F32)
    inv_masks = _inverse_masks(row, col, c)
    r128, c128 = _tri(LANES)
    eye128 = (r128 == c128).astype(F32)
    rows = [slice(ci * c, (ci + 1) * c) for ci in range(n_chunks)]
    gsum_t = [lax.dot_general(eye128, gsum[rs], _NT, precision=HIGHEST, preferred_element_type=F32)
              for rs in rows]

    def per(fn):
        return jnp.stack([fn(ci, rows[ci], h) for ci in range(n_chunks) for h in range(heads)])

    q = per(lambda ci, rs, h: qkv[rs, h * dk:(h + 1) * dk])
    k = per(lambda ci, rs, h: qkv[rs, bw + h * dk:bw + (h + 1) * dk])
    v = per(lambda ci, rs, h: qkv[rs, 2 * bw + h * dk:2 * bw + (h + 1) * dk])
    gcol = per(lambda ci, rs, h: jnp.broadcast_to(gsum[rs, h:h + 1], (c, dk)))
    bcol = per(lambda ci, rs, h: jnp.broadcast_to(beta[rs, heads + h:heads + h + 1], (c, dk)))
    vcol = per(lambda ci, rs, h: jnp.broadcast_to(valid[rs], (c, dk)))
    grow = per(lambda ci, rs, h: gsum_t[ci][h:h + 1, :])

    qn = q * lax.rsqrt(jnp.sum(q * q, axis=-1, keepdims=True) + EPS) * (dk ** -0.5)
    kn = vcol * (k * lax.rsqrt(jnp.sum(k * k, axis=-1, keepdims=True) + EPS))
    eg = jnp.exp(gcol)
    decay = jnp.exp(jnp.where(incl, gcol[:, :, :c] - grow, 0.0))
    kb = kn * bcol
    kq = _bdot(jnp.concatenate([kb, qn], axis=1), kn, _BNT)
    lmat = jnp.where(strict, kq[:, :c] * decay, 0.0)
    qk = jnp.where(incl, kq[:, c:] * decay, 0.0)
    tinv = _unit_lower_inverse(lmat, eye, inv_masks, _bdot)
    sol = _bdot(tinv, jnp.concatenate([v * bcol, kb * eg], axis=2))
    u = sol[:, :, :dk]
    wq = jnp.concatenate([sol[:, :, dk:], qn * eg], axis=1)
    g_last = gcol[:, c - 1:c, :]
    k_dec = kn * jnp.exp(g_last - gcol)
    s_decay = jnp.exp(g_last)

    s = s_scr[...]
    for ci in range(n_chunks):
        sel = slice(ci * heads, (ci + 1) * heads)
        ws = _bdot(wq[sel], s)
        v_new = u[sel] - ws[:, :c]
        o = ws[:, c:] + _bdot(qk[sel], v_new)
        s = s * s_decay[sel] + _bdot(k_dec[sel], v_new, _BTN)
        on = o * lax.rsqrt(jnp.mean(o * o, axis=-1, keepdims=True) + EPS) * nw_ref[...]
        for h in range(heads):
            sl = slice(h * dk, (h + 1) * dk)
            o_ref[rows[ci], sl] = (on[h] * _silu(gz[rows[ci], sl])).astype(BF16)
    s_scr[...] = s

    @pl.when(t == nt - 1)
    def _():
        sout_ref[0] = s


def _gdn(z, zs, conv_tile, conv_w, alog, dtb, norm_w, s0, nb, tp, t_valid, bw, heads, col_block):
    c = min(GDN_CHUNK, tp)
    r = min(GDN_CHUNK * GDN_CHUNKS_PER_STEP, tp)
    nt = tp // r
    dk = bw // heads
    return pl.pallas_call(
        functools.partial(_gdn_kernel, bw=bw, heads=heads, t_valid=t_valid, c=c),
        grid=(nb, nt),
        in_specs=[
            pl.BlockSpec((r, 4 * bw), lambda b, t: (b * nt + t, col_block)),
            pl.BlockSpec((r, LANES), lambda b, t: (b * nt + t, 0)),
            pl.BlockSpec((1, SUBLANES, 3 * bw), lambda b, t: (b, 0, 0)),
            pl.BlockSpec(conv_w.shape, lambda b, t: (0, 0)),
            pl.BlockSpec((1, LANES), lambda b, t: (0, 0)),
            pl.BlockSpec((1, LANES), lambda b, t: (0, 0)),
            pl.BlockSpec((1, dk), lambda b, t: (0, 0)),
            pl.BlockSpec((1, heads, dk, dk), lambda b, t: (b, 0, 0, 0)),
        ],
        out_specs=[
            pl.BlockSpec((r, bw), lambda b, t: (b * nt + t, 0)),
            pl.BlockSpec((1, SUBLANES, 3 * bw), lambda b, t: (b, 0, 0)),
            pl.BlockSpec((1, heads, dk, dk), lambda b, t: (b, 0, 0, 0)),
        ],
        out_shape=[
            jax.ShapeDtypeStruct((nb * tp, bw), BF16),
            jax.ShapeDtypeStruct((nb, SUBLANES, 3 * bw), F32),
            jax.ShapeDtypeStruct((nb, heads, dk, dk), F32),
        ],
        scratch_shapes=[pltpu.VMEM((SUBLANES, 3 * bw), F32), pltpu.VMEM((heads, dk, dk), F32)],
        compiler_params=_params("parallel", "arbitrary"),
        name="branch_gdn",
    )(z, zs, conv_tile, conv_w, alog, dtb, norm_w, s0)


def _gla_kernel(z_ref, zs_ref, wgk_ref, bgk_ref, nw_ref, s0_ref, o_ref, sout_ref, s_scr,
                *, bw, heads, t_valid, c):
    t = pl.program_id(1)
    nt = pl.num_programs(1)

    @pl.when(t == 0)
    def _():
        s_scr[...] = s0_ref[0]

    z = z_ref[...]
    r = z.shape[0]
    n_chunks = r // c
    dk = bw // (2 * heads)
    dv = bw // heads
    valid = _valid_rows(r, t_valid).astype(F32)
    gk = valid * (-_softplus(-(_dot(zs_ref[...], wgk_ref[...]) + bgk_ref[...])) / GLA_NORMALIZER)
    row_r, col_r = _tri(r)
    same_chunk = (row_r // c) == (col_r // c)
    gsum = jnp.dot(((row_r >= col_r) & same_chunk).astype(F32), gk, precision=HIGHEST,
                   preferred_element_type=F32)
    row, col = _tri(c)
    incl = row >= col
    lz = z[:, 2 * bw:3 * bw]
    rows = [slice(ci * c, (ci + 1) * c) for ci in range(n_chunks)]

    def per(fn):
        return jnp.stack([fn(rows[ci], h) for ci in range(n_chunks) for h in range(heads)])

    q = per(lambda rs, h: z[rs, h * dk:(h + 1) * dk]) * (dk ** -0.5)
    k = per(lambda rs, h: z[rs, bw // 2 + h * dk:bw // 2 + (h + 1) * dk] * valid[rs])
    v = per(lambda rs, h: z[rs, bw + h * dv:bw + (h + 1) * dv])
    g = per(lambda rs, h: gsum[rs, h * dk:(h + 1) * dk])
    q_dec = q * jnp.exp(g)
    attn = jnp.where(incl, _bdot(q_dec, k * jnp.exp(-g), _BNT), 0.0)
    o_intra = _bdot(attn, v)
    g_last = g[:, c - 1:c, :]
    k_dec = k * jnp.exp(g_last - g)
    s_decay = jnp.exp(g_last)

    s_t = s_scr[...]
    for ci in range(n_chunks):
        sel = slice(ci * heads, (ci + 1) * heads)
        o = o_intra[sel] + _bdot(q_dec[sel], s_t, _BNT)
        s_t = s_t * s_decay[sel] + _bdot(v[sel], k_dec[sel], _BTN)
        on = o * lax.rsqrt(jnp.mean(o * o, axis=-1, keepdims=True) + EPS) * nw_ref[...]
        for h in range(heads):
            sl = slice(h * dv, (h + 1) * dv)
            o_ref[rows[ci], sl] = (on[h] * _silu(lz[rows[ci], sl])).astype(BF16)
    s_scr[...] = s_t

    @pl.when(t == nt - 1)
    def _():
        sout_ref[0] = s_t


def _gla(z, zs, wgk_pad, bgk, norm_w, s0_t, nb, tp, t_valid, bw, heads, col_block):
    c = min(GLA_CHUNK, tp)
    r = min(GLA_CHUNK * GLA_CHUNKS_PER_STEP, tp)
    nt = tp // r
    dk = bw // (2 * heads)
    dv = bw // heads
    return pl.pallas_call(
        functools.partial(_gla_kernel, bw=bw, heads=heads, t_valid=t_valid, c=c),
        grid=(nb, nt),
        in_specs=[
            pl.BlockSpec((r, 3 * bw), lambda b, t: (b * nt + t, col_block)),
            pl.BlockSpec((r, LANES), lambda b, t: (b * nt + t, 0)),
            pl.BlockSpec(wgk_pad.shape, lambda b, t: (0, 0)),
            pl.BlockSpec(bgk.shape, lambda b, t: (0, 0)),
            pl.BlockSpec((1, dv), lambda b, t: (0, 0)),
            pl.BlockSpec((1, heads, dv, dk), lambda b, t: (b, 0, 0, 0)),
        ],
        out_specs=[
            pl.BlockSpec((r, bw), lambda b, t: (b * nt + t, 0)),
            pl.BlockSpec((1, heads, dv, dk), lambda b, t: (b, 0, 0, 0)),
        ],
        out_shape=[
            jax.ShapeDtypeStruct((nb * tp, bw), BF16),
            jax.ShapeDtypeStruct((nb, heads, dv, dk), F32),
        ],
        scratch_shapes=[pltpu.VMEM((heads, dv, dk), F32)],
        compiler_params=_params("parallel", "arbitrary"),
        name="branch_gla",
    )(z, zs, wgk_pad, bgk, norm_w, s0_t)


def _rope_kernel(z_ref, cos_ref, sa_ref, sb_ref, q_ref, k_ref, v_ref, nk_ref, nv_ref,
                 *, bw, heads, half, scale):
    z = z_ref[...]
    cos = cos_ref[...]
    sa = sa_ref[...]
    sb = sb_ref[...]
    dh = bw // heads

    def rope(x):
        return x * cos + pltpu.roll(x, dh - half, 1) * sa + pltpu.roll(x, half, 1) * sb

    for h in range(heads):
        sl = slice(h * dh, (h + 1) * dh)
        q = rope(z[:, sl])
        k = rope(z[:, bw + h * dh:bw + (h + 1) * dh])
        q_ref[:, sl] = (q * scale).astype(BF16)
        k_ref[:, sl] = k.astype(BF16)
        nk_ref[:, sl] = k
    v = z[:, 2 * bw:3 * bw]
    v_ref[...] = v.astype(BF16)
    nv_ref[...] = v


def _rope(z, tabs, nb, tp, bw, heads, col_block):
    tt = _tile(tp, 256)
    nt = tp // tt
    dh = bw // heads
    dqk = dh // 2
    kern = functools.partial(_rope_kernel, bw=bw, heads=heads, half=dqk // 8, scale=dqk ** -0.5)
    rows = nb * tp
    spec = pl.BlockSpec((tt, bw), lambda i: (i, 0))
    tspec = pl.BlockSpec((tt, dh), lambda i: (i % nt, 0))
    return pl.pallas_call(
        kern,
        grid=(rows // tt,),
        in_specs=[pl.BlockSpec((tt, 4 * bw), lambda i: (i, col_block)), tspec, tspec, tspec],
        out_specs=[spec] * 5,
        out_shape=[jax.ShapeDtypeStruct((rows, bw), BF16)] * 3 + [jax.ShapeDtypeStruct((rows, bw), F32)] * 2,
        compiler_params=_params("parallel"),
        name="diff_rope",
    )(z, *tabs)


def _lambda(lam_ref, lam_init):
    lf = lam_ref[...]
    a = jnp.sum(lf[0:1] * lf[1:2], axis=-1, keepdims=True)
    b = jnp.sum(lf[2:3] * lf[3:4], axis=-1, keepdims=True)
    return jnp.exp(a) - jnp.exp(b) + lam_init


def _stack_components(q, dqk):
    lane = lax.broadcasted_iota(jnp.int32, q.shape, 1)
    zero = jnp.zeros_like(q)
    return jnp.concatenate([jnp.where(lane < dqk, q, zero), jnp.where(lane >= dqk, q, zero)], axis=0)


def _online_update(s, v, m, l, acc, mm=_dot):
    m_new = jnp.maximum(m, jnp.max(s, axis=-1, keepdims=True))
    alpha = jnp.exp(m - m_new)
    p = jnp.exp(s - m_new)
    return m_new, alpha * l + jnp.sum(p, axis=-1, keepdims=True), alpha * acc + mm(p, v)


def _attn_finish(acc, l, n, lam, lam_init, nw, dz):
    o = acc[:n] / l[:n] - lam * (acc[n:] / l[n:])
    on = o * lax.rsqrt(jnp.mean(o * o, axis=-1, keepdims=True) + EPS) * nw * (1.0 - lam_init)
    return (on * _silu(dz)).astype(BF16)


def _flash_kernel(q_ref, k_ref, v_ref, dz_ref, lam_ref, nw_ref, o_ref, *, dqk, lam_init, hb):
    i = pl.program_id(2)
    tq = q_ref.shape[0]
    dh = 2 * dqk
    heads = [slice(h * dh, (h + 1) * dh) for h in range(hb)]
    qs = jnp.stack([_stack_components(q_ref[:, sl], dqk) for sl in heads])

    def block(j):
        start = pl.multiple_of(j * tq, tq)
        kb = k_ref[pl.ds(start, tq), :]
        vb = v_ref[pl.ds(start, tq), :]
        s = _bdot(qs, jnp.stack([kb[:, sl] for sl in heads]), _BNT)
        return s, jnp.stack([vb[:, sl] for sl in heads])

    def body(j, carry):
        s, v = block(j)
        return _online_update(s, v, *carry, mm=_bdot)

    init = (jnp.full((hb, 2 * tq, 1), -jnp.inf, F32), jnp.zeros((hb, 2 * tq, 1), F32),
            jnp.zeros((hb, 2 * tq, dh), F32))
    carry = lax.fori_loop(0, i, body, init)
    s, v = block(i)
    row = lax.broadcasted_iota(jnp.int32, s.shape, 1) % tq
    col = lax.broadcasted_iota(jnp.int32, s.shape, 2)
    m, l, acc = _online_update(jnp.where(col <= row, s, -jnp.inf), v, *carry, mm=_bdot)
    lam = _lambda(lam_ref, lam_init)
    for h, sl in enumerate(heads):
        o_ref[:, sl] = _attn_finish(acc[h], l[h], tq, lam, lam_init, nw_ref[...], dz_ref[:, sl])


def _flash(q, k, v, z, lam, norm_w, nb, tp, bw, heads, lam_init, dz_col0):
    dh = bw // heads
    hb = 2 if heads % 2 == 0 else 1
    tq = _tile(tp, FLASH_BLOCK)
    nq = tp // tq
    kern = functools.partial(_flash_kernel, dqk=dh // 2, lam_init=lam_init, hb=hb)
    return pl.pallas_call(
        kern,
        grid=(nb, heads // hb, nq),
        in_specs=[
            pl.BlockSpec((tq, hb * dh), lambda b, h, i: (b * nq + i, h)),
            pl.BlockSpec((tp, hb * dh), lambda b, h, i: (b, h)),
            pl.BlockSpec((tp, hb * dh), lambda b, h, i: (b, h)),
            pl.BlockSpec((tq, hb * dh), lambda b, h, i: (b * nq + i, dz_col0 // hb + h)),
            pl.BlockSpec(lam.shape, lambda b, h, i: (0, 0)),
            pl.BlockSpec((1, dh), lambda b, h, i: (0, 0)),
        ],
        out_specs=pl.BlockSpec((tq, hb * dh), lambda b, h, i: (b * nq + i, h)),
        out_shape=jax.ShapeDtypeStruct((nb * tp, bw), BF16),
        compiler_params=_params("parallel", "parallel", "arbitrary"),
        name="diff_flash",
    )(q, k, v, z, lam, norm_w)


def _decode_kernel(pt_ref, q_ref, *refs, heads, dqk, t_valid, lam_init, pps):
    kc_refs, vc_refs = refs[:pps], refs[pps:2 * pps]
    kn_ref, vn_ref, dz_ref, lam_ref, nw_ref, o_ref, m_scr, l_scr, acc_scr = refs[2 * pps:]
    j = pl.program_id(1)
    nj = pl.num_programs(1)
    n = q_ref.shape[0]
    dh = 2 * dqk
    rows = 2 * n * heads
    head_lanes = [slice(h * dh, (h + 1) * dh) for h in range(heads)]

    @pl.when(j == 0)
    def _():
        m_scr[...] = jnp.full(m_scr.shape, -jnp.inf, F32)
        l_scr[...] = jnp.zeros(l_scr.shape, F32)
        acc_scr[...] = jnp.zeros(acc_scr.shape, F32)

    q_all = jnp.concatenate([_stack_components(q_ref[:, sl].astype(F32), dqk) for sl in head_lanes], axis=0)
    keys = kc_refs[0].shape[0]
    row_head = lax.broadcasted_iota(jnp.int32, (rows, keys), 0) // (2 * n)
    col_head = lax.broadcasted_iota(jnp.int32, (rows, keys), 1) % heads
    same_head = row_head == col_head
    state = (m_scr[...], l_scr[...], acc_scr[...])
    for kc_ref, vc_ref in zip(kc_refs, vc_refs):
        s = jnp.where(same_head, _dot_nt(q_all, kc_ref[...]), -jnp.inf)
        state = _online_update(s, vc_ref[...], *state)
    m_scr[...], l_scr[...], acc_scr[...] = state

    @pl.when(j == nj - 1)
    def _():
        k_new = jnp.concatenate([kn_ref[:, sl].astype(F32) for sl in head_lanes], axis=0)
        v_new = jnp.concatenate([vn_ref[:, sl].astype(F32) for sl in head_lanes], axis=0)
        r2 = lax.broadcasted_iota(jnp.int32, (rows, n * heads), 0)
        c2 = lax.broadcasted_iota(jnp.int32, (rows, n * heads), 1)
        visible = ((r2 // (2 * n)) == (c2 // n)) & ((c2 % n) <= (r2 % n)) & ((c2 % n) < t_valid)
        s = jnp.where(visible, _dot_nt(q_all, k_new), -jnp.inf)
        m, l, acc = _online_update(s, v_new, *state)
        lam = _lambda(lam_ref, lam_init)
        for h, sl in enumerate(head_lanes):
            blk = slice(h * 2 * n, (h + 1) * 2 * n)
            o_ref[:, sl] = _attn_finish(acc[blk], l[blk], n, lam, lam_init, nw_ref[...], dz_ref[:, sl])


def _decode(q, k_new, v_new, z, cache_k, cache_v, page_table, layer, lam, norm_w, nb, t_valid, bw, heads,
            lam_init, dz_col_block):
    n_pages = page_table.shape[1]
    dh = bw // heads
    pps = _tile(n_pages, DECODE_PAGES_PER_STEP)
    kern = functools.partial(_decode_kernel, heads=heads, dqk=dh // 2, t_valid=t_valid, lam_init=lam_init, pps=pps)
    row_spec = pl.BlockSpec((SUBLANES, bw), lambda b, j, pt: (b, 0))
    cache_specs = [
        pl.BlockSpec((None, None, cache_k.shape[2], dh),
                     lambda b, j, pt, i=i: (layer, pt[b * n_pages + j * pps + i], 0, 0))
        for i in range(pps)
    ]
    rows = 2 * SUBLANES * heads
    grid_spec = pltpu.PrefetchScalarGridSpec(
        num_scalar_prefetch=1,
        grid=(nb, n_pages // pps),
        in_specs=[row_spec] + cache_specs + cache_specs + [
            row_spec, row_spec,
            pl.BlockSpec((SUBLANES, bw), lambda b, j, pt: (b, dz_col_block)),
            pl.BlockSpec(lam.shape, lambda b, j, pt: (0, 0)),
            pl.BlockSpec((1, dh), lambda b, j, pt: (0, 0)),
        ],
        out_specs=row_spec,
        scratch_shapes=[pltpu.VMEM((rows, 1), F32), pltpu.VMEM((rows, 1), F32), pltpu.VMEM((rows, dh), F32)],
    )
    return pl.pallas_call(
        kern,
        grid_spec=grid_spec,
        out_shape=jax.ShapeDtypeStruct((nb * SUBLANES, bw), BF16),
        compiler_params=_params("parallel", "arbitrary"),
        name="diff_decode",
    )(page_table.reshape(-1), q, *([cache_k] * pps), *([cache_v] * pps), k_new, v_new, z, lam, norm_w)


def _merge_kernel(h_ref, a_ref, b_ref, c_ref, d_ref, m0_ref, m1_ref, m2_ref, m3_ref, wb_ref, o_ref):
    h = h_ref[...]
    acc = None
    for n, (br, wm) in enumerate(zip((a_ref, b_ref, c_ref, d_ref), (m0_ref, m1_ref, m2_ref, m3_ref))):
        gate = jax.nn.sigmoid(jnp.dot(h, wm[...], preferred_element_type=F32))
        term = gate * jnp.dot(br[...], wb_ref[n], preferred_element_type=F32)
        acc = term if acc is None else acc + term
    o_ref[...] = acc.astype(BF16)


def _merge(h, branches, w_merge, w_branch, layer):
    m, d = h.shape
    bw = branches[0].shape[1]
    tm = _tile(m, 512)
    tn = _tile(d, 256)
    nj = d // tn
    br_spec = pl.BlockSpec((tm, bw), lambda i, j: (i, 0))
    return pl.pallas_call(
        _merge_kernel,
        grid=(m // tm, nj),
        in_specs=[pl.BlockSpec((tm, d), lambda i, j: (i, 0))] + [br_spec] * 4 + [
            pl.BlockSpec((None, d, tn), lambda i, j, n=n: (layer, 0, n * nj + j)) for n in range(4)
        ] + [pl.BlockSpec((None, 4, bw, tn), lambda i, j: (layer, 0, 0, j))],
        out_specs=pl.BlockSpec((tm, tn), lambda i, j: (i, j)),
        out_shape=jax.ShapeDtypeStruct((m, d), BF16),
        compiler_params=_params("parallel", "arbitrary"),
        name="merge",
    )(h, *branches, w_merge, w_merge, w_merge, w_merge, w_branch)


def _outproj_kernel(a_ref, w_ref, x_ref, g_ref, o_ref):
    y = jnp.dot(a_ref[...], w_ref[...], preferred_element_type=F32)
    o_ref[...] = x_ref[...] + g_ref[0] * y


def _outproj(a, w, layer, x, gate, rows_per_mod):
    m, d = a.shape
    r = gate.shape[1]
    tm = r if r > 1 else _tile(rows_per_mod, 1024)
    per = rows_per_mod // tm if r == 1 else 1
    tn = _tile(d, 512)
    return pl.pallas_call(
        _outproj_kernel,
        grid=(m // tm, d // tn),
        in_specs=[
            pl.BlockSpec((tm, d), lambda i, j: (i, 0)),
            pl.BlockSpec((None, d, tn), lambda i, j: (layer, 0, j)),
            pl.BlockSpec((tm, tn), lambda i, j: (i, j)),
            pl.BlockSpec((1, r, tn), lambda i, j: (i // per, 0, j)),
        ],
        out_specs=pl.BlockSpec((tm, tn), lambda i, j: (i, j)),
        out_shape=jax.ShapeDtypeStruct((m, d), F32),
        compiler_params=_params("parallel", "arbitrary"),
        name="out_proj",
    )(a, w, x, gate)


def _final_norm_kernel(x_ref, w_ref, o_ref):
    x = x_ref[...]
    o_ref[...] = x * lax.rsqrt(jnp.mean(x * x, axis=-1, keepdims=True) + EPS) * w_ref[...]


def _final_norm(x, w):
    m, d = x.shape
    tr = _tile(m, 256)
    return pl.pallas_call(
        _final_norm_kernel,
        grid=(m // tr,),
        in_specs=[pl.BlockSpec((tr, d), lambda i: (i, 0)), pl.BlockSpec((1, d), lambda i: (0, 0))],
        out_specs=pl.BlockSpec((tr, d), lambda i: (i, 0)),
        out_shape=jax.ShapeDtypeStruct((m, d), F32),
        compiler_params=_params("parallel"),
        name="final_norm",
    )(x, w.reshape(1, d))


def _rope_tables(pos, dh, dqk):
    rope_dim = dqk // 4
    half = rope_dim // 2
    inv_freq = ROPE_THETA ** (-jnp.arange(half, dtype=F32) * (2.0 / rope_dim))
    ang = pos.astype(F32)[:, None] * inv_freq[None, :]
    cos, sin = jnp.cos(ang), jnp.sin(ang)
    n = pos.shape[0]
    pad = jnp.zeros((n, dqk - rope_dim), F32)
    comp_cos = jnp.concatenate([cos, cos, pad + 1.0], axis=1)
    comp_sa = jnp.concatenate([-sin, jnp.zeros_like(sin), pad], axis=1)
    comp_sb = jnp.concatenate([jnp.zeros_like(sin), sin, pad], axis=1)
    reps = dh // dqk
    return tuple(jnp.tile(t, (1, reps)) for t in (comp_cos, comp_sa, comp_sb))


def _tail_tile(state, width):
    nb, k, _ = state.shape
    return jnp.concatenate([jnp.zeros((nb, SUBLANES - k, width), F32), state.astype(F32)], axis=1)


def _pad_time(x, tp):
    nb, t = x.shape[:2]
    return jnp.pad(x, [(0, 0), (0, tp - t)] + [(0, 0)] * (x.ndim - 2))


def kernel(x_prompt, x_sample, c_prompt, c_sample, state_conv_a, state_gdn_conv, state_gdn, state_gla, cache_k, cache_v, page_table, w_ada, b_ada, w_in, conv_a_w, gdn_conv_w, gdn_a_log, gdn_dt_bias, gdn_norm_w, diff_lambda, diff_norm_w, gla_w_gk2, gla_b_gk, gla_norm_w, w_branch, w_out, final_norm_w):
    nbp, tpp, d = x_prompt.shape
    nbs, tvs, _ = x_sample.shape
    depth = w_in.shape[0]
    bw = d // 4
    gdn_h = gdn_a_log.shape[1]
    diff_h = cache_k.shape[3]
    dh = cache_k.shape[4]
    dqk = dh // 2
    gla_h = state_gla.shape[2]
    gla_dk = state_gla.shape[3]
    rank = gla_w_gk2.shape[1]
    conv_a_k = conv_a_w.shape[1]
    gdn_k = gdn_conv_w.shape[1]
    past_len = page_table.shape[1] * cache_k.shape[2]
    tps = SUBLANES
    assert tvs <= tps and 2 * gdn_h + rank <= LANES and bw // gdn_h == LANES and dh == LANES

    off_gdn = 4 * bw
    off_side = off_gdn + 4 * bw
    off_diff = off_side + 2 * gdn_h
    off_gla = off_diff + 4 * bw
    off_lr = off_gla + 3 * bw
    off_merge = off_lr + rank
    w_ab = _repack(w_in, 0, off_side, bw)
    w_cd = _repack(w_in, off_diff, off_lr - off_diff, bw)
    w_side = _side_weights(w_in, off_side, 2 * gdn_h, off_lr, rank)
    w_merge = _repack(w_in, off_merge, 4 * d, bw)
    w_branch_b = w_branch.astype(BF16)
    w_out_b = w_out.astype(BF16)
    wgk_pad = jnp.concatenate(
        [jnp.zeros((depth, 2 * gdn_h, gla_h * gla_dk), F32), gla_w_gk2,
         jnp.zeros((depth, LANES - 2 * gdn_h - rank, gla_h * gla_dk), F32)], axis=1).astype(BF16)
    lane_pad = jnp.zeros((depth, LANES - gdn_h), F32)
    alog_pad = jnp.concatenate([gdn_a_log, lane_pad], axis=1)
    dtb_pad = jnp.concatenate([gdn_dt_bias, lane_pad], axis=1)

    pool, page = cache_k.shape[1:3]
    cache_k_rows = cache_k.reshape(depth, pool, page * diff_h, dh)
    cache_v_rows = cache_v.reshape(depth, pool, page * diff_h, dh)

    n_c = nbp + nbs
    c_rows = -(-n_c // SUBLANES) * SUBLANES
    c_all = jnp.concatenate([c_prompt, c_sample, jnp.zeros((c_rows - n_c, d), F32)], axis=0)
    mod = _ada(c_all, w_ada, b_ada)

    groups = []
    groups.append(dict(
        nb=nbp, tp=tpp, tv=tpp, x=x_prompt.reshape(nbp * tpp, d), mod_rows=slice(0, nbp), per_row=False,
        conv_a=jnp.zeros((depth, nbp, conv_a_k - 1, bw), F32),
        gdn_conv=jnp.zeros((depth, nbp, gdn_k - 1, 3 * bw), F32),
        gdn_s=jnp.zeros((depth, nbp) + state_gdn.shape[2:], F32),
        gla_s=jnp.zeros((depth, nbp) + state_gla.shape[2:], F32),
        tabs=_rope_tables(jnp.arange(tpp, dtype=jnp.int32), dh, dqk), paged=False))
    groups.append(dict(
        nb=nbs, tp=tps, tv=tvs, x=_pad_time(x_sample, tps).reshape(nbs * tps, d), mod_rows=slice(nbp, n_c),
        per_row=True, conv_a=state_conv_a, gdn_conv=state_gdn_conv, gdn_s=state_gdn, gla_s=state_gla,
        tabs=_rope_tables(past_len + jnp.arange(tps, dtype=jnp.int32), dh, dqk), paged=True))

    results = []
    for g in groups:
        nb, tp, tv = g["nb"], g["tp"], g["tv"]
        x = g["x"]
        outs = [[] for _ in range(6)]
        for l in range(depth):
            m_l = mod[l, g["mod_rows"]]
            if g["per_row"]:
                m_l = jnp.repeat(m_l, tp, axis=0).reshape(1, nb * tp, 3 * d)
            else:
                m_l = m_l.reshape(nb, 1, 3 * d)
            shift, scale, gate = m_l[..., :d], m_l[..., d:2 * d], m_l[..., 2 * d:]
            h = _prenorm(x, scale, shift, tp)
            z_ab = _matmul(h, w_ab, l)
            z_c = _matmul(h, w_cd, l, 0, 4 * bw)
            z_d = _matmul(h, w_cd, l, 4 * bw, 3 * bw)
            zs = _matmul(h, w_side, l, tn_pref=LANES)

            out_a, tail_a = _conva(z_ab, _tail_tile(g["conv_a"][l], bw), conv_a_w[l], nb, tp, bw)
            out_b, tail_b, s_gdn = _gdn(
                z_ab, zs, _tail_tile(g["gdn_conv"][l], 3 * bw), gdn_conv_w[l], alog_pad[l:l + 1],
                dtb_pad[l:l + 1], gdn_norm_w[l:l + 1], g["gdn_s"][l].astype(F32), nb, tp, tv, bw, gdn_h, 1)
            q_r, k_r, v_r, new_k, new_v = _rope(z_c, g["tabs"], nb, tp, bw, diff_h, 0)
            lam_init = 0.8 - 0.6 * math.exp(-0.3 * l)
            if g["paged"]:
                out_c = _decode(q_r, k_r, v_r, z_c, cache_k_rows, cache_v_rows, page_table, l, diff_lambda[l],
                                diff_norm_w[l:l + 1], nb, tv, bw, diff_h, lam_init, 3)
            else:
                out_c = _flash(q_r, k_r, v_r, z_c, diff_lambda[l], diff_norm_w[l:l + 1], nb, tp, bw, diff_h,
                               lam_init, 3 * diff_h)
            out_d, s_gla_t = _gla(
                z_d, zs, wgk_pad[l], gla_b_gk[l:l + 1], gla_norm_w[l:l + 1],
                jnp.swapaxes(g["gla_s"][l].astype(F32), -1, -2), nb, tp, tv, bw, gla_h, 0)

            merged = _merge(h, (out_a, out_b, out_c, out_d), w_merge, w_branch_b, l)
            x = _outproj(merged, w_out_b, l, x, gate, tp)

            last = tv - (tp - SUBLANES)
            outs[0].append(tail_a[:, last - (conv_a_k - 1):last])
            outs[1].append(tail_b[:, last - (gdn_k - 1):last])
            outs[2].append(s_gdn)
            outs[3].append(jnp.swapaxes(s_gla_t, -1, -2))
            outs[4].append(new_k.reshape(nb, tp, diff_h, dh)[:, :tv])
            outs[5].append(new_v.reshape(nb, tp, diff_h, dh)[:, :tv])
        y = _final_norm(x, final_norm_w).reshape(nb, tp, d)[:, :tv]
        results.append((y, *[jnp.stack(o) for o in outs]))

    (yp, *sp), (ys, *ss) = results
    return (yp, ys, *sp, *ss)
```

```python
import functools
import math

import jax
import jax.numpy as jnp
import numpy as np
from jax import lax
from jax.experimental import pallas as pl
from jax.experimental.pallas import tpu as pltpu

F32 = jnp.float32
BF16 = jnp.bfloat16
EPS = 1e-6
GDN_CHUNK = 64
GDN_CHUNKS_PER_STEP = 4
DECODE_PAGES_PER_STEP = 8
FLASH_BLOCK = 256
GLA_CHUNK = 32
GLA_CHUNKS_PER_STEP = 8
GLA_NORMALIZER = 16.0
ROPE_THETA = 500000.0
SUBLANES = 8
LANES = 128
VMEM_LIMIT = 56 * 1024 * 1024
HIGHEST = lax.Precision.HIGHEST

_NT = (((1,), (1,)), ((), ()))
_TN = (((0,), (0,)), ((), ()))
_BNN = (((2,), (1,)), ((0,), (0,)))
_BNT = (((2,), (2,)), ((0,), (0,)))
_BTN = (((1,), (1,)), ((0,), (0,)))


def _params(*sem):
    return pltpu.CompilerParams(dimension_semantics=sem, vmem_limit_bytes=VMEM_LIMIT)


def _silu(x):
    return x * jax.nn.sigmoid(x)


def _softplus(x):
    return jnp.maximum(x, 0.0) + jnp.log(1.0 + jnp.exp(-jnp.abs(x)))


def _dot(a, b):
    return jnp.dot(a.astype(BF16), b.astype(BF16), preferred_element_type=F32)


def _dot_nt(a, b):
    return lax.dot_general(a.astype(BF16), b.astype(BF16), _NT, preferred_element_type=F32)


def _dot_tn(a, b):
    return lax.dot_general(a.astype(BF16), b.astype(BF16), _TN, preferred_element_type=F32)


def _bdot(a, b, dims=_BNN):
    return lax.dot_general(a.astype(BF16), b.astype(BF16), dims, preferred_element_type=F32)


def _tile(n, pref):
    if n <= pref:
        return n
    t = pref
    while n % t:
        t //= 2
    return t


def _ada_kernel(c_ref, w_ref, b_ref, o_ref):
    c = c_ref[...]
    o_ref[0] = _dot(_silu(c), w_ref[0]) + b_ref[0]


def _ada(c_all, w_ada, b_ada):
    depth, d, n = w_ada.shape
    rows = c_all.shape[0]
    tn = _tile(n, 1024)
    return pl.pallas_call(
        _ada_kernel,
        grid=(depth, n // tn),
        in_specs=[
            pl.BlockSpec((rows, d), lambda l, j: (0, 0)),
            pl.BlockSpec((1, d, tn), lambda l, j: (l, 0, j)),
            pl.BlockSpec((1, 1, tn), lambda l, j: (l, 0, j)),
        ],
        out_specs=pl.BlockSpec((1, rows, tn), lambda l, j: (l, 0, j)),
        out_shape=jax.ShapeDtypeStruct((depth, rows, n), F32),
        compiler_params=_params("parallel", "parallel"),
        name="ada_mod",
    )(c_all, w_ada, b_ada.reshape(depth, 1, n))


def _repack_kernel(a_ref, o_ref):
    o_ref[...] = a_ref[0].T.astype(BF16)


def _repack(w_t, start, width):
    depth, _, k = w_t.shape
    tn = _tile(width, 1024)
    tk = _tile(k, 1024)
    return pl.pallas_call(
        _repack_kernel,
        grid=(depth, k // tk, width // tn),
        in_specs=[pl.BlockSpec((pl.Element(1), pl.Element(tn), pl.Element(tk)),
                               lambda l, i, j: (l, pl.multiple_of(start + j * tn, SUBLANES),
                                                pl.multiple_of(i * tk, LANES)))],
        out_specs=pl.BlockSpec((None, tk, tn), lambda l, i, j: (l, i, j)),
        out_shape=jax.ShapeDtypeStruct((depth, k, width), BF16),
        compiler_params=_params("parallel", "parallel", "parallel"),
        name="repack",
    )(w_t)


def _side_kernel(a_ref, b_ref, o_ref, *, n_a, n_b):
    lane = lax.broadcasted_iota(jnp.int32, o_ref.shape, 1)
    o_ref[...] = jnp.where(lane < n_a, a_ref[0].T, jnp.where(lane < n_a + n_b, b_ref[0].T, 0.0)).astype(BF16)


def _side_weights(w_t, start_a, n_a, start_b, n_b):
    depth, _, k = w_t.shape
    assert n_a + n_b <= LANES
    tk = _tile(k, 512)
    return pl.pallas_call(
        functools.partial(_side_kernel, n_a=n_a, n_b=n_b),
        grid=(depth, k // tk),
        in_specs=[
            pl.BlockSpec((pl.Element(1), pl.Element(LANES), pl.Element(tk)), lambda l, i: (l, start_a, i * tk)),
            pl.BlockSpec((pl.Element(1), pl.Element(LANES), pl.Element(tk)),
                         lambda l, i: (l, start_b - n_a, i * tk)),
        ],
        out_specs=pl.BlockSpec((None, tk, LANES), lambda l, i: (l, i, 0)),
        out_shape=jax.ShapeDtypeStruct((depth, k, LANES), BF16),
        compiler_params=_params("parallel", "parallel"),
        name="repack_side",
    )(w_t, w_t)


def _prenorm_kernel(x_ref, sc_ref, sh_ref, h_ref):
    x = x_ref[...]
    y = x * lax.rsqrt(jnp.mean(x * x, axis=-1, keepdims=True) + EPS)
    h_ref[...] = (y * (1.0 + sc_ref[0]) + sh_ref[0]).astype(BF16)


def _prenorm(x, scale, shift, rows_per_mod):
    m, d = x.shape
    r = scale.shape[1]
    tr = r if r > 1 else _tile(rows_per_mod, 256)
    per = rows_per_mod // tr if r == 1 else 1
    return pl.pallas_call(
        _prenorm_kernel,
        grid=(m // tr,),
        in_specs=[
            pl.BlockSpec((tr, d), lambda i: (i, 0)),
            pl.BlockSpec((1, r, d), lambda i: (i // per, 0, 0)),
            pl.BlockSpec((1, r, d), lambda i: (i // per, 0, 0)),
        ],
        out_specs=pl.BlockSpec((tr, d), lambda i: (i, 0)),
        out_shape=jax.ShapeDtypeStruct((m, d), BF16),
        compiler_params=_params("parallel"),
        name="prenorm",
    )(x, scale, shift)


def _mm_kernel(a_ref, w_ref, o_ref):
    o_ref[...] = jnp.dot(a_ref[...], w_ref[...], preferred_element_type=F32).astype(o_ref.dtype)


def _matmul(a, w, layer, col0=0, n=None, out_dtype=F32, tm_pref=1024, tn_pref=512):
    m, k = a.shape
    n = w.shape[2] if n is None else n
    tm = _tile(m, tm_pref)
    tn = _tile(n, tn_pref)
    assert col0 % tn == 0
    return pl.pallas_call(
        _mm_kernel,
        grid=(m // tm, n // tn),
        in_specs=[
            pl.BlockSpec((tm, k), lambda i, j: (i, 0)),
            pl.BlockSpec((None, k, tn), lambda i, j: (layer, 0, col0 // tn + j)),
        ],
        out_specs=pl.BlockSpec((tm, tn), lambda i, j: (i, j)),
        out_shape=jax.ShapeDtypeStruct((m, n), out_dtype),
        compiler_params=_params("parallel", "arbitrary"),
        name="in_proj",
    )(a, w)


def _causal_conv(carry_ref, st_ref, w_ref, u):
    n = u.shape[0]

    @pl.when(pl.program_id(1) == 0)
    def _():
        carry_ref[...] = st_ref[0]

    ext = jnp.concatenate([carry_ref[...], u], axis=0)
    w = w_ref[...]
    taps = w.shape[0]
    y = w[taps - 1:taps] * u
    for k in range(1, taps):
        y = y + w[taps - 1 - k:taps - k] * pltpu.roll(ext, k, 0)[SUBLANES:SUBLANES + n]
    tail = u[n - SUBLANES:n]
    carry_ref[...] = tail
    return y, tail


def _valid_rows(c, t_valid):
    t = pl.program_id(1)
    row = lax.broadcasted_iota(jnp.int32, (c, 1), 0) + t * c
    return row < t_valid


def _tri(c):
    row = lax.broadcasted_iota(jnp.int32, (c, c), 0)
    col = lax.broadcasted_iota(jnp.int32, (c, c), 1)
    return row, col


def _conva_kernel(z_ref, st_ref, w_ref, o_ref, tail_ref, carry, *, bw):
    z = z_ref[...]
    a_h, a_b, a_c, a_z = (z[:, i * bw:(i + 1) * bw] for i in range(4))
    y, tail = _causal_conv(carry, st_ref, w_ref, a_c * a_h)
    o_ref[...] = (a_b * y * _silu(a_z)).astype(BF16)
    tail_ref[0] = tail


def _conva(z, state_tile, w, nb, tp, bw):
    tt = _tile(tp, 256)
    nt = tp // tt
    return pl.pallas_call(
        functools.partial(_conva_kernel, bw=bw),
        grid=(nb, nt),
        in_specs=[
            pl.BlockSpec((tt, 4 * bw), lambda b, t: (b * nt + t, 0)),
            pl.BlockSpec((1, SUBLANES, bw), lambda b, t: (b, 0, 0)),
            pl.BlockSpec(w.shape, lambda b, t: (0, 0)),
        ],
        out_specs=[
            pl.BlockSpec((tt, bw), lambda b, t: (b * nt + t, 0)),
            pl.BlockSpec((1, SUBLANES, bw), lambda b, t: (b, 0, 0)),
        ],
        out_shape=[
            jax.ShapeDtypeStruct((nb * tp, bw), BF16),
            jax.ShapeDtypeStruct((nb, SUBLANES, bw), F32),
        ],
        scratch_shapes=[pltpu.VMEM((SUBLANES, bw), F32)],
        compiler_params=_params("parallel", "arbitrary"),
        name="branch_conv",
    )(z, state_tile, w)


INVERSE_BASE = 8


def _inverse_masks(row, col, c):
    base = min(INVERSE_BASE, c)
    masks = [(row // base) == (col // base)]
    b = base
    while b < c:
        masks.append(((row // (2 * b)) == (col // (2 * b))) & ((row // b) != (col // b)))
        b *= 2
    return masks


def _unit_lower_inverse(lmat, eye, masks, mm):
    base = min(INVERSE_BASE, lmat.shape[-1])
    p = jnp.where(masks[0], lmat, 0.0)
    x = eye - p
    for _ in range(max(0, int(math.log2(base)) - 1)):
        p = mm(p, p)
        x = x + mm(x, p)
    for m in masks[1:]:
        x = x - mm(mm(x, jnp.where(m, lmat, 0.0)), x)
    return x


def _gdn_kernel(z_ref, zs_ref, cst_ref, cw_ref, alog_ref, dtb_ref, nw_ref, s0_ref,
                o_ref, tail_ref, sout_ref, carry, s_scr, *, bw, heads, t_valid, c):
    t = pl.program_id(1)
    nt = pl.num_programs(1)

    @pl.when(t == 0)
    def _():
        s_scr[...] = s0_ref[0]

    z = z_ref[...]
    r = z.shape[0]
    n_chunks = r // c
    dk = bw // heads
    gz = z[:, 3 * bw:4 * bw]
    y, tail = _causal_conv(carry, cst_ref, cw_ref, z[:, :3 * bw])
    tail_ref[0] = tail
    qkv = _silu(y)

    valid = _valid_rows(r, t_valid).astype(F32)
    zs = zs_ref[...]
    g = valid * (-jnp.exp(alog_ref[...]) * _softplus(zs + dtb_ref[...]))
    beta = valid * jax.nn.sigmoid(zs)
    row_r, col_r = _tri(r)
    same_chunk = (row_r // c) == (col_r // c)
    gsum = jnp.dot(((row_r >= col_r) & same_chunk).astype(F32), g, precision=HIGHEST,
                   preferred_element_type=F32)
    row, col = _tri(c)
    incl = row >= col
    strict = row > col
    eye = (row == col).astype(F32)
    inv_masks = _inverse_masks(row, col, c)
    r128, c128 = _tri(LANES)
    eye128 = (r128 == c128).astype(F32)
    rows = [slice(ci * c, (ci + 1) * c) for ci in range(n_chunks)]
    gsum_t = [lax.dot_general(eye128, gsum[rs], _NT, precision=HIGHEST, preferred_element_type=F32)
              for rs in rows]

    def per(fn):
        return jnp.stack([fn(ci, rows[ci], h) for ci in range(n_chunks) for h in range(heads)])

    q = per(lambda ci, rs, h: qkv[rs, h * dk:(h + 1) * dk])
    k = per(lambda ci, rs, h: qkv[rs, bw + h * dk:bw + (h + 1) * dk])
    v = per(lambda ci, rs, h: qkv[rs, 2 * bw + h * dk:2 * bw + (h + 1) * dk])
    gcol = per(lambda ci, rs, h: jnp.broadcast_to(gsum[rs, h:h + 1], (c, dk)))
    bcol = per(lambda ci, rs, h: jnp.broadcast_to(beta[rs, heads + h:heads + h + 1], (c, dk)))
    vcol = per(lambda ci, rs, h: jnp.broadcast_to(valid[rs], (c, dk)))
    grow = per(lambda ci, rs, h: gsum_t[ci][h:h + 1, :])

    qn = q * lax.rsqrt(jnp.sum(q * q, axis=-1, keepdims=True) + EPS) * (dk ** -0.5)
    kn = vcol * (k * lax.rsqrt(jnp.sum(k * k, axis=-1, keepdims=True) + EPS))
    eg = jnp.exp(gcol)
    decay = jnp.exp(jnp.where(incl, gcol[:, :, :c] - grow, 0.0))
    kb = kn * bcol
    kq = _bdot(jnp.concatenate([kb, qn], axis=1), kn, _BNT)
    lmat = jnp.where(strict, kq[:, :c] * decay, 0.0)
    qk = jnp.where(incl, kq[:, c:] * decay, 0.0)
    tinv = _unit_lower_inverse(lmat, eye, inv_masks, _bdot)
    sol = _bdot(tinv, jnp.concatenate([v * bcol, kb * eg], axis=2))
    u = sol[:, :, :dk]
    wq = jnp.concatenate([sol[:, :, dk:], qn * eg], axis=1)
    g_last = gcol[:, c - 1:c, :]
    k_dec = kn * jnp.exp(g_last - gcol)
    s_decay = jnp.exp(g_last)

    s = s_scr[...]
    for ci in range(n_chunks):
        sel = slice(ci * heads, (ci + 1) * heads)
        ws = _bdot(wq[sel], s)
        v_new = u[sel] - ws[:, :c]
        o = ws[:, c:] + _bdot(qk[sel], v_new)
        s = s * s_decay[sel] + _bdot(k_dec[sel], v_new, _BTN)
        on = o * lax.rsqrt(jnp.mean(o * o, axis=-1, keepdims=True) + EPS) * nw_ref[...]
        for h in range(heads):
            sl = slice(h * dk, (h + 1) * dk)
            o_ref[rows[ci], sl] = (on[h] * _silu(gz[rows[ci], sl])).astype(BF16)
    s_scr[...] = s

    @pl.when(t == nt - 1)
    def _():
        sout_ref[0] = s


def _gdn(z, zs, conv_tile, conv_w, alog, dtb, norm_w, s0, nb, tp, t_valid, bw, heads, col_block):
    c = min(GDN_CHUNK, tp)
    r = min(GDN_CHUNK * GDN_CHUNKS_PER_STEP, tp)
    nt = tp // r
    dk = bw // heads
    return pl.pallas_call(
        functools.partial(_gdn_kernel, bw=bw, heads=heads, t_valid=t_valid, c=c),
        grid=(nb, nt),
        in_specs=[
            pl.BlockSpec((r, 4 * bw), lambda b, t: (b * nt + t, col_block)),
            pl.BlockSpec((r, LANES), lambda b, t: (b * nt + t, 0)),
            pl.BlockSpec((1, SUBLANES, 3 * bw), lambda b, t: (b, 0, 0)),
            pl.BlockSpec(conv_w.shape, lambda b, t: (0, 0)),
            pl.BlockSpec((1, LANES), lambda b, t: (0, 0)),
            pl.BlockSpec((1, LANES), lambda b, t: (0, 0)),
            pl.BlockSpec((1, dk), lambda b, t: (0, 0)),
            pl.BlockSpec((1, heads, dk, dk), lambda b, t: (b, 0, 0, 0)),
        ],
        out_specs=[
            pl.BlockSpec((r, bw), lambda b, t: (b * nt + t, 0)),
            pl.BlockSpec((1, SUBLANES, 3 * bw), lambda b, t: (b, 0, 0)),
            pl.BlockSpec((1, heads, dk, dk), lambda b, t: (b, 0, 0, 0)),
        ],
        out_shape=[
            jax.ShapeDtypeStruct((nb * tp, bw), BF16),
            jax.ShapeDtypeStruct((nb, SUBLANES, 3 * bw), F32),
            jax.ShapeDtypeStruct((nb, heads, dk, dk), F32),
        ],
        scratch_shapes=[pltpu.VMEM((SUBLANES, 3 * bw), F32), pltpu.VMEM((heads, dk, dk), F32)],
        compiler_params=_params("parallel", "arbitrary"),
        name="branch_gdn",
    )(z, zs, conv_tile, conv_w, alog, dtb, norm_w, s0)


def _gla_kernel(z_ref, zs_ref, wgk_ref, bgk_ref, nw_ref, s0_ref, o_ref, sout_ref, s_scr,
                *, bw, heads, t_valid, c):
    t = pl.program_id(1)
    nt = pl.num_programs(1)

    @pl.when(t == 0)
    def _():
        s_scr[...] = s0_ref[0]

    z = z_ref[...]
    r = z.shape[0]
    n_chunks = r // c
    dk = bw // (2 * heads)
    dv = bw // heads
    valid = _valid_rows(r, t_valid).astype(F32)
    gk = valid * (-_softplus(-(_dot(zs_ref[...], wgk_ref[...]) + bgk_ref[...])) / GLA_NORMALIZER)
    row_r, col_r = _tri(r)
    same_chunk = (row_r // c) == (col_r // c)
    gsum = jnp.dot(((row_r >= col_r) & same_chunk).astype(F32), gk, precision=HIGHEST,
                   preferred_element_type=F32)
    row, col = _tri(c)
    incl = row >= col
    lz = z[:, 2 * bw:3 * bw]
    rows = [slice(ci * c, (ci + 1) * c) for ci in range(n_chunks)]

    def per(fn):
        return jnp.stack([fn(rows[ci], h) for ci in range(n_chunks) for h in range(heads)])

    q = per(lambda rs, h: z[rs, h * dk:(h + 1) * dk]) * (dk ** -0.5)
    k = per(lambda rs, h: z[rs, bw // 2 + h * dk:bw // 2 + (h + 1) * dk] * valid[rs])
    v = per(lambda rs, h: z[rs, bw + h * dv:bw + (h + 1) * dv])
    g = per(lambda rs, h: gsum[rs, h * dk:(h + 1) * dk])
    q_dec = q * jnp.exp(g)
    attn = jnp.where(incl, _bdot(q_dec, k * jnp.exp(-g), _BNT), 0.0)
    o_intra = _bdot(attn, v)
    g_last = g[:, c - 1:c, :]
    k_dec = k * jnp.exp(g_last - g)
    s_decay = jnp.exp(g_last)

    s_t = s_scr[...]
    for ci in range(n_chunks):
        sel = slice(ci * heads, (ci + 1) * heads)
        o = o_intra[sel] + _bdot(q_dec[sel], s_t, _BNT)
        s_t = s_t * s_decay[sel] + _bdot(v[sel], k_dec[sel], _BTN)
        on = o * lax.rsqrt(jnp.mean(o * o, axis=-1, keepdims=True) + EPS) * nw_ref[...]
        for h in range(heads):
            sl = slice(h * dv, (h + 1) * dv)
            o_ref[rows[ci], sl] = (on[h] * _silu(lz[rows[ci], sl])).astype(BF16)
    s_scr[...] = s_t

    @pl.when(t == nt - 1)
    def _():
        sout_ref[0] = s_t


def _gla(z, zs, wgk_pad, bgk, norm_w, s0_t, nb, tp, t_valid, bw, heads, col_block):
    c = min(GLA_CHUNK, tp)
    r = min(GLA_CHUNK * GLA_CHUNKS_PER_STEP, tp)
    nt = tp // r
    dk = bw // (2 * heads)
    dv = bw // heads
    return pl.pallas_call(
        functools.partial(_gla_kernel, bw=bw, heads=heads, t_valid=t_valid, c=c),
        grid=(nb, nt),
        in_specs=[
            pl.BlockSpec((r, 3 * bw), lambda b, t: (b * nt + t, col_block)),
            pl.BlockSpec((r, LANES), lambda b, t: (b * nt + t, 0)),
            pl.BlockSpec(wgk_pad.shape, lambda b, t: (0, 0)),
            pl.BlockSpec(bgk.shape, lambda b, t: (0, 0)),
            pl.BlockSpec((1, dv), lambda b, t: (0, 0)),
            pl.BlockSpec((1, heads, dv, dk), lambda b, t: (b, 0, 0, 0)),
        ],
        out_specs=[
            pl.BlockSpec((r, bw), lambda b, t: (b * nt + t, 0)),
            pl.BlockSpec((1, heads, dv, dk), lambda b, t: (b, 0, 0, 0)),
        ],
        out_shape=[
            jax.ShapeDtypeStruct((nb * tp, bw), BF16),
            jax.ShapeDtypeStruct((nb, heads, dv, dk), F32),
        ],
        scratch_shapes=[pltpu.VMEM((heads, dv, dk), F32)],
        compiler_params=_params("parallel", "arbitrary"),
        name="branch_gla",
    )(z, zs, wgk_pad, bgk, norm_w, s0_t)


def _rope_kernel(z_ref, cos_ref, sa_ref, sb_ref, q_ref, k_ref, v_ref, nk_ref, nv_ref,
                 *, bw, heads, half, scale):
    z = z_ref[...]
    cos = cos_ref[...]
    sa = sa_ref[...]
    sb = sb_ref[...]
    dh = bw // heads

    def rope(x):
        return x * cos + pltpu.roll(x, dh - half, 1) * sa + pltpu.roll(x, half, 1) * sb

    for h in range(heads):
        sl = slice(h * dh, (h + 1) * dh)
        q = rope(z[:, sl])
        k = rope(z[:, bw + h * dh:bw + (h + 1) * dh])
        q_ref[:, sl] = (q * scale).astype(BF16)
        k_ref[:, sl] = k.astype(BF16)
        nk_ref[:, sl] = k
    v = z[:, 2 * bw:3 * bw]
    v_ref[...] = v.astype(BF16)
    nv_ref[...] = v


def _rope(z, tabs, nb, tp, bw, heads, col_block):
    tt = _tile(tp, 256)
    nt = tp // tt
    dh = bw // heads
    dqk = dh // 2
    kern = functools.partial(_rope_kernel, bw=bw, heads=heads, half=dqk // 8, scale=dqk ** -0.5)
    rows = nb * tp
    spec = pl.BlockSpec((tt, bw), lambda i: (i, 0))
    tspec = pl.BlockSpec((tt, dh), lambda i: (i % nt, 0))
    return pl.pallas_call(
        kern,
        grid=(rows // tt,),
        in_specs=[pl.BlockSpec((tt, 4 * bw), lambda i: (i, col_block)), tspec, tspec, tspec],
        out_specs=[spec] * 5,
        out_shape=[jax.ShapeDtypeStruct((rows, bw), BF16)] * 3 + [jax.ShapeDtypeStruct((rows, bw), F32)] * 2,
        compiler_params=_params("parallel"),
        name="diff_rope",
    )(z, *tabs)


def _lambda(lam_ref, lam_init):
    lf = lam_ref[...]
    a = jnp.sum(lf[0:1] * lf[1:2], axis=-1, keepdims=True)
    b = jnp.sum(lf[2:3] * lf[3:4], axis=-1, keepdims=True)
    return jnp.exp(a) - jnp.exp(b) + lam_init


def _stack_components(q, dqk):
    lane = lax.broadcasted_iota(jnp.int32, q.shape, 1)
    zero = jnp.zeros_like(q)
    return jnp.concatenate([jnp.where(lane < dqk, q, zero), jnp.where(lane >= dqk, q, zero)], axis=0)


def _online_update(s, v, m, l, acc, mm=_dot):
    m_new = jnp.maximum(m, jnp.max(s, axis=-1, keepdims=True))
    alpha = jnp.exp(m - m_new)
    p = jnp.exp(s - m_new)
    return m_new, alpha * l + jnp.sum(p, axis=-1, keepdims=True), alpha * acc + mm(p, v)


def _attn_finish(acc, l, n, lam, lam_init, nw, dz):
    o = acc[:n] / l[:n] - lam * (acc[n:] / l[n:])
    on = o * lax.rsqrt(jnp.mean(o * o, axis=-1, keepdims=True) + EPS) * nw * (1.0 - lam_init)
    return (on * _silu(dz)).astype(BF16)


def _flash_kernel(q_ref, k_ref, v_ref, dz_ref, lam_ref, nw_ref, o_ref, *, dqk, lam_init, hb):
    i = pl.program_id(2)
    tq = q_ref.shape[0]
    dh = 2 * dqk
    heads = [slice(h * dh, (h + 1) * dh) for h in range(hb)]
    qs = jnp.stack([_stack_components(q_ref[:, sl], dqk) for sl in heads])

    def block(j):
        start = pl.multiple_of(j * tq, tq)
        kb = k_ref[pl.ds(start, tq), :]
        vb = v_ref[pl.ds(start, tq), :]
        s = _bdot(qs, jnp.stack([kb[:, sl] for sl in heads]), _BNT)
        return s, jnp.stack([vb[:, sl] for sl in heads])

    def body(j, carry):
        s, v = block(j)
        return _online_update(s, v, *carry, mm=_bdot)

    init = (jnp.full((hb, 2 * tq, 1), -jnp.inf, F32), jnp.zeros((hb, 2 * tq, 1), F32),
            jnp.zeros((hb, 2 * tq, dh), F32))
    carry = lax.fori_loop(0, i, body, init)
    s, v = block(i)
    row = lax.broadcasted_iota(jnp.int32, s.shape, 1) % tq
    col = lax.broadcasted_iota(jnp.int32, s.shape, 2)
    m, l, acc = _online_update(jnp.where(col <= row, s, -jnp.inf), v, *carry, mm=_bdot)
    lam = _lambda(lam_ref, lam_init)
    for h, sl in enumerate(heads):
        o_ref[:, sl] = _attn_finish(acc[h], l[h], tq, lam, lam_init, nw_ref[...], dz_ref[:, sl])


def _flash(q, k, v, z, lam, norm_w, nb, tp, bw, heads, lam_init, dz_col0):
    dh = bw // heads
    hb = 2 if heads % 2 == 0 else 1
    tq = _tile(tp, FLASH_BLOCK)
    nq = tp // tq
    kern = functools.partial(_flash_kernel, dqk=dh // 2, lam_init=lam_init, hb=hb)
    return pl.pallas_call(
        kern,
        grid=(nb, heads // hb, nq),
        in_specs=[
            pl.BlockSpec((tq, hb * dh), lambda b, h, i: (b * nq + i, h)),
            pl.BlockSpec((tp, hb * dh), lambda b, h, i: (b, h)),
            pl.BlockSpec((tp, hb * dh), lambda b, h, i: (b, h)),
            pl.BlockSpec((tq, hb * dh), lambda b, h, i: (b * nq + i, dz_col0 // hb + h)),
            pl.BlockSpec(lam.shape, lambda b, h, i: (0, 0)),
            pl.BlockSpec((1, dh), lambda b, h, i: (0, 0)),
        ],
        out_specs=pl.BlockSpec((tq, hb * dh), lambda b, h, i: (b * nq + i, h)),
        out_shape=jax.ShapeDtypeStruct((nb * tp, bw), BF16),
        compiler_params=_params("parallel", "parallel", "arbitrary"),
        name="diff_flash",
    )(q, k, v, z, lam, norm_w)


def _decode_kernel(pt_ref, q_ref, *refs, heads, dqk, t_valid, lam_init, pps):
    kc_refs, vc_refs = refs[:pps], refs[pps:2 * pps]
    kn_ref, vn_ref, dz_ref, lam_ref, nw_ref, o_ref, m_scr, l_scr, acc_scr = refs[2 * pps:]
    j = pl.program_id(1)
    nj = pl.num_programs(1)
    n = q_ref.shape[0]
    dh = 2 * dqk
    rows = 2 * n * heads
    head_lanes = [slice(h * dh, (h + 1) * dh) for h in range(heads)]

    @pl.when(j == 0)
    def _():
        m_scr[...] = jnp.full(m_scr.shape, -jnp.inf, F32)
        l_scr[...] = jnp.zeros(l_scr.shape, F32)
        acc_scr[...] = jnp.zeros(acc_scr.shape, F32)

    q_all = jnp.concatenate([_stack_components(q_ref[:, sl].astype(F32), dqk) for sl in head_lanes], axis=0)
    keys = kc_refs[0].shape[0]
    row_head = lax.broadcasted_iota(jnp.int32, (rows, keys), 0) // (2 * n)
    col_head = lax.broadcasted_iota(jnp.int32, (rows, keys), 1) % heads
    same_head = row_head == col_head
    state = (m_scr[...], l_scr[...], acc_scr[...])
    for kc_ref, vc_ref in zip(kc_refs, vc_refs):
        s = jnp.where(same_head, _dot_nt(q_all, kc_ref[...]), -jnp.inf)
        state = _online_update(s, vc_ref[...], *state)
    m_scr[...], l_scr[...], acc_scr[...] = state

    @pl.when(j == nj - 1)
    def _():
        k_new = jnp.concatenate([kn_ref[:, sl].astype(F32) for sl in head_lanes], axis=0)
        v_new = jnp.concatenate([vn_ref[:, sl].astype(F32) for sl in head_lanes], axis=0)
        r2 = lax.broadcasted_iota(jnp.int32, (rows, n * heads), 0)
        c2 = lax.broadcasted_iota(jnp.int32, (rows, n * heads), 1)
        visible = ((r2 // (2 * n)) == (c2 // n)) & ((c2 % n) <= (r2 % n)) & ((c2 % n) < t_valid)
        s = jnp.where(visible, _dot_nt(q_all, k_new), -jnp.inf)
        m, l, acc = _online_update(s, v_new, *state)
        lam = _lambda(lam_ref, lam_init)
        for h, sl in enumerate(head_lanes):
            blk = slice(h * 2 * n, (h + 1) * 2 * n)
            o_ref[:, sl] = _attn_finish(acc[blk], l[blk], n, lam, lam_init, nw_ref[...], dz_ref[:, sl])


def _decode(q, k_new, v_new, z, cache_k, cache_v, page_table, layer, lam, norm_w, nb, t_valid, bw, heads,
            lam_init, dz_col_block):
    n_pages = page_table.shape[1]
    dh = bw // heads
    pps = _tile(n_pages, DECODE_PAGES_PER_STEP)
    kern = functools.partial(_decode_kernel, heads=heads, dqk=dh // 2, t_valid=t_valid, lam_init=lam_init, pps=pps)
    row_spec = pl.BlockSpec((SUBLANES, bw), lambda b, j, pt: (b, 0))
    cache_specs = [
        pl.BlockSpec((None, None, cache_k.shape[2], dh),
                     lambda b, j, pt, i=i: (layer, pt[b * n_pages + j * pps + i], 0, 0))
        for i in range(pps)
    ]
    rows = 2 * SUBLANES * heads
    grid_spec = pltpu.PrefetchScalarGridSpec(
        num_scalar_prefetch=1,
        grid=(nb, n_pages // pps),
        in_specs=[row_spec] + cache_specs + cache_specs + [
            row_spec, row_spec,
            pl.BlockSpec((SUBLANES, bw), lambda b, j, pt: (b, dz_col_block)),
            pl.BlockSpec(lam.shape, lambda b, j, pt: (0, 0)),
            pl.BlockSpec((1, dh), lambda b, j, pt: (0, 0)),
        ],
        out_specs=row_spec,
        scratch_shapes=[pltpu.VMEM((rows, 1), F32), pltpu.VMEM((rows, 1), F32), pltpu.VMEM((rows, dh), F32)],
    )
    return pl.pallas_call(
        kern,
        grid_spec=grid_spec,
        out_shape=jax.ShapeDtypeStruct((nb * SUBLANES, bw), BF16),
        compiler_params=_params("parallel", "arbitrary"),
        name="diff_decode",
    )(page_table.reshape(-1), q, *([cache_k] * pps), *([cache_v] * pps), k_new, v_new, z, lam, norm_w)


def _merge_kernel(h_ref, a_ref, b_ref, c_ref, d_ref, m0_ref, m1_ref, m2_ref, m3_ref, wb_ref, o_ref):
    h = h_ref[...]
    acc = None
    for n, (br, wm) in enumerate(zip((a_ref, b_ref, c_ref, d_ref), (m0_ref, m1_ref, m2_ref, m3_ref))):
        gate = jax.nn.sigmoid(jnp.dot(h, wm[...], preferred_element_type=F32))
        term = gate * jnp.dot(br[...], wb_ref[n], preferred_element_type=F32)
        acc = term if acc is None else acc + term
    o_ref[...] = acc.astype(BF16)


def _merge(h, branches, w_merge, w_branch, layer):
    m, d = h.shape
    bw = branches[0].shape[1]
    tm = _tile(m, 512)
    tn = _tile(d, 256)
    nj = d // tn
    br_spec = pl.BlockSpec((tm, bw), lambda i, j: (i, 0))
    return pl.pallas_call(
        _merge_kernel,
        grid=(m // tm, nj),
        in_specs=[pl.BlockSpec((tm, d), lambda i, j: (i, 0))] + [br_spec] * 4 + [
            pl.BlockSpec((None, d, tn), lambda i, j, n=n: (layer, 0, n * nj + j)) for n in range(4)
        ] + [pl.BlockSpec((None, 4, bw, tn), lambda i, j: (layer, 0, 0, j))],
        out_specs=pl.BlockSpec((tm, tn), lambda i, j: (i, j)),
        out_shape=jax.ShapeDtypeStruct((m, d), BF16),
        compiler_params=_params("parallel", "arbitrary"),
        name="merge",
    )(h, *branches, w_merge, w_merge, w_merge, w_merge, w_branch)


def _outproj_kernel(a_ref, w_ref, x_ref, g_ref, o_ref):
    y = jnp.dot(a_ref[...], w_ref[...], preferred_element_type=F32)
    o_ref[...] = x_ref[...] + g_ref[0] * y


def _outproj(a, w, layer, x, gate, rows_per_mod):
    m, d = a.shape
    r = gate.shape[1]
    tm = r if r > 1 else _tile(rows_per_mod, 1024)
    per = rows_per_mod // tm if r == 1 else 1
    tn = _tile(d, 512)
    return pl.pallas_call(
        _outproj_kernel,
        grid=(m // tm, d // tn),
        in_specs=[
            pl.BlockSpec((tm, d), lambda i, j: (i, 0)),
            pl.BlockSpec((None, d, tn), lambda i, j: (layer, 0, j)),
            pl.BlockSpec((tm, tn), lambda i, j: (i, j)),
            pl.BlockSpec((1, r, tn), lambda i, j: (i // per, 0, j)),
        ],
        out_specs=pl.BlockSpec((tm, tn), lambda i, j: (i, j)),
        out_shape=jax.ShapeDtypeStruct((m, d), F32),
        compiler_params=_params("parallel", "arbitrary"),
        name="out_proj",
    )(a, w, x, gate)


def _final_norm_kernel(x_ref, w_ref, o_ref):
    x = x_ref[...]
    o_ref[...] = x * lax.rsqrt(jnp.mean(x * x, axis=-1, keepdims=True) + EPS) * w_ref[...]


def _final_norm(x, w):
    m, d = x.shape
    tr = _tile(m, 256)
    return pl.pallas_call(
        _final_norm_kernel,
        grid=(m // tr,),
        in_specs=[pl.BlockSpec((tr, d), lambda i: (i, 0)), pl.BlockSpec((1, d), lambda i: (0, 0))],
        out_specs=pl.BlockSpec((tr, d), lambda i: (i, 0)),
        out_shape=jax.ShapeDtypeStruct((m, d), F32),
        compiler_params=_params("parallel"),
        name="final_norm",
    )(x, w.reshape(1, d))


def _rope_tables(pos, dh, dqk):
    rope_dim = dqk // 4
    half = rope_dim // 2
    inv_freq = ROPE_THETA ** (-jnp.arange(half, dtype=F32) * (2.0 / rope_dim))
    ang = pos.astype(F32)[:, None] * inv_freq[None, :]
    cos, sin = jnp.cos(ang), jnp.sin(ang)
    n = pos.shape[0]
    pad = jnp.zeros((n, dqk - rope_dim), F32)
    comp_cos = jnp.concatenate([cos, cos, pad + 1.0], axis=1)
    comp_sa = jnp.concatenate([-sin, jnp.zeros_like(sin), pad], axis=1)
    comp_sb = jnp.concatenate([jnp.zeros_like(sin), sin, pad], axis=1)
    reps = dh // dqk
    return tuple(jnp.tile(t, (1, reps)) for t in (comp_cos, comp_sa, comp_sb))


def _tail_tile(state, width):
    nb, k, _ = state.shape
    return jnp.concatenate([jnp.zeros((nb, SUBLANES - k, width), F32), state.astype(F32)], axis=1)


def _pad_time(x, tp):
    nb, t = x.shape[:2]
    return jnp.pad(x, [(0, 0), (0, tp - t)] + [(0, 0)] * (x.ndim - 2))


def kernel(x_prompt, x_sample, c_prompt, c_sample, state_conv_a, state_gdn_conv, state_gdn, state_gla, cache_k, cache_v, page_table, w_ada, b_ada, w_in, conv_a_w, gdn_conv_w, gdn_a_log, gdn_dt_bias, gdn_norm_w, diff_lambda, diff_norm_w, gla_w_gk2, gla_b_gk, gla_norm_w, w_branch, w_out, final_norm_w):
    nbp, tpp, d = x_prompt.shape
    nbs, tvs, _ = x_sample.shape
    depth = w_in.shape[0]
    bw = d // 4
    gdn_h = gdn_a_log.shape[1]
    diff_h = cache_k.shape[3]
    dh = cache_k.shape[4]
    dqk = dh // 2
    gla_h = state_gla.shape[2]
    gla_dk = state_gla.shape[3]
    rank = gla_w_gk2.shape[1]
    conv_a_k = conv_a_w.shape[1]
    gdn_k = gdn_conv_w.shape[1]
    past_len = page_table.shape[1] * cache_k.shape[2]
    tps = SUBLANES
    assert tvs <= tps and 2 * gdn_h + rank <= LANES and bw // gdn_h == LANES and dh == LANES

    off_gdn = 4 * bw
    off_side = off_gdn + 4 * bw
    off_diff = off_side + 2 * gdn_h
    off_gla = off_diff + 4 * bw
    off_lr = off_gla + 3 * bw
    off_merge = off_lr + rank
    w_t = jnp.swapaxes(w_in, 1, 2)
    w_ab = _repack(w_t, 0, off_side)
    w_cd = _repack(w_t, off_diff, off_lr - off_diff)
    w_side = _side_weights(w_t, off_side, 2 * gdn_h, off_lr, rank)
    w_merge = _repack(w_t, off_merge, 4 * d)
    w_branch_b = w_branch.astype(BF16)
    w_out_b = w_out.astype(BF16)
    wgk_pad = jnp.concatenate(
        [jnp.zeros((depth, 2 * gdn_h, gla_h * gla_dk), F32), gla_w_gk2,
         jnp.zeros((depth, LANES - 2 * gdn_h - rank, gla_h * gla_dk), F32)], axis=1).astype(BF16)
    lane_pad = jnp.zeros((depth, LANES - gdn_h), F32)
    alog_pad = jnp.concatenate([gdn_a_log, lane_pad], axis=1)
    dtb_pad = jnp.concatenate([gdn_dt_bias, lane_pad], axis=1)

    pool, page = cache_k.shape[1:3]
    cache_k_rows = cache_k.reshape(depth, pool, page * diff_h, dh)
    cache_v_rows = cache_v.reshape(depth, pool, page * diff_h, dh)

    n_c = nbp + nbs
    c_rows = -(-n_c // SUBLANES) * SUBLANES
    c_all = jnp.concatenate([c_prompt, c_sample, jnp.zeros((c_rows - n_c, d), F32)], axis=0)
    mod = _ada(c_all, w_ada, b_ada)

    groups = []
    groups.append(dict(
        nb=nbp, tp=tpp, tv=tpp, x=x_prompt.reshape(nbp * tpp, d), mod_rows=slice(0, nbp), per_row=False,
        conv_a=jnp.zeros((depth, nbp, conv_a_k - 1, bw), F32),
        gdn_conv=jnp.zeros((depth, nbp, gdn_k - 1, 3 * bw), F32),
        gdn_s=jnp.zeros((depth, nbp) + state_gdn.shape[2:], F32),
        gla_s=jnp.zeros((depth, nbp) + state_gla.shape[2:], F32),
        tabs=_rope_tables(jnp.arange(tpp, dtype=jnp.int32), dh, dqk), paged=False))
    groups.append(dict(
        nb=nbs, tp=tps, tv=tvs, x=_pad_time(x_sample, tps).reshape(nbs * tps, d), mod_rows=slice(nbp, n_c),
        per_row=True, conv_a=state_conv_a, gdn_conv=state_gdn_conv, gdn_s=state_gdn, gla_s=state_gla,
        tabs=_rope_tables(past_len + jnp.arange(tps, dtype=jnp.int32), dh, dqk), paged=True))

    results = []
    for g in groups:
        nb, tp, tv = g["nb"], g["tp"], g["tv"]
        x = g["x"]
        outs = [[] for _ in range(6)]
        for l in range(depth):
            m_l = mod[l, g["mod_rows"]]
            if g["per_row"]:
                m_l = jnp.repeat(m_l, tp, axis=0).reshape(1, nb * tp, 3 * d)
            else:
                m_l = m_l.reshape(nb, 1, 3 * d)
            shift, scale, gate = m_l[..., :d], m_l[..., d:2 * d], m_l[..., 2 * d:]
            h = _prenorm(x, scale, shift, tp)
            z_ab = _matmul(h, w_ab, l)
            z_c = _matmul(h, w_cd, l, 0, 4 * bw)
            z_d = _matmul(h, w_cd, l, 4 * bw, 3 * bw)
            zs = _matmul(h, w_side, l, tn_pref=LANES)

            out_a, tail_a = _conva(z_ab, _tail_tile(g["conv_a"][l], bw), conv_a_w[l], nb, tp, bw)
            out_b, tail_b, s_gdn = _gdn(
                z_ab, zs, _tail_tile(g["gdn_conv"][l], 3 * bw), gdn_conv_w[l], alog_pad[l:l + 1],
                dtb_pad[l:l + 1], gdn_norm_w[l:l + 1], g["gdn_s"][l].astype(F32), nb, tp, tv, bw, gdn_h, 1)
            q_r, k_r, v_r, new_k, new_v = _rope(z_c, g["tabs"], nb, tp, bw, diff_h, 0)
            lam_init = 0.8 - 0.6 * math.exp(-0.3 * l)
            if g["paged"]:
                out_c = _decode(q_r, k_r, v_r, z_c, cache_k_rows, cache_v_rows, page_table, l, diff_lambda[l],
                                diff_norm_w[l:l + 1], nb, tv, bw, diff_h, lam_init, 3)
            else:
                out_c = _flash(q_r, k_r, v_r, z_c, diff_lambda[l], diff_norm_w[l:l + 1], nb, tp, bw, diff_h,
                               lam_init, 3 * diff_h)
            out_d, s_gla_t = _gla(
                z_d, zs, wgk_pad[l], gla_b_gk[l:l + 1], gla_norm_w[l:l + 1],
                jnp.swapaxes(g["gla_s"][l].astype(F32), -1, -2), nb, tp, tv, bw, gla_h, 0)

            merged = _merge(h, (out_a, out_b, out_c, out_d), w_merge, w_branch_b, l)
            x = _outproj(merged, w_out_b, l, x, gate, tp)

            last = tv - (tp - SUBLANES)
            outs[0].append(tail_a[:, last - (conv_a_k - 1):last])
            outs[1].append(tail_b[:, last - (gdn_k - 1):last])
            outs[2].append(s_gdn)
            outs[3].append(jnp.swapaxes(s_gla_t, -1, -2))
            outs[4].append(new_k.reshape(nb, tp, diff_h, dh)[:, :tv])
            outs[5].append(new_v.reshape(nb, tp, diff_h, dh)[:, :tv])
        y = _final_norm(x, final_norm_w).reshape(nb, tp, d)[:, :tv]
        results.append((y, *[jnp.stack(o) for o in outs]))

    (yp, *sp), (ys, *ss) = results
    return (yp, ys, *sp, *ss)
```

```python
import functools
import math

import jax
import jax.numpy as jnp
import numpy as np
from jax import lax
from jax.experimental import pallas as pl
from jax.experimental.pallas import tpu as pltpu

F32 = jnp.float32
BF16 = jnp.bfloat16
EPS = 1e-6
GDN_CHUNK = 64
GDN_CHUNKS_PER_STEP = 4
DECODE_PAGES_PER_STEP = 8
FLASH_BLOCK = 256
FLASH_HEADS = 4
GLA_CHUNK = 32
GLA_CHUNKS_PER_STEP = 8
GLA_NORMALIZER = 16.0
ROPE_THETA = 500000.0
SUBLANES = 8
LANES = 128
VMEM_LIMIT = 56 * 1024 * 1024
HIGHEST = lax.Precision.HIGHEST

_NT = (((1,), (1,)), ((), ()))
_TN = (((0,), (0,)), ((), ()))
_BNN = (((2,), (1,)), ((0,), (0,)))
_BNT = (((2,), (2,)), ((0,), (0,)))
_BTN = (((1,), (1,)), ((0,), (0,)))


def _params(*sem):
    return pltpu.CompilerParams(dimension_semantics=sem, vmem_limit_bytes=VMEM_LIMIT)


def _silu(x):
    return x * jax.nn.sigmoid(x)


def _softplus(x):
    return jnp.maximum(x, 0.0) + jnp.log(1.0 + jnp.exp(-jnp.abs(x)))


def _dot(a, b):
    return jnp.dot(a.astype(BF16), b.astype(BF16), preferred_element_type=F32)


def _dot_nt(a, b):
    return lax.dot_general(a.astype(BF16), b.astype(BF16), _NT, preferred_element_type=F32)


def _dot_tn(a, b):
    return lax.dot_general(a.astype(BF16), b.astype(BF16), _TN, preferred_element_type=F32)


def _bdot(a, b, dims=_BNN):
    return lax.dot_general(a.astype(BF16), b.astype(BF16), dims, preferred_element_type=F32)


def _tile(n, pref):
    if n <= pref:
        return n
    t = pref
    while n % t:
        t //= 2
    return t


def _ada_kernel(c_ref, w_ref, b_ref, o_ref):
    c = c_ref[...]
    o_ref[0] = _dot(_silu(c), w_ref[0]) + b_ref[0]


def _ada(c_all, w_ada, b_ada):
    depth, d, n = w_ada.shape
    rows = c_all.shape[0]
    tn = _tile(n, 1024)
    return pl.pallas_call(
        _ada_kernel,
        grid=(depth, n // tn),
        in_specs=[
            pl.BlockSpec((rows, d), lambda l, j: (0, 0)),
            pl.BlockSpec((1, d, tn), lambda l, j: (l, 0, j)),
            pl.BlockSpec((1, 1, tn), lambda l, j: (l, 0, j)),
        ],
        out_specs=pl.BlockSpec((1, rows, tn), lambda l, j: (l, 0, j)),
        out_shape=jax.ShapeDtypeStruct((depth, rows, n), F32),
        compiler_params=_params("parallel", "parallel"),
        name="ada_mod",
    )(c_all, w_ada, b_ada.reshape(depth, 1, n))


def _repack_kernel(a_ref, o_ref):
    o_ref[...] = a_ref[0].T.astype(BF16)


def _repack(w_t, start, width):
    depth, _, k = w_t.shape
    tn = _tile(width, 1024)
    tk = _tile(k, 1024)
    return pl.pallas_call(
        _repack_kernel,
        grid=(depth, k // tk, width // tn),
        in_specs=[pl.BlockSpec((pl.Element(1), pl.Element(tn), pl.Element(tk)),
                               lambda l, i, j: (l, pl.multiple_of(start + j * tn, SUBLANES),
                                                pl.multiple_of(i * tk, LANES)))],
        out_specs=pl.BlockSpec((None, tk, tn), lambda l, i, j: (l, i, j)),
        out_shape=jax.ShapeDtypeStruct((depth, k, width), BF16),
        compiler_params=_params("parallel", "parallel", "parallel"),
        name="repack",
    )(w_t)


def _side_kernel(a_ref, b_ref, o_ref, *, n_a, n_b):
    lane = lax.broadcasted_iota(jnp.int32, o_ref.shape, 1)
    o_ref[...] = jnp.where(lane < n_a, a_ref[0].T, jnp.where(lane < n_a + n_b, b_ref[0].T, 0.0)).astype(BF16)


def _side_weights(w_t, start_a, n_a, start_b, n_b):
    depth, _, k = w_t.shape
    assert n_a + n_b <= LANES
    tk = _tile(k, 512)
    return pl.pallas_call(
        functools.partial(_side_kernel, n_a=n_a, n_b=n_b),
        grid=(depth, k // tk),
        in_specs=[
            pl.BlockSpec((pl.Element(1), pl.Element(LANES), pl.Element(tk)), lambda l, i: (l, start_a, i * tk)),
            pl.BlockSpec((pl.Element(1), pl.Element(LANES), pl.Element(tk)),
                         lambda l, i: (l, start_b - n_a, i * tk)),
        ],
        out_specs=pl.BlockSpec((None, tk, LANES), lambda l, i: (l, i, 0)),
        out_shape=jax.ShapeDtypeStruct((depth, k, LANES), BF16),
        compiler_params=_params("parallel", "parallel"),
        name="repack_side",
    )(w_t, w_t)


def _prenorm_kernel(x_ref, sc_ref, sh_ref, h_ref):
    x = x_ref[...]
    y = x * lax.rsqrt(jnp.mean(x * x, axis=-1, keepdims=True) + EPS)
    h_ref[...] = (y * (1.0 + sc_ref[0]) + sh_ref[0]).astype(BF16)


def _prenorm(x, scale, shift, rows_per_mod):
    m, d = x.shape
    r = scale.shape[1]
    tr = r if r > 1 else _tile(rows_per_mod, 256)
    per = rows_per_mod // tr if r == 1 else 1
    return pl.pallas_call(
        _prenorm_kernel,
        grid=(m // tr,),
        in_specs=[
            pl.BlockSpec((tr, d), lambda i: (i, 0)),
            pl.BlockSpec((1, r, d), lambda i: (i // per, 0, 0)),
            pl.BlockSpec((1, r, d), lambda i: (i // per, 0, 0)),
        ],
        out_specs=pl.BlockSpec((tr, d), lambda i: (i, 0)),
        out_shape=jax.ShapeDtypeStruct((m, d), BF16),
        compiler_params=_params("parallel"),
        name="prenorm",
    )(x, scale, shift)


def _mm_kernel(a_ref, w_ref, o_ref):
    o_ref[...] = jnp.dot(a_ref[...], w_ref[...], preferred_element_type=F32).astype(o_ref.dtype)


def _matmul(a, w, layer, col0=0, n=None, out_dtype=F32, tm_pref=1024, tn_pref=1024):
    m, k = a.shape
    n = w.shape[2] if n is None else n
    tm = _tile(m, tm_pref)
    tn = _tile(math.gcd(n, col0) if col0 else n, tn_pref)
    return pl.pallas_call(
        _mm_kernel,
        grid=(m // tm, n // tn),
        in_specs=[
            pl.BlockSpec((tm, k), lambda i, j: (i, 0)),
            pl.BlockSpec((None, k, tn), lambda i, j: (layer, 0, col0 // tn + j)),
        ],
        out_specs=pl.BlockSpec((tm, tn), lambda i, j: (i, j)),
        out_shape=jax.ShapeDtypeStruct((m, n), out_dtype),
        compiler_params=_params("parallel", "arbitrary"),
        name="in_proj",
    )(a, w)


def _causal_conv(carry_ref, st_ref, w_ref, u):
    n = u.shape[0]

    @pl.when(pl.program_id(1) == 0)
    def _():
        carry_ref[...] = st_ref[0]

    ext = jnp.concatenate([carry_ref[...], u], axis=0)
    w = w_ref[...]
    taps = w.shape[0]
    y = w[taps - 1:taps] * u
    for k in range(1, taps):
        y = y + w[taps - 1 - k:taps - k] * pltpu.roll(ext, k, 0)[SUBLANES:SUBLANES + n]
    tail = u[n - SUBLANES:n]
    carry_ref[...] = tail
    return y, tail


def _valid_rows(c, t_valid):
    t = pl.program_id(1)
    row = lax.broadcasted_iota(jnp.int32, (c, 1), 0) + t * c
    return row < t_valid


def _tri(c):
    row = lax.broadcasted_iota(jnp.int32, (c, c), 0)
    col = lax.broadcasted_iota(jnp.int32, (c, c), 1)
    return row, col


def _conva_kernel(z_ref, st_ref, w_ref, o_ref, tail_ref, carry, *, bw):
    z = z_ref[...]
    a_h, a_b, a_c, a_z = (z[:, i * bw:(i + 1) * bw] for i in range(4))
    y, tail = _causal_conv(carry, st_ref, w_ref, a_c * a_h)
    o_ref[...] = (a_b * y * _silu(a_z)).astype(BF16)
    tail_ref[0] = tail


def _conva(z, state_tile, w, nb, tp, bw):
    tt = _tile(tp, 256)
    nt = tp // tt
    return pl.pallas_call(
        functools.partial(_conva_kernel, bw=bw),
        grid=(nb, nt),
        in_specs=[
            pl.BlockSpec((tt, 4 * bw), lambda b, t: (b * nt + t, 0)),
            pl.BlockSpec((1, SUBLANES, bw), lambda b, t: (b, 0, 0)),
            pl.BlockSpec(w.shape, lambda b, t: (0, 0)),
        ],
        out_specs=[
            pl.BlockSpec((tt, bw), lambda b, t: (b * nt + t, 0)),
            pl.BlockSpec((1, SUBLANES, bw), lambda b, t: (b, 0, 0)),
        ],
        out_shape=[
            jax.ShapeDtypeStruct((nb * tp, bw), BF16),
            jax.ShapeDtypeStruct((nb, SUBLANES, bw), F32),
        ],
        scratch_shapes=[pltpu.VMEM((SUBLANES, bw), F32)],
        compiler_params=_params("parallel", "arbitrary"),
        name="branch_conv",
    )(z, state_tile, w)


INVERSE_BASE = 8


def _inverse_masks(row, col, c):
    base = min(INVERSE_BASE, c)
    masks = [(row // base) == (col // base)]
    b = base
    while b < c:
        masks.append(((row // (2 * b)) == (col // (2 * b))) & ((row // b) != (col // b)))
        b *= 2
    return masks


def _unit_lower_inverse(lmat, eye, masks, mm):
    base = min(INVERSE_BASE, lmat.shape[-1])
    p = jnp.where(masks[0], lmat, 0.0)
    x = eye - p
    for _ in range(max(0, int(math.log2(base)) - 1)):
        p = mm(p, p)
        x = x + mm(x, p)
    for m in masks[1:]:
        x = x - mm(mm(x, jnp.where(m, lmat, 0.0)), x)
    return x


def _gdn_kernel(z_ref, zs_ref, cst_ref, cw_ref, alog_ref, dtb_ref, nw_ref, s0_ref,
                o_ref, tail_ref, sout_ref, carry, s_scr, *, bw, heads, t_valid, c):
    t = pl.program_id(1)
    nt = pl.num_programs(1)

    @pl.when(t == 0)
    def _():
        s_scr[...] = s0_ref[0]

    z = z_ref[...]
    r = z.shape[0]
    n_chunks = r // c
    dk = bw // heads
    gz = z[:, 3 * bw:4 * bw]
    y, tail = _causal_conv(carry, cst_ref, cw_ref, z[:, :3 * bw])
    tail_ref[0] = tail
    qkv = _silu(y)

    valid = _valid_rows(r, t_valid).astype(F32)
    zs = zs_ref[...]
    g = valid * (-jnp.exp(alog_ref[...]) * _softplus(zs + dtb_ref[...]))
    beta = valid * jax.nn.sigmoid(zs)
    row_r, col_r = _tri(r)
    same_chunk = (row_r // c) == (col_r // c)
    gsum = jnp.dot(((row_r >= col_r) & same_chunk).astype(F32), g, precision=HIGHEST,
                   preferred_element_type=F32)
    row, col = _tri(c)
    incl = row >= col
    strict = row > col
    eye = (row == col).astype(F32)
    inv_masks = _inverse_masks(row, col, c)
    r128, c128 = _tri(LANES)
    eye128 = (r128 == c128).astype(F32)
    rows = [slice(ci * c, (ci + 1) * c) for ci in range(n_chunks)]
    gsum_t = [lax.dot_general(eye128, gsum[rs], _NT, precision=HIGHEST, preferred_element_type=F32)
              for rs in rows]

    def per(fn):
        return jnp.stack([fn(ci, rows[ci], h) for ci in range(n_chunks) for h in range(heads)])

    q = per(lambda ci, rs, h: qkv[rs, h * dk:(h + 1) * dk])
    k = per(lambda ci, rs, h: qkv[rs, bw + h * dk:bw + (h + 1) * dk])
    v = per(lambda ci, rs, h: qkv[rs, 2 * bw + h * dk:2 * bw + (h + 1) * dk])
    gcol = per(lambda ci, rs, h: jnp.broadcast_to(gsum[rs, h:h + 1], (c, dk)))
    bcol = per(lambda ci, rs, h: jnp.broadcast_to(beta[rs, heads + h:heads + h + 1], (c, dk)))
    vcol = per(lambda ci, rs, h: jnp.broadcast_to(valid[rs], (c, dk)))
    grow = per(lambda ci, rs, h: gsum_t[ci][h:h + 1, :])

    qn = q * lax.rsqrt(jnp.sum(q * q, axis=-1, keepdims=True) + EPS) * (dk ** -0.5)
    kn = vcol * (k * lax.rsqrt(jnp.sum(k * k, axis=-1, keepdims=True) + EPS))
    eg = jnp.exp(gcol)
    decay = jnp.exp(jnp.where(incl, gcol[:, :, :c] - grow, 0.0))
    kb = kn * bcol
    kq = _bdot(jnp.concatenate([kb, qn], axis=1), kn, _BNT)
    lmat = jnp.where(strict, kq[:, :c] * decay, 0.0)
    qk = jnp.where(incl, kq[:, c:] * decay, 0.0)
    tinv = _unit_lower_inverse(lmat, eye, inv_masks, _bdot)
    sol = _bdot(tinv, jnp.concatenate([v * bcol, kb * eg], axis=2))
    u = sol[:, :, :dk]
    wq = jnp.concatenate([sol[:, :, dk:], qn * eg], axis=1)
    g_last = gcol[:, c - 1:c, :]
    k_dec = kn * jnp.exp(g_last - gcol)
    s_decay = jnp.exp(g_last)

    s = s_scr[...]
    for ci in range(n_chunks):
        sel = slice(ci * heads, (ci + 1) * heads)
        ws = _bdot(wq[sel], s)
        v_new = u[sel] - ws[:, :c]
        o = ws[:, c:] + _bdot(qk[sel], v_new)
        s = s * s_decay[sel] + _bdot(k_dec[sel], v_new, _BTN)
        on = o * lax.rsqrt(jnp.mean(o * o, axis=-1, keepdims=True) + EPS) * nw_ref[...]
        for h in range(heads):
            sl = slice(h * dk, (h + 1) * dk)
            o_ref[rows[ci], sl] = (on[h] * _silu(gz[rows[ci], sl])).astype(BF16)
    s_scr[...] = s

    @pl.when(t == nt - 1)
    def _():
        sout_ref[0] = s


def _gdn(z, zs, conv_tile, conv_w, alog, dtb, norm_w, s0, nb, tp, t_valid, bw, heads, col_block):
    c = min(GDN_CHUNK, tp)
    r = min(GDN_CHUNK * GDN_CHUNKS_PER_STEP, tp)
    nt = tp // r
    dk = bw // heads
    return pl.pallas_call(
        functools.partial(_gdn_kernel, bw=bw, heads=heads, t_valid=t_valid, c=c),
        grid=(nb, nt),
        in_specs=[
            pl.BlockSpec((r, 4 * bw), lambda b, t: (b * nt + t, col_block)),
            pl.BlockSpec((r, LANES), lambda b, t: (b * nt + t, 0)),
            pl.BlockSpec((1, SUBLANES, 3 * bw), lambda b, t: (b, 0, 0)),
            pl.BlockSpec(conv_w.shape, lambda b, t: (0, 0)),
            pl.BlockSpec((1, LANES), lambda b, t: (0, 0)),
            pl.BlockSpec((1, LANES), lambda b, t: (0, 0)),
            pl.BlockSpec((1, dk), lambda b, t: (0, 0)),
            pl.BlockSpec((1, heads, dk, dk), lambda b, t: (b, 0, 0, 0)),
        ],
        out_specs=[
            pl.BlockSpec((r, bw), lambda b, t: (b * nt + t, 0)),
            pl.BlockSpec((1, SUBLANES, 3 * bw), lambda b, t: (b, 0, 0)),
            pl.BlockSpec((1, heads, dk, dk), lambda b, t: (b, 0, 0, 0)),
        ],
        out_shape=[
            jax.ShapeDtypeStruct((nb * tp, bw), BF16),
            jax.ShapeDtypeStruct((nb, SUBLANES, 3 * bw), F32),
            jax.ShapeDtypeStruct((nb, heads, dk, dk), F32),
        ],
        scratch_shapes=[pltpu.VMEM((SUBLANES, 3 * bw), F32), pltpu.VMEM((heads, dk, dk), F32)],
        compiler_params=_params("parallel", "arbitrary"),
        name="branch_gdn",
    )(z, zs, conv_tile, conv_w, alog, dtb, norm_w, s0)


def _gla_kernel(z_ref, zs_ref, wgk_ref, bgk_ref, nw_ref, s0_ref, o_ref, sout_ref, s_scr,
                *, bw, heads, t_valid, c):
    t = pl.program_id(1)
    nt = pl.num_programs(1)

    @pl.when(t == 0)
    def _():
        s_scr[...] = s0_ref[0]

    z = z_ref[...]
    r = z.shape[0]
    n_chunks = r // c
    dk = bw // (2 * heads)
    dv = bw // heads
    valid = _valid_rows(r, t_valid).astype(F32)
    gk = valid * (-_softplus(-(_dot(zs_ref[...], wgk_ref[...]) + bgk_ref[...])) / GLA_NORMALIZER)
    row_r, col_r = _tri(r)
    same_chunk = (row_r // c) == (col_r // c)
    gsum = jnp.dot(((row_r >= col_r) & same_chunk).astype(F32), gk, precision=HIGHEST,
                   preferred_element_type=F32)
    row, col = _tri(c)
    incl = row >= col
    lz = z[:, 2 * bw:3 * bw]
    rows = [slice(ci * c, (ci + 1) * c) for ci in range(n_chunks)]

    def per(fn):
        return jnp.stack([fn(rows[ci], h) for ci in range(n_chunks) for h in range(heads)])

    q = per(lambda rs, h: z[rs, h * dk:(h + 1) * dk]) * (dk ** -0.5)
    k = per(lambda rs, h: z[rs, bw // 2 + h * dk:bw // 2 + (h + 1) * dk] * valid[rs])
    v = per(lambda rs, h: z[rs, bw + h * dv:bw + (h + 1) * dv])
    g = per(lambda rs, h: gsum[rs, h * dk:(h + 1) * dk])
    q_dec = q * jnp.exp(g)
    attn = jnp.where(incl, _bdot(q_dec, k * jnp.exp(-g), _BNT), 0.0)
    o_intra = _bdot(attn, v)
    g_last = g[:, c - 1:c, :]
    k_dec = k * jnp.exp(g_last - g)
    s_decay = jnp.exp(g_last)

    s_t = s_scr[...]
    for ci in range(n_chunks):
        sel = slice(ci * heads, (ci + 1) * heads)
        o = o_intra[sel] + _bdot(q_dec[sel], s_t, _BNT)
        s_t = s_t * s_decay[sel] + _bdot(v[sel], k_dec[sel], _BTN)
        on = o * lax.rsqrt(jnp.mean(o * o, axis=-1, keepdims=True) + EPS) * nw_ref[...]
        for h in range(heads):
            sl = slice(h * dv, (h + 1) * dv)
            o_ref[rows[ci], sl] = (on[h] * _silu(lz[rows[ci], sl])).astype(BF16)
    s_scr[...] = s_t

    @pl.when(t == nt - 1)
    def _():
        sout_ref[0] = s_t


def _gla(z, zs, wgk_pad, bgk, norm_w, s0_t, nb, tp, t_valid, bw, heads, col_block):
    c = min(GLA_CHUNK, tp)
    r = min(GLA_CHUNK * GLA_CHUNKS_PER_STEP, tp)
    nt = tp // r
    dk = bw // (2 * heads)
    dv = bw // heads
    return pl.pallas_call(
        functools.partial(_gla_kernel, bw=bw, heads=heads, t_valid=t_valid, c=c),
        grid=(nb, nt),
        in_specs=[
            pl.BlockSpec((r, 3 * bw), lambda b, t: (b * nt + t, col_block)),
            pl.BlockSpec((r, LANES), lambda b, t: (b * nt + t, 0)),
            pl.BlockSpec(wgk_pad.shape, lambda b, t: (0, 0)),
            pl.BlockSpec(bgk.shape, lambda b, t: (0, 0)),
            pl.BlockSpec((1, dv), lambda b, t: (0, 0)),
            pl.BlockSpec((1, heads, dv, dk), lambda b, t: (b, 0, 0, 0)),
        ],
        out_specs=[
            pl.BlockSpec((r, bw), lambda b, t: (b * nt + t, 0)),
            pl.BlockSpec((1, heads, dv, dk), lambda b, t: (b, 0, 0, 0)),
        ],
        out_shape=[
            jax.ShapeDtypeStruct((nb * tp, bw), BF16),
            jax.ShapeDtypeStruct((nb, heads, dv, dk), F32),
        ],
        scratch_shapes=[pltpu.VMEM((heads, dv, dk), F32)],
        compiler_params=_params("parallel", "arbitrary"),
        name="branch_gla",
    )(z, zs, wgk_pad, bgk, norm_w, s0_t)


def _rope_kernel(z_ref, cos_ref, sa_ref, sb_ref, q_ref, k_ref, v_ref, nk_ref, nv_ref,
                 *, bw, heads, half, scale):
    z = z_ref[...]
    cos = cos_ref[...]
    sa = sa_ref[...]
    sb = sb_ref[...]
    dh = bw // heads

    def rope(x):
        return x * cos + pltpu.roll(x, dh - half, 1) * sa + pltpu.roll(x, half, 1) * sb

    for h in range(heads):
        sl = slice(h * dh, (h + 1) * dh)
        q = rope(z[:, sl])
        k = rope(z[:, bw + h * dh:bw + (h + 1) * dh])
        q_ref[:, sl] = (q * scale).astype(BF16)
        k_ref[:, sl] = k.astype(BF16)
        nk_ref[:, sl] = k
    v = z[:, 2 * bw:3 * bw]
    v_ref[...] = v.astype(BF16)
    nv_ref[...] = v


def _rope(z, tabs, nb, tp, bw, heads, col_block):
    tt = _tile(tp, 256)
    nt = tp // tt
    dh = bw // heads
    dqk = dh // 2
    kern = functools.partial(_rope_kernel, bw=bw, heads=heads, half=dqk // 8, scale=dqk ** -0.5)
    rows = nb * tp
    spec = pl.BlockSpec((tt, bw), lambda i: (i, 0))
    tspec = pl.BlockSpec((tt, dh), lambda i: (i % nt, 0))
    return pl.pallas_call(
        kern,
        grid=(rows // tt,),
        in_specs=[pl.BlockSpec((tt, 4 * bw), lambda i: (i, col_block)), tspec, tspec, tspec],
        out_specs=[spec] * 5,
        out_shape=[jax.ShapeDtypeStruct((rows, bw), BF16)] * 3 + [jax.ShapeDtypeStruct((rows, bw), F32)] * 2,
        compiler_params=_params("parallel"),
        name="diff_rope",
    )(z, *tabs)


def _lambda(lam_ref, lam_init):
    lf = lam_ref[...]
    a = jnp.sum(lf[0:1] * lf[1:2], axis=-1, keepdims=True)
    b = jnp.sum(lf[2:3] * lf[3:4], axis=-1, keepdims=True)
    return jnp.exp(a) - jnp.exp(b) + lam_init


def _stack_components(q, dqk):
    lane = lax.broadcasted_iota(jnp.int32, q.shape, 1)
    zero = jnp.zeros_like(q)
    return jnp.concatenate([jnp.where(lane < dqk, q, zero), jnp.where(lane >= dqk, q, zero)], axis=0)


def _online_update(s, v, m, l, acc, mm=_dot):
    m_new = jnp.maximum(m, jnp.max(s, axis=-1, keepdims=True))
    alpha = jnp.exp(m - m_new)
    p = jnp.exp(s - m_new)
    return m_new, alpha * l + jnp.sum(p, axis=-1, keepdims=True), alpha * acc + mm(p, v)


def _attn_finish(acc, l, n, lam, lam_init, nw, dz):
    o = acc[:n] / l[:n] - lam * (acc[n:] / l[n:])
    on = o * lax.rsqrt(jnp.mean(o * o, axis=-1, keepdims=True) + EPS) * nw * (1.0 - lam_init)
    return (on * _silu(dz)).astype(BF16)


def _flash_kernel(q_ref, k_ref, v_ref, dz_ref, lam_ref, nw_ref, o_ref, *, dqk, lam_init, hb):
    i = pl.program_id(2)
    tq = q_ref.shape[0]
    dh = 2 * dqk
    heads = [slice(h * dh, (h + 1) * dh) for h in range(hb)]
    qs = jnp.stack([_stack_components(q_ref[:, sl], dqk) for sl in heads])

    def block(j):
        start = pl.multiple_of(j * tq, tq)
        kb = k_ref[pl.ds(start, tq), :]
        vb = v_ref[pl.ds(start, tq), :]
        s = _bdot(qs, jnp.stack([kb[:, sl] for sl in heads]), _BNT)
        return s, jnp.stack([vb[:, sl] for sl in heads])

    def body(j, carry):
        s, v = block(j)
        return _online_update(s, v, *carry, mm=_bdot)

    init = (jnp.full((hb, 2 * tq, 1), -jnp.inf, F32), jnp.zeros((hb, 2 * tq, 1), F32),
            jnp.zeros((hb, 2 * tq, dh), F32))
    carry = lax.fori_loop(0, i, body, init)
    s, v = block(i)
    row = lax.broadcasted_iota(jnp.int32, s.shape, 1) % tq
    col = lax.broadcasted_iota(jnp.int32, s.shape, 2)
    m, l, acc = _online_update(jnp.where(col <= row, s, -jnp.inf), v, *carry, mm=_bdot)
    lam = _lambda(lam_ref, lam_init)
    for h, sl in enumerate(heads):
        o_ref[:, sl] = _attn_finish(acc[h], l[h], tq, lam, lam_init, nw_ref[...], dz_ref[:, sl])


def _flash(q, k, v, z, lam, norm_w, nb, tp, bw, heads, lam_init, dz_col0):
    dh = bw // heads
    hb = _tile(heads, FLASH_HEADS)
    tq = _tile(tp, FLASH_BLOCK)
    nq = tp // tq
    kern = functools.partial(_flash_kernel, dqk=dh // 2, lam_init=lam_init, hb=hb)
    return pl.pallas_call(
        kern,
        grid=(nb, heads // hb, nq),
        in_specs=[
            pl.BlockSpec((tq, hb * dh), lambda b, h, i: (b * nq + i, h)),
            pl.BlockSpec((tp, hb * dh), lambda b, h, i: (b, h)),
            pl.BlockSpec((tp, hb * dh), lambda b, h, i: (b, h)),
            pl.BlockSpec((tq, hb * dh), lambda b, h, i: (b * nq + i, dz_col0 // hb + h)),
            pl.BlockSpec(lam.shape, lambda b, h, i: (0, 0)),
            pl.BlockSpec((1, dh), lambda b, h, i: (0, 0)),
        ],
        out_specs=pl.BlockSpec((tq, hb * dh), lambda b, h, i: (b * nq + i, h)),
        out_shape=jax.ShapeDtypeStruct((nb * tp, bw), BF16),
        compiler_params=_params("parallel", "parallel", "arbitrary"),
        name="diff_flash",
    )(q, k, v, z, lam, norm_w)


def _decode_kernel(pt_ref, q_ref, *refs, heads, dqk, t_valid, lam_init, pps):
    kc_refs, vc_refs = refs[:pps], refs[pps:2 * pps]
    kn_ref, vn_ref, dz_ref, lam_ref, nw_ref, o_ref, m_scr, l_scr, acc_scr = refs[2 * pps:]
    j = pl.program_id(1)
    nj = pl.num_programs(1)
    n = q_ref.shape[0]
    dh = 2 * dqk
    rows = 2 * n * heads
    head_lanes = [slice(h * dh, (h + 1) * dh) for h in range(heads)]

    @pl.when(j == 0)
    def _():
        m_scr[...] = jnp.full(m_scr.shape, -jnp.inf, F32)
        l_scr[...] = jnp.zeros(l_scr.shape, F32)
        acc_scr[...] = jnp.zeros(acc_scr.shape, F32)

    q_all = jnp.concatenate([_stack_components(q_ref[:, sl].astype(F32), dqk) for sl in head_lanes], axis=0)
    keys = kc_refs[0].shape[0]
    row_head = lax.broadcasted_iota(jnp.int32, (rows, keys), 0) // (2 * n)
    col_head = lax.broadcasted_iota(jnp.int32, (rows, keys), 1) % heads
    same_head = row_head == col_head
    m, l, acc = m_scr[...], l_scr[...], acc_scr[...]
    scores = [jnp.where(same_head, _dot_nt(q_all, kc_ref[...]), -jnp.inf) for kc_ref in kc_refs]
    m_new = m
    for s in scores:
        m_new = jnp.maximum(m_new, jnp.max(s, axis=-1, keepdims=True))
    alpha = jnp.exp(m - m_new)
    l = alpha * l
    acc = alpha * acc
    for s, vc_ref in zip(scores, vc_refs):
        p = jnp.exp(s - m_new)
        l = l + jnp.sum(p, axis=-1, keepdims=True)
        acc = acc + _dot(p, vc_ref[...])
    state = (m_new, l, acc)
    m_scr[...], l_scr[...], acc_scr[...] = state

    @pl.when(j == nj - 1)
    def _():
        k_new = jnp.concatenate([kn_ref[:, sl].astype(F32) for sl in head_lanes], axis=0)
        v_new = jnp.concatenate([vn_ref[:, sl].astype(F32) for sl in head_lanes], axis=0)
        r2 = lax.broadcasted_iota(jnp.int32, (rows, n * heads), 0)
        c2 = lax.broadcasted_iota(jnp.int32, (rows, n * heads), 1)
        visible = ((r2 // (2 * n)) == (c2 // n)) & ((c2 % n) <= (r2 % n)) & ((c2 % n) < t_valid)
        s = jnp.where(visible, _dot_nt(q_all, k_new), -jnp.inf)
        m, l, acc = _online_update(s, v_new, *state)
        lam = _lambda(lam_ref, lam_init)
        for h, sl in enumerate(head_lanes):
            blk = slice(h * 2 * n, (h + 1) * 2 * n)
            o_ref[:, sl] = _attn_finish(acc[blk], l[blk], n, lam, lam_init, nw_ref[...], dz_ref[:, sl])


def _decode(q, k_new, v_new, z, cache_k, cache_v, page_table, layer, lam, norm_w, nb, t_valid, bw, heads,
            lam_init, dz_col_block):
    n_pages = page_table.shape[1]
    dh = bw // heads
    pps = _tile(n_pages, DECODE_PAGES_PER_STEP)
    kern = functools.partial(_decode_kernel, heads=heads, dqk=dh // 2, t_valid=t_valid, lam_init=lam_init, pps=pps)
    row_spec = pl.BlockSpec((SUBLANES, bw), lambda b, j, pt: (b, 0))
    cache_specs = [
        pl.BlockSpec((None, None, cache_k.shape[2], dh),
                     lambda b, j, pt, i=i: (layer, pt[b * n_pages + j * pps + i], 0, 0))
        for i in range(pps)
    ]
    rows = 2 * SUBLANES * heads
    grid_spec = pltpu.PrefetchScalarGridSpec(
        num_scalar_prefetch=1,
        grid=(nb, n_pages // pps),
        in_specs=[row_spec] + cache_specs + cache_specs + [
            row_spec, row_spec,
            pl.BlockSpec((SUBLANES, bw), lambda b, j, pt: (b, dz_col_block)),
            pl.BlockSpec(lam.shape, lambda b, j, pt: (0, 0)),
            pl.BlockSpec((1, dh), lambda b, j, pt: (0, 0)),
        ],
        out_specs=row_spec,
        scratch_shapes=[pltpu.VMEM((rows, 1), F32), pltpu.VMEM((rows, 1), F32), pltpu.VMEM((rows, dh), F32)],
    )
    return pl.pallas_call(
        kern,
        grid_spec=grid_spec,
        out_shape=jax.ShapeDtypeStruct((nb * SUBLANES, bw), BF16),
        compiler_params=_params("parallel", "arbitrary"),
        name="diff_decode",
    )(page_table.reshape(-1), q, *([cache_k] * pps), *([cache_v] * pps), k_new, v_new, z, lam, norm_w)


def _merge_kernel(h_ref, a_ref, b_ref, c_ref, d_ref, m0_ref, m1_ref, m2_ref, m3_ref, wb_ref, o_ref):
    h = h_ref[...]
    acc = None
    for n, (br, wm) in enumerate(zip((a_ref, b_ref, c_ref, d_ref), (m0_ref, m1_ref, m2_ref, m3_ref))):
        gate = jax.nn.sigmoid(jnp.dot(h, wm[...], preferred_element_type=F32))
        term = gate * jnp.dot(br[...], wb_ref[n], preferred_element_type=F32)
        acc = term if acc is None else acc + term
    o_ref[...] = acc.astype(BF16)


def _merge(h, branches, w_merge, w_branch, layer):
    m, d = h.shape
    bw = branches[0].shape[1]
    tm = _tile(m, 512)
    tn = _tile(d, 256)
    nj = d // tn
    br_spec = pl.BlockSpec((tm, bw), lambda i, j: (i, 0))
    return pl.pallas_call(
        _merge_kernel,
        grid=(m // tm, nj),
        in_specs=[pl.BlockSpec((tm, d), lambda i, j: (i, 0))] + [br_spec] * 4 + [
            pl.BlockSpec((None, d, tn), lambda i, j, n=n: (layer, 0, n * nj + j)) for n in range(4)
        ] + [pl.BlockSpec((None, 4, bw, tn), lambda i, j: (layer, 0, 0, j))],
        out_specs=pl.BlockSpec((tm, tn), lambda i, j: (i, j)),
        out_shape=jax.ShapeDtypeStruct((m, d), BF16),
        compiler_params=_params("parallel", "arbitrary"),
        name="merge",
    )(h, *branches, w_merge, w_merge, w_merge, w_merge, w_branch)


def _outproj_kernel(a_ref, w_ref, x_ref, g_ref, o_ref):
    y = jnp.dot(a_ref[...], w_ref[...], preferred_element_type=F32)
    o_ref[...] = x_ref[...] + g_ref[0] * y


def _outproj(a, w, layer, x, gate, rows_per_mod):
    m, d = a.shape
    r = gate.shape[1]
    tm = r if r > 1 else _tile(rows_per_mod, 1024)
    per = rows_per_mod // tm if r == 1 else 1
    tn = _tile(d, 1024)
    return pl.pallas_call(
        _outproj_kernel,
        grid=(m // tm, d // tn),
        in_specs=[
            pl.BlockSpec((tm, d), lambda i, j: (i, 0)),
            pl.BlockSpec((None, d, tn), lambda i, j: (layer, 0, j)),
            pl.BlockSpec((tm, tn), lambda i, j: (i, j)),
            pl.BlockSpec((1, r, tn), lambda i, j: (i // per, 0, j)),
        ],
        out_specs=pl.BlockSpec((tm, tn), lambda i, j: (i, j)),
        out_shape=jax.ShapeDtypeStruct((m, d), F32),
        compiler_params=_params("parallel", "arbitrary"),
        name="out_proj",
    )(a, w, x, gate)


def _final_norm_kernel(x_ref, w_ref, o_ref):
    x = x_ref[...]
    o_ref[...] = x * lax.rsqrt(jnp.mean(x * x, axis=-1, keepdims=True) + EPS) * w_ref[...]


def _final_norm(x, w):
    m, d = x.shape
    tr = _tile(m, 256)
    return pl.pallas_call(
        _final_norm_kernel,
        grid=(m // tr,),
        in_specs=[pl.BlockSpec((tr, d), lambda i: (i, 0)), pl.BlockSpec((1, d), lambda i: (0, 0))],
        out_specs=pl.BlockSpec((tr, d), lambda i: (i, 0)),
        out_shape=jax.ShapeDtypeStruct((m, d), F32),
        compiler_params=_params("parallel"),
        name="final_norm",
    )(x, w.reshape(1, d))


def _rope_tables(pos, dh, dqk):
    rope_dim = dqk // 4
    half = rope_dim // 2
    inv_freq = ROPE_THETA ** (-jnp.arange(half, dtype=F32) * (2.0 / rope_dim))
    ang = pos.astype(F32)[:, None] * inv_freq[None, :]
    cos, sin = jnp.cos(ang), jnp.sin(ang)
    n = pos.shape[0]
    pad = jnp.zeros((n, dqk - rope_dim), F32)
    comp_cos = jnp.concatenate([cos, cos, pad + 1.0], axis=1)
    comp_sa = jnp.concatenate([-sin, jnp.zeros_like(sin), pad], axis=1)
    comp_sb = jnp.concatenate([jnp.zeros_like(sin), sin, pad], axis=1)
    reps = dh // dqk
    return tuple(jnp.tile(t, (1, reps)) for t in (comp_cos, comp_sa, comp_sb))


def _tail_tile(state, width):
    nb, k, _ = state.shape
    return jnp.concatenate([jnp.zeros((nb, SUBLANES - k, width), F32), state.astype(F32)], axis=1)


def _pad_time(x, tp):
    nb, t = x.shape[:2]
    return jnp.pad(x, [(0, 0), (0, tp - t)] + [(0, 0)] * (x.ndim - 2))


def kernel(x_prompt, x_sample, c_prompt, c_sample, state_conv_a, state_gdn_conv, state_gdn, state_gla, cache_k, cache_v, page_table, w_ada, b_ada, w_in, conv_a_w, gdn_conv_w, gdn_a_log, gdn_dt_bias, gdn_norm_w, diff_lambda, diff_norm_w, gla_w_gk2, gla_b_gk, gla_norm_w, w_branch, w_out, final_norm_w):
    nbp, tpp, d = x_prompt.shape
    nbs, tvs, _ = x_sample.shape
    depth = w_in.shape[0]
    bw = d // 4
    gdn_h = gdn_a_log.shape[1]
    diff_h = cache_k.shape[3]
    dh = cache_k.shape[4]
    dqk = dh // 2
    gla_h = state_gla.shape[2]
    gla_dk = state_gla.shape[3]
    rank = gla_w_gk2.shape[1]
    conv_a_k = conv_a_w.shape[1]
    gdn_k = gdn_conv_w.shape[1]
    past_len = page_table.shape[1] * cache_k.shape[2]
    tps = SUBLANES
    assert tvs <= tps and 2 * gdn_h + rank <= LANES and bw // gdn_h == LANES and dh == LANES

    off_gdn = 4 * bw
    off_side = off_gdn + 4 * bw
    off_diff = off_side + 2 * gdn_h
    off_gla = off_diff + 4 * bw
    off_lr = off_gla + 3 * bw
    off_merge = off_lr + rank
    w_t = jnp.swapaxes(w_in, 1, 2)
    w_ab = _repack(w_t, 0, off_side)
    w_cd = _repack(w_t, off_diff, off_lr - off_diff)
    w_side = _side_weights(w_t, off_side, 2 * gdn_h, off_lr, rank)
    w_merge = _repack(w_t, off_merge, 4 * d)
    w_branch_b = w_branch.astype(BF16)
    w_out_b = w_out.astype(BF16)
    wgk_pad = jnp.concatenate(
        [jnp.zeros((depth, 2 * gdn_h, gla_h * gla_dk), F32), gla_w_gk2,
         jnp.zeros((depth, LANES - 2 * gdn_h - rank, gla_h * gla_dk), F32)], axis=1).astype(BF16)
    lane_pad = jnp.zeros((depth, LANES - gdn_h), F32)
    alog_pad = jnp.concatenate([gdn_a_log, lane_pad], axis=1)
    dtb_pad = jnp.concatenate([gdn_dt_bias, lane_pad], axis=1)

    pool, page = cache_k.shape[1:3]
    cache_k_rows = cache_k.reshape(depth, pool, page * diff_h, dh)
    cache_v_rows = cache_v.reshape(depth, pool, page * diff_h, dh)

    n_c = nbp + nbs
    c_rows = -(-n_c // SUBLANES) * SUBLANES
    c_all = jnp.concatenate([c_prompt, c_sample, jnp.zeros((c_rows - n_c, d), F32)], axis=0)
    mod = _ada(c_all, w_ada, b_ada)

    groups = []
    groups.append(dict(
        nb=nbp, tp=tpp, tv=tpp, x=x_prompt.reshape(nbp * tpp, d), mod_rows=slice(0, nbp), per_row=False,
        conv_a=jnp.zeros((depth, nbp, conv_a_k - 1, bw), F32),
        gdn_conv=jnp.zeros((depth, nbp, gdn_k - 1, 3 * bw), F32),
        gdn_s=jnp.zeros((depth, nbp) + state_gdn.shape[2:], F32),
        gla_s=jnp.zeros((depth, nbp) + state_gla.shape[2:], F32),
        tabs=_rope_tables(jnp.arange(tpp, dtype=jnp.int32), dh, dqk), paged=False))
    groups.append(dict(
        nb=nbs, tp=tps, tv=tvs, x=_pad_time(x_sample, tps).reshape(nbs * tps, d), mod_rows=slice(nbp, n_c),
        per_row=True, conv_a=state_conv_a, gdn_conv=state_gdn_conv, gdn_s=state_gdn, gla_s=state_gla,
        tabs=_rope_tables(past_len + jnp.arange(tps, dtype=jnp.int32), dh, dqk), paged=True))

    results = []
    for g in groups:
        nb, tp, tv = g["nb"], g["tp"], g["tv"]
        x = g["x"]
        outs = [[] for _ in range(6)]
        for l in range(depth):
            m_l = mod[l, g["mod_rows"]]
            if g["per_row"]:
                m_l = jnp.repeat(m_l, tp, axis=0).reshape(1, nb * tp, 3 * d)
            else:
                m_l = m_l.reshape(nb, 1, 3 * d)
            shift, scale, gate = m_l[..., :d], m_l[..., d:2 * d], m_l[..., 2 * d:]
            h = _prenorm(x, scale, shift, tp)
            z_ab = _matmul(h, w_ab, l)
            z_c = _matmul(h, w_cd, l, 0, 4 * bw)
            z_d = _matmul(h, w_cd, l, 4 * bw, 3 * bw)
            zs = _matmul(h, w_side, l, tn_pref=LANES)

            out_a, tail_a = _conva(z_ab, _tail_tile(g["conv_a"][l], bw), conv_a_w[l], nb, tp, bw)
            out_b, tail_b, s_gdn = _gdn(
                z_ab, zs, _tail_tile(g["gdn_conv"][l], 3 * bw), gdn_conv_w[l], alog_pad[l:l + 1],
                dtb_pad[l:l + 1], gdn_norm_w[l:l + 1], g["gdn_s"][l].astype(F32), nb, tp, tv, bw, gdn_h, 1)
            q_r, k_r, v_r, new_k, new_v = _rope(z_c, g["tabs"], nb, tp, bw, diff_h, 0)
            lam_init = 0.8 - 0.6 * math.exp(-0.3 * l)
            if g["paged"]:
                out_c = _decode(q_r, k_r, v_r, z_c, cache_k_rows, cache_v_rows, page_table, l, diff_lambda[l],
                                diff_norm_w[l:l + 1], nb, tv, bw, diff_h, lam_init, 3)
            else:
                out_c = _flash(q_r, k_r, v_r, z_c, diff_lambda[l], diff_norm_w[l:l + 1], nb, tp, bw, diff_h,
                               lam_init, 3 * diff_h)
            out_d, s_gla_t = _gla(
                z_d, zs, wgk_pad[l], gla_b_gk[l:l + 1], gla_norm_w[l:l + 1],
                jnp.swapaxes(g["gla_s"][l].astype(F32), -1, -2), nb, tp, tv, bw, gla_h, 0)

            merged = _merge(h, (out_a, out_b, out_c, out_d), w_merge, w_branch_b, l)
            x = _outproj(merged, w_out_b, l, x, gate, tp)

            last = tv - (tp - SUBLANES)
            outs[0].append(tail_a[:, last - (conv_a_k - 1):last])
            outs[1].append(tail_b[:, last - (gdn_k - 1):last])
            outs[2].append(s_gdn)
            outs[3].append(jnp.swapaxes(s_gla_t, -1, -2))
            outs[4].append(new_k.reshape(nb, tp, diff_h, dh)[:, :tv])
            outs[5].append(new_v.reshape(nb, tp, diff_h, dh)[:, :tv])
        y = _final_norm(x, final_norm_w).reshape(nb, tp, d)[:, :tv]
        results.append((y, *[jnp.stack(o) for o in outs]))

    (yp, *sp), (ys, *ss) = results
    return (yp, ys, *sp, *ss)
```

```python
import functools
import math

import jax
import jax.numpy as jnp
import numpy as np
from jax import lax
from jax.experimental import pallas as pl
from jax.experimental.pallas import tpu as pltpu

F32 = jnp.float32
BF16 = jnp.bfloat16
EPS = 1e-6
GDN_CHUNK = 64
GDN_CHUNKS_PER_STEP = 4
DECODE_PAGES_PER_STEP = 8
FLASH_BLOCK = 256
FLASH_KV_BLOCKS = 2
FLASH_HEADS = 4
GLA_CHUNK = 32
GLA_CHUNKS_PER_STEP = 8
GLA_NORMALIZER = 16.0
ROPE_THETA = 500000.0
SUBLANES = 8
LANES = 128
VMEM_LIMIT = 56 * 1024 * 1024
HIGHEST = lax.Precision.HIGHEST

_NT = (((1,), (1,)), ((), ()))
_TN = (((0,), (0,)), ((), ()))
_BNN = (((2,), (1,)), ((0,), (0,)))
_BNT = (((2,), (2,)), ((0,), (0,)))
_BTN = (((1,), (1,)), ((0,), (0,)))


def _params(*sem):
    return pltpu.CompilerParams(dimension_semantics=sem, vmem_limit_bytes=VMEM_LIMIT)


def _silu(x):
    return x * jax.nn.sigmoid(x)


def _softplus(x):
    return jnp.maximum(x, 0.0) + jnp.log(1.0 + jnp.exp(-jnp.abs(x)))


def _dot(a, b):
    return jnp.dot(a.astype(BF16), b.astype(BF16), preferred_element_type=F32)


def _dot_nt(a, b):
    return lax.dot_general(a.astype(BF16), b.astype(BF16), _NT, preferred_element_type=F32)


def _dot_tn(a, b):
    return lax.dot_general(a.astype(BF16), b.astype(BF16), _TN, preferred_element_type=F32)


def _bdot(a, b, dims=_BNN):
    return lax.dot_general(a.astype(BF16), b.astype(BF16), dims, preferred_element_type=F32)


def _tile(n, pref):
    if n <= pref:
        return n
    t = pref
    while n % t:
        t //= 2
    return t


def _ada_kernel(c_ref, w_ref, b_ref, o_ref):
    c = c_ref[...]
    o_ref[0] = _dot(_silu(c), w_ref[0]) + b_ref[0]


def _ada(c_all, w_ada, b_ada):
    depth, d, n = w_ada.shape
    rows = c_all.shape[0]
    tn = _tile(n, 1024)
    return pl.pallas_call(
        _ada_kernel,
        grid=(depth, n // tn),
        in_specs=[
            pl.BlockSpec((rows, d), lambda l, j: (0, 0)),
            pl.BlockSpec((1, d, tn), lambda l, j: (l, 0, j)),
            pl.BlockSpec((1, 1, tn), lambda l, j: (l, 0, j)),
        ],
        out_specs=pl.BlockSpec((1, rows, tn), lambda l, j: (l, 0, j)),
        out_shape=jax.ShapeDtypeStruct((depth, rows, n), F32),
        compiler_params=_params("parallel", "parallel"),
        name="ada_mod",
    )(c_all, w_ada, b_ada.reshape(depth, 1, n))


def _repack_kernel(a_ref, o_ref):
    o_ref[...] = a_ref[0].T.astype(BF16)


def _repack(w_t, start, width):
    depth, _, k = w_t.shape
    tn = _tile(width, 1024)
    tk = _tile(k, 1024)
    return pl.pallas_call(
        _repack_kernel,
        grid=(depth, k // tk, width // tn),
        in_specs=[pl.BlockSpec((pl.Element(1), pl.Element(tn), pl.Element(tk)),
                               lambda l, i, j: (l, pl.multiple_of(start + j * tn, SUBLANES),
                                                pl.multiple_of(i * tk, LANES)))],
        out_specs=pl.BlockSpec((None, tk, tn), lambda l, i, j: (l, i, j)),
        out_shape=jax.ShapeDtypeStruct((depth, k, width), BF16),
        compiler_params=_params("parallel", "parallel", "parallel"),
        name="repack",
    )(w_t)


def _side_kernel(a_ref, b_ref, o_ref, *, n_a, n_b):
    lane = lax.broadcasted_iota(jnp.int32, o_ref.shape, 1)
    o_ref[...] = jnp.where(lane < n_a, a_ref[0].T, jnp.where(lane < n_a + n_b, b_ref[0].T, 0.0)).astype(BF16)


def _side_weights(w_t, start_a, n_a, start_b, n_b):
    depth, _, k = w_t.shape
    assert n_a + n_b <= LANES
    tk = _tile(k, 512)
    return pl.pallas_call(
        functools.partial(_side_kernel, n_a=n_a, n_b=n_b),
        grid=(depth, k // tk),
        in_specs=[
            pl.BlockSpec((pl.Element(1), pl.Element(LANES), pl.Element(tk)), lambda l, i: (l, start_a, i * tk)),
            pl.BlockSpec((pl.Element(1), pl.Element(LANES), pl.Element(tk)),
                         lambda l, i: (l, start_b - n_a, i * tk)),
        ],
        out_specs=pl.BlockSpec((None, tk, LANES), lambda l, i: (l, i, 0)),
        out_shape=jax.ShapeDtypeStruct((depth, k, LANES), BF16),
        compiler_params=_params("parallel", "parallel"),
        name="repack_side",
    )(w_t, w_t)


def _prenorm_kernel(x_ref, sc_ref, sh_ref, h_ref):
    x = x_ref[...]
    y = x * lax.rsqrt(jnp.mean(x * x, axis=-1, keepdims=True) + EPS)
    h_ref[...] = (y * (1.0 + sc_ref[0]) + sh_ref[0]).astype(BF16)


def _prenorm(x, scale, shift, rows_per_mod):
    m, d = x.shape
    r = scale.shape[1]
    tr = r if r > 1 else _tile(rows_per_mod, 256)
    per = rows_per_mod // tr if r == 1 else 1
    return pl.pallas_call(
        _prenorm_kernel,
        grid=(m // tr,),
        in_specs=[
            pl.BlockSpec((tr, d), lambda i: (i, 0)),
            pl.BlockSpec((1, r, d), lambda i: (i // per, 0, 0)),
            pl.BlockSpec((1, r, d), lambda i: (i // per, 0, 0)),
        ],
        out_specs=pl.BlockSpec((tr, d), lambda i: (i, 0)),
        out_shape=jax.ShapeDtypeStruct((m, d), BF16),
        compiler_params=_params("parallel"),
        name="prenorm",
    )(x, scale, shift)


def _mm_kernel(a_ref, w_ref, o_ref):
    o_ref[...] = jnp.dot(a_ref[...], w_ref[...], preferred_element_type=F32).astype(o_ref.dtype)


def _matmul(a, w, layer, col0=0, n=None, out_dtype=F32, tm_pref=1024, tn_pref=1024):
    m, k = a.shape
    n = w.shape[2] if n is None else n
    tm = _tile(m, tm_pref)
    tn = _tile(math.gcd(n, col0) if col0 else n, tn_pref)
    return pl.pallas_call(
        _mm_kernel,
        grid=(m // tm, n // tn),
        in_specs=[
            pl.BlockSpec((tm, k), lambda i, j: (i, 0)),
            pl.BlockSpec((None, k, tn), lambda i, j: (layer, 0, col0 // tn + j)),
        ],
        out_specs=pl.BlockSpec((tm, tn), lambda i, j: (i, j)),
        out_shape=jax.ShapeDtypeStruct((m, n), out_dtype),
        compiler_params=_params("parallel", "arbitrary"),
        name="in_proj",
    )(a, w)


def _causal_conv(carry_ref, st_ref, w_ref, u):
    n = u.shape[0]

    @pl.when(pl.program_id(1) == 0)
    def _():
        carry_ref[...] = st_ref[0]

    ext = jnp.concatenate([carry_ref[...], u], axis=0)
    w = w_ref[...]
    taps = w.shape[0]
    y = w[taps - 1:taps] * u
    for k in range(1, taps):
        y = y + w[taps - 1 - k:taps - k] * pltpu.roll(ext, k, 0)[SUBLANES:SUBLANES + n]
    tail = u[n - SUBLANES:n]
    carry_ref[...] = tail
    return y, tail


def _valid_rows(c, t_valid):
    t = pl.program_id(1)
    row = lax.broadcasted_iota(jnp.int32, (c, 1), 0) + t * c
    return row < t_valid


def _tri(c):
    row = lax.broadcasted_iota(jnp.int32, (c, c), 0)
    col = lax.broadcasted_iota(jnp.int32, (c, c), 1)
    return row, col


def _conva_kernel(z_ref, st_ref, w_ref, o_ref, tail_ref, carry, *, bw):
    z = z_ref[...]
    a_h, a_b, a_c, a_z = (z[:, i * bw:(i + 1) * bw] for i in range(4))
    y, tail = _causal_conv(carry, st_ref, w_ref, a_c * a_h)
    o_ref[...] = (a_b * y * _silu(a_z)).astype(BF16)
    tail_ref[0] = tail


def _conva(z, state_tile, w, nb, tp, bw):
    tt = _tile(tp, 256)
    nt = tp // tt
    return pl.pallas_call(
        functools.partial(_conva_kernel, bw=bw),
        grid=(nb, nt),
        in_specs=[
            pl.BlockSpec((tt, 4 * bw), lambda b, t: (b * nt + t, 0)),
            pl.BlockSpec((1, SUBLANES, bw), lambda b, t: (b, 0, 0)),
            pl.BlockSpec(w.shape, lambda b, t: (0, 0)),
        ],
        out_specs=[
            pl.BlockSpec((tt, bw), lambda b, t: (b * nt + t, 0)),
            pl.BlockSpec((1, SUBLANES, bw), lambda b, t: (b, 0, 0)),
        ],
        out_shape=[
            jax.ShapeDtypeStruct((nb * tp, bw), BF16),
            jax.ShapeDtypeStruct((nb, SUBLANES, bw), F32),
        ],
        scratch_shapes=[pltpu.VMEM((SUBLANES, bw), F32)],
        compiler_params=_params("parallel", "arbitrary"),
        name="branch_conv",
    )(z, state_tile, w)


INVERSE_BASE = 8


def _inverse_masks(row, col, c):
    base = min(INVERSE_BASE, c)
    masks = [(row // base) == (col // base)]
    b = base
    while b < c:
        masks.append(((row // (2 * b)) == (col // (2 * b))) & ((row // b) != (col // b)))
        b *= 2
    return masks


def _unit_lower_inverse(lmat, eye, masks, mm):
    base = min(INVERSE_BASE, lmat.shape[-1])
    p = jnp.where(masks[0], lmat, 0.0)
    x = eye - p
    for _ in range(max(0, int(math.log2(base)) - 1)):
        p = mm(p, p)
        x = x + mm(x, p)
    for m in masks[1:]:
        x = x - mm(mm(x, jnp.where(m, lmat, 0.0)), x)
    return x


def _gdn_kernel(z_ref, zs_ref, cst_ref, cw_ref, alog_ref, dtb_ref, nw_ref, s0_ref,
                o_ref, tail_ref, sout_ref, carry, s_scr, *, bw, heads, t_valid, c):
    t = pl.program_id(1)
    nt = pl.num_programs(1)

    @pl.when(t == 0)
    def _():
        s_scr[...] = s0_ref[0]

    z = z_ref[...]
    r = z.shape[0]
    n_chunks = r // c
    dk = bw // heads
    gz = z[:, 3 * bw:4 * bw]
    y, tail = _causal_conv(carry, cst_ref, cw_ref, z[:, :3 * bw])
    tail_ref[0] = tail
    qkv = _silu(y)

    valid = _valid_rows(r, t_valid).astype(F32)
    zs = zs_ref[...]
    g = valid * (-jnp.exp(alog_ref[...]) * _softplus(zs + dtb_ref[...]))
    beta = valid * jax.nn.sigmoid(zs)
    row_r, col_r = _tri(r)
    same_chunk = (row_r // c) == (col_r // c)
    gsum = jnp.dot(((row_r >= col_r) & same_chunk).astype(F32), g, precision=HIGHEST,
                   preferred_element_type=F32)
    row, col = _tri(c)
    incl = row >= col
    strict = row > col
    eye = (row == col).astype(F32)
    inv_masks = _inverse_masks(row, col, c)
    r128, c128 = _tri(LANES)
    eye128 = (r128 == c128).astype(F32)
    rows = [slice(ci * c, (ci + 1) * c) for ci in range(n_chunks)]
    gsum_t = [lax.dot_general(eye128, gsum[rs], _NT, precision=HIGHEST, preferred_element_type=F32)
              for rs in rows]

    def per(fn):
        return jnp.stack([fn(ci, rows[ci], h) for ci in range(n_chunks) for h in range(heads)])

    q = per(lambda ci, rs, h: qkv[rs, h * dk:(h + 1) * dk])
    k = per(lambda ci, rs, h: qkv[rs, bw + h * dk:bw + (h + 1) * dk])
    v = per(lambda ci, rs, h: qkv[rs, 2 * bw + h * dk:2 * bw + (h + 1) * dk])
    gcol = per(lambda ci, rs, h: jnp.broadcast_to(gsum[rs, h:h + 1], (c, dk)))
    bcol = per(lambda ci, rs, h: jnp.broadcast_to(beta[rs, heads + h:heads + h + 1], (c, dk)))
    vcol = per(lambda ci, rs, h: jnp.broadcast_to(valid[rs], (c, dk)))
    grow = per(lambda ci, rs, h: gsum_t[ci][h:h + 1, :])

    qn = q * lax.rsqrt(jnp.sum(q * q, axis=-1, keepdims=True) + EPS) * (dk ** -0.5)
    kn = vcol * (k * lax.rsqrt(jnp.sum(k * k, axis=-1, keepdims=True) + EPS))
    eg = jnp.exp(gcol)
    decay = jnp.exp(jnp.where(incl, gcol[:, :, :c] - grow, 0.0))
    kb = kn * bcol
    kq = _bdot(jnp.concatenate([kb, qn], axis=1), kn, _BNT)
    lmat = jnp.where(strict, kq[:, :c] * decay, 0.0)
    qk = jnp.where(incl, kq[:, c:] * decay, 0.0)
    tinv = _unit_lower_inverse(lmat, eye, inv_masks, _bdot)
    sol = _bdot(tinv, jnp.concatenate([v * bcol, kb * eg], axis=2))
    u = sol[:, :, :dk]
    wq = jnp.concatenate([sol[:, :, dk:], qn * eg], axis=1)
    g_last = gcol[:, c - 1:c, :]
    k_dec = kn * jnp.exp(g_last - gcol)
    s_decay = jnp.exp(g_last)

    s = s_scr[...]
    for ci in range(n_chunks):
        sel = slice(ci * heads, (ci + 1) * heads)
        ws = _bdot(wq[sel], s)
        v_new = u[sel] - ws[:, :c]
        o = ws[:, c:] + _bdot(qk[sel], v_new)
        s = s * s_decay[sel] + _bdot(k_dec[sel], v_new, _BTN)
        on = o * lax.rsqrt(jnp.mean(o * o, axis=-1, keepdims=True) + EPS) * nw_ref[...]
        for h in range(heads):
            sl = slice(h * dk, (h + 1) * dk)
            o_ref[rows[ci], sl] = (on[h] * _silu(gz[rows[ci], sl])).astype(BF16)
    s_scr[...] = s

    @pl.when(t == nt - 1)
    def _():
        sout_ref[0] = s


def _gdn(z, zs, conv_tile, conv_w, alog, dtb, norm_w, s0, nb, tp, t_valid, bw, heads, col_block):
    c = min(GDN_CHUNK, tp)
    r = min(GDN_CHUNK * GDN_CHUNKS_PER_STEP, tp)
    nt = tp // r
    dk = bw // heads
    return pl.pallas_call(
        functools.partial(_gdn_kernel, bw=bw, heads=heads, t_valid=t_valid, c=c),
        grid=(nb, nt),
        in_specs=[
            pl.BlockSpec((r, 4 * bw), lambda b, t: (b * nt + t, col_block)),
            pl.BlockSpec((r, LANES), lambda b, t: (b * nt + t, 0)),
            pl.BlockSpec((1, SUBLANES, 3 * bw), lambda b, t: (b, 0, 0)),
            pl.BlockSpec(conv_w.shape, lambda b, t: (0, 0)),
            pl.BlockSpec((1, LANES), lambda b, t: (0, 0)),
            pl.BlockSpec((1, LANES), lambda b, t: (0, 0)),
            pl.BlockSpec((1, dk), lambda b, t: (0, 0)),
            pl.BlockSpec((1, heads, dk, dk), lambda b, t: (b, 0, 0, 0)),
        ],
        out_specs=[
            pl.BlockSpec((r, bw), lambda b, t: (b * nt + t, 0)),
            pl.BlockSpec((1, SUBLANES, 3 * bw), lambda b, t: (b, 0, 0)),
            pl.BlockSpec((1, heads, dk, dk), lambda b, t: (b, 0, 0, 0)),
        ],
        out_shape=[
            jax.ShapeDtypeStruct((nb * tp, bw), BF16),
            jax.ShapeDtypeStruct((nb, SUBLANES, 3 * bw), F32),
            jax.ShapeDtypeStruct((nb, heads, dk, dk), F32),
        ],
        scratch_shapes=[pltpu.VMEM((SUBLANES, 3 * bw), F32), pltpu.VMEM((heads, dk, dk), F32)],
        compiler_params=_params("parallel", "arbitrary"),
        name="branch_gdn",
    )(z, zs, conv_tile, conv_w, alog, dtb, norm_w, s0)


def _gla_kernel(z_ref, zs_ref, wgk_ref, bgk_ref, nw_ref, s0_ref, o_ref, sout_ref, s_scr,
                *, bw, heads, t_valid, c):
    t = pl.program_id(1)
    nt = pl.num_programs(1)

    @pl.when(t == 0)
    def _():
        s_scr[...] = s0_ref[0]

    z = z_ref[...]
    r = z.shape[0]
    n_chunks = r // c
    dk = bw // (2 * heads)
    dv = bw // heads
    valid = _valid_rows(r, t_valid).astype(F32)
    gk = valid * (-_softplus(-(_dot(zs_ref[...], wgk_ref[...]) + bgk_ref[...])) / GLA_NORMALIZER)
    row_r, col_r = _tri(r)
    same_chunk = (row_r // c) == (col_r // c)
    gsum = jnp.dot(((row_r >= col_r) & same_chunk).astype(F32), gk, precision=HIGHEST,
                   preferred_element_type=F32)
    row, col = _tri(c)
    incl = row >= col
    lz = z[:, 2 * bw:3 * bw]
    rows = [slice(ci * c, (ci + 1) * c) for ci in range(n_chunks)]

    def per(fn):
        return jnp.stack([fn(rows[ci], h) for ci in range(n_chunks) for h in range(heads)])

    q = per(lambda rs, h: z[rs, h * dk:(h + 1) * dk]) * (dk ** -0.5)
    k = per(lambda rs, h: z[rs, bw // 2 + h * dk:bw // 2 + (h + 1) * dk] * valid[rs])
    v = per(lambda rs, h: z[rs, bw + h * dv:bw + (h + 1) * dv])
    g = per(lambda rs, h: gsum[rs, h * dk:(h + 1) * dk])
    q_dec = q * jnp.exp(g)
    attn = jnp.where(incl, _bdot(q_dec, k * jnp.exp(-g), _BNT), 0.0)
    o_intra = _bdot(attn, v)
    g_last = g[:, c - 1:c, :]
    k_dec = k * jnp.exp(g_last - g)
    s_decay = jnp.exp(g_last)

    s_t = s_scr[...]
    for ci in range(n_chunks):
        sel = slice(ci * heads, (ci + 1) * heads)
        o = o_intra[sel] + _bdot(q_dec[sel], s_t, _BNT)
        s_t = s_t * s_decay[sel] + _bdot(v[sel], k_dec[sel], _BTN)
        on = o * lax.rsqrt(jnp.mean(o * o, axis=-1, keepdims=True) + EPS) * nw_ref[...]
        for h in range(heads):
            sl = slice(h * dv, (h + 1) * dv)
            o_ref[rows[ci], sl] = (on[h] * _silu(lz[rows[ci], sl])).astype(BF16)
    s_scr[...] = s_t

    @pl.when(t == nt - 1)
    def _():
        sout_ref[0] = s_t


def _gla(z, zs, wgk_pad, bgk, norm_w, s0_t, nb, tp, t_valid, bw, heads, col_block):
    c = min(GLA_CHUNK, tp)
    r = min(GLA_CHUNK * GLA_CHUNKS_PER_STEP, tp)
    nt = tp // r
    dk = bw // (2 * heads)
    dv = bw // heads
    return pl.pallas_call(
        functools.partial(_gla_kernel, bw=bw, heads=heads, t_valid=t_valid, c=c),
        grid=(nb, nt),
        in_specs=[
            pl.BlockSpec((r, 3 * bw), lambda b, t: (b * nt + t, col_block)),
            pl.BlockSpec((r, LANES), lambda b, t: (b * nt + t, 0)),
            pl.BlockSpec(wgk_pad.shape, lambda b, t: (0, 0)),
            pl.BlockSpec(bgk.shape, lambda b, t: (0, 0)),
            pl.BlockSpec((1, dv), lambda b, t: (0, 0)),
            pl.BlockSpec((1, heads, dv, dk), lambda b, t: (b, 0, 0, 0)),
        ],
        out_specs=[
            pl.BlockSpec((r, bw), lambda b, t: (b * nt + t, 0)),
            pl.BlockSpec((1, heads, dv, dk), lambda b, t: (b, 0, 0, 0)),
        ],
        out_shape=[
            jax.ShapeDtypeStruct((nb * tp, bw), BF16),
            jax.ShapeDtypeStruct((nb, heads, dv, dk), F32),
        ],
        scratch_shapes=[pltpu.VMEM((heads, dv, dk), F32)],
        compiler_params=_params("parallel", "arbitrary"),
        name="branch_gla",
    )(z, zs, wgk_pad, bgk, norm_w, s0_t)


def _rope_kernel(z_ref, cos_ref, sa_ref, sb_ref, *refs, bw, heads, half, scale):
    q_ref, k_ref, v_ref, nk_ref, nv_ref = refs[-5:]
    z = z_ref[...]
    cos = cos_ref[...]
    sa = sa_ref[...]
    sb = sb_ref[...]
    dh = bw // heads

    def rope(x):
        return x * cos + pltpu.roll(x, dh - half, 1) * sa + pltpu.roll(x, half, 1) * sb

    for h in range(heads):
        sl = slice(h * dh, (h + 1) * dh)
        q = rope(z[:, sl])
        k = rope(z[:, bw + h * dh:bw + (h + 1) * dh])
        q_ref[:, sl] = (q * scale).astype(BF16)
        k_ref[:, sl] = k.astype(BF16)
        nk_ref[:, sl] = k
    v = z[:, 2 * bw:3 * bw]
    v_ref[...] = v.astype(BF16)
    nv_ref[...] = v


def _rope(z, tabs, nb, tp, bw, heads, col_block, layer, depth, stacks):
    tt = _tile(tp, 256)
    nt = tp // tt
    dh = bw // heads
    dqk = dh // 2
    kern = functools.partial(_rope_kernel, bw=bw, heads=heads, half=dqk // 8, scale=dqk ** -0.5)
    rows = nb * tp
    spec = pl.BlockSpec((tt, bw), lambda i: (i, 0))
    tspec = pl.BlockSpec((tt, dh), lambda i: (i % nt, 0))
    return pl.pallas_call(
        kern,
        grid=(rows // tt,),
        in_specs=[pl.BlockSpec((tt, 4 * bw), lambda i: (i, col_block)), tspec, tspec, tspec]
        + ([pl.BlockSpec(memory_space=pl.ANY)] * 2 if stacks else []),
        out_specs=[spec] * 3 + [pl.BlockSpec((None, tt, bw), lambda i: (layer, i, 0))] * 2,
        out_shape=[jax.ShapeDtypeStruct((rows, bw), BF16)] * 3
        + [jax.ShapeDtypeStruct((depth, rows, bw), F32)] * 2,
        input_output_aliases={4: 3, 5: 4} if stacks else {},
        compiler_params=_params("parallel"),
        name="diff_rope",
    )(z, *tabs, *(stacks or ()))


def _lambda(lam_ref, lam_init):
    lf = lam_ref[...]
    a = jnp.sum(lf[0:1] * lf[1:2], axis=-1, keepdims=True)
    b = jnp.sum(lf[2:3] * lf[3:4], axis=-1, keepdims=True)
    return jnp.exp(a) - jnp.exp(b) + lam_init


def _stack_components(q, dqk):
    lane = lax.broadcasted_iota(jnp.int32, q.shape, 1)
    zero = jnp.zeros_like(q)
    return jnp.concatenate([jnp.where(lane < dqk, q, zero), jnp.where(lane >= dqk, q, zero)], axis=0)


def _online_update(s, v, m, l, acc, mm=_dot):
    m_new = jnp.maximum(m, jnp.max(s, axis=-1, keepdims=True))
    alpha = jnp.exp(m - m_new)
    p = jnp.exp(s - m_new)
    return m_new, alpha * l + jnp.sum(p, axis=-1, keepdims=True), alpha * acc + mm(p, v)


def _attn_finish(acc, l, n, lam, lam_init, nw, dz):
    o = acc[:n] / l[:n] - lam * (acc[n:] / l[n:])
    on = o * lax.rsqrt(jnp.mean(o * o, axis=-1, keepdims=True) + EPS) * nw * (1.0 - lam_init)
    return (on * _silu(dz)).astype(BF16)


def _flash_kernel(q_ref, k_ref, v_ref, dz_ref, lam_ref, nw_ref, o_ref, *, dqk, lam_init, hb, tk):
    i = pl.program_id(2)
    tq = q_ref.shape[0]
    dh = 2 * dqk
    heads = [slice(h * dh, (h + 1) * dh) for h in range(hb)]
    qs = jnp.stack([_stack_components(q_ref[:, sl], dqk) for sl in heads])

    def block(j):
        start = pl.multiple_of(j * tk, tk)
        kb = k_ref[pl.ds(start, tk), :]
        vb = v_ref[pl.ds(start, tk), :]
        s = _bdot(qs, jnp.stack([kb[:, sl] for sl in heads]), _BNT)
        return s, jnp.stack([vb[:, sl] for sl in heads])

    def body(j, carry):
        s, v = block(j)
        return _online_update(s, v, *carry, mm=_bdot)

    init = (jnp.full((hb, 2 * tq, 1), -jnp.inf, F32), jnp.zeros((hb, 2 * tq, 1), F32),
            jnp.zeros((hb, 2 * tq, dh), F32))
    n_past = (i * tq) // tk
    carry = lax.fori_loop(0, n_past, body, init)
    s, v = block(n_past)
    row = i * tq + lax.broadcasted_iota(jnp.int32, s.shape, 1) % tq
    col = n_past * tk + lax.broadcasted_iota(jnp.int32, s.shape, 2)
    m, l, acc = _online_update(jnp.where(col <= row, s, -jnp.inf), v, *carry, mm=_bdot)
    lam = _lambda(lam_ref, lam_init)
    for h, sl in enumerate(heads):
        o_ref[:, sl] = _attn_finish(acc[h], l[h], tq, lam, lam_init, nw_ref[...], dz_ref[:, sl])


def _flash(q, k, v, z, lam, norm_w, nb, tp, bw, heads, lam_init, dz_col0):
    dh = bw // heads
    hb = _tile(heads, FLASH_HEADS)
    tq = _tile(tp, FLASH_BLOCK)
    nq = tp // tq
    tk = FLASH_KV_BLOCKS * tq if tp % (FLASH_KV_BLOCKS * tq) == 0 else tq
    kern = functools.partial(_flash_kernel, dqk=dh // 2, lam_init=lam_init, hb=hb, tk=tk)
    return pl.pallas_call(
        kern,
        grid=(nb, heads // hb, nq),
        in_specs=[
            pl.BlockSpec((tq, hb * dh), lambda b, h, i: (b * nq + i, h)),
            pl.BlockSpec((tp, hb * dh), lambda b, h, i: (b, h)),
            pl.BlockSpec((tp, hb * dh), lambda b, h, i: (b, h)),
            pl.BlockSpec((tq, hb * dh), lambda b, h, i: (b * nq + i, dz_col0 // hb + h)),
            pl.BlockSpec(lam.shape, lambda b, h, i: (0, 0)),
            pl.BlockSpec((1, dh), lambda b, h, i: (0, 0)),
        ],
        out_specs=pl.BlockSpec((tq, hb * dh), lambda b, h, i: (b * nq + i, h)),
        out_shape=jax.ShapeDtypeStruct((nb * tp, bw), BF16),
        compiler_params=_params("parallel", "parallel", "arbitrary"),
        name="diff_flash",
    )(q, k, v, z, lam, norm_w)


def _decode_kernel(pt_ref, q_ref, *refs, heads, dqk, t_valid, lam_init, pps):
    kc_refs, vc_refs = refs[:pps], refs[pps:2 * pps]
    kn_ref, vn_ref, dz_ref, lam_ref, nw_ref, o_ref, m_scr, l_scr, acc_scr = refs[2 * pps:]
    j = pl.program_id(1)
    nj = pl.num_programs(1)
    n = q_ref.shape[0]
    dh = 2 * dqk
    rows = 2 * n * heads
    head_lanes = [slice(h * dh, (h + 1) * dh) for h in range(heads)]

    @pl.when(j == 0)
    def _():
        m_scr[...] = jnp.full(m_scr.shape, -jnp.inf, F32)
        l_scr[...] = jnp.zeros(l_scr.shape, F32)
        acc_scr[...] = jnp.zeros(acc_scr.shape, F32)

    q_all = jnp.concatenate([_stack_components(q_ref[:, sl].astype(F32), dqk) for sl in head_lanes], axis=0)
    keys = kc_refs[0].shape[0]
    row_head = lax.broadcasted_iota(jnp.int32, (rows, keys), 0) // (2 * n)
    col_head = lax.broadcasted_iota(jnp.int32, (rows, keys), 1) % heads
    same_head = row_head == col_head
    m, l, acc = m_scr[...], l_scr[...], acc_scr[...]
    scores = [jnp.where(same_head, _dot_nt(q_all, kc_ref[...]), -jnp.inf) for kc_ref in kc_refs]
    m_new = m
    for s in scores:
        m_new = jnp.maximum(m_new, jnp.max(s, axis=-1, keepdims=True))
    alpha = jnp.exp(m - m_new)
    l = alpha * l
    acc = alpha * acc
    for s, vc_ref in zip(scores, vc_refs):
        p = jnp.exp(s - m_new)
        l = l + jnp.sum(p, axis=-1, keepdims=True)
        acc = acc + _dot(p, vc_ref[...])
    state = (m_new, l, acc)
    m_scr[...], l_scr[...], acc_scr[...] = state

    @pl.when(j == nj - 1)
    def _():
        k_new = jnp.concatenate([kn_ref[:, sl].astype(F32) for sl in head_lanes], axis=0)
        v_new = jnp.concatenate([vn_ref[:, sl].astype(F32) for sl in head_lanes], axis=0)
        r2 = lax.broadcasted_iota(jnp.int32, (rows, n * heads), 0)
        c2 = lax.broadcasted_iota(jnp.int32, (rows, n * heads), 1)
        visible = ((r2 // (2 * n)) == (c2 // n)) & ((c2 % n) <= (r2 % n)) & ((c2 % n) < t_valid)
        s = jnp.where(visible, _dot_nt(q_all, k_new), -jnp.inf)
        m, l, acc = _online_update(s, v_new, *state)
        lam = _lambda(lam_ref, lam_init)
        for h, sl in enumerate(head_lanes):
            blk = slice(h * 2 * n, (h + 1) * 2 * n)
            o_ref[:, sl] = _attn_finish(acc[blk], l[blk], n, lam, lam_init, nw_ref[...], dz_ref[:, sl])


def _decode(q, k_new, v_new, z, cache_k, cache_v, page_table, layer, lam, norm_w, nb, t_valid, bw, heads,
            lam_init, dz_col_block):
    n_pages = page_table.shape[1]
    dh = bw // heads
    pps = _tile(n_pages, DECODE_PAGES_PER_STEP)
    kern = functools.partial(_decode_kernel, heads=heads, dqk=dh // 2, t_valid=t_valid, lam_init=lam_init, pps=pps)
    row_spec = pl.BlockSpec((SUBLANES, bw), lambda b, j, pt: (b, 0))
    cache_specs = [
        pl.BlockSpec((None, None, cache_k.shape[2], dh),
                     lambda b, j, pt, i=i: (layer, pt[b * n_pages + j * pps + i], 0, 0))
        for i in range(pps)
    ]
    rows = 2 * SUBLANES * heads
    grid_spec = pltpu.PrefetchScalarGridSpec(
        num_scalar_prefetch=1,
        grid=(nb, n_pages // pps),
        in_specs=[row_spec] + cache_specs + cache_specs + [
            row_spec, row_spec,
            pl.BlockSpec((SUBLANES, bw), lambda b, j, pt: (b, dz_col_block)),
            pl.BlockSpec(lam.shape, lambda b, j, pt: (0, 0)),
            pl.BlockSpec((1, dh), lambda b, j, pt: (0, 0)),
        ],
        out_specs=row_spec,
        scratch_shapes=[pltpu.VMEM((rows, 1), F32), pltpu.VMEM((rows, 1), F32), pltpu.VMEM((rows, dh), F32)],
    )
    return pl.pallas_call(
        kern,
        grid_spec=grid_spec,
        out_shape=jax.ShapeDtypeStruct((nb * SUBLANES, bw), BF16),
        compiler_params=_params("parallel", "arbitrary"),
        name="diff_decode",
    )(page_table.reshape(-1), q, *([cache_k] * pps), *([cache_v] * pps), k_new, v_new, z, lam, norm_w)


def _merge_kernel(h_ref, a_ref, b_ref, c_ref, d_ref, m0_ref, m1_ref, m2_ref, m3_ref, wb_ref, o_ref):
    h = h_ref[...]
    acc = None
    for n, (br, wm) in enumerate(zip((a_ref, b_ref, c_ref, d_ref), (m0_ref, m1_ref, m2_ref, m3_ref))):
        gate = jax.nn.sigmoid(jnp.dot(h, wm[...], preferred_element_type=F32))
        term = gate * jnp.dot(br[...], wb_ref[n], preferred_element_type=F32)
        acc = term if acc is None else acc + term
    o_ref[...] = acc.astype(BF16)


def _merge(h, branches, w_merge, w_branch, layer):
    m, d = h.shape
    bw = branches[0].shape[1]
    tm = _tile(m, 512)
    tn = _tile(d, 256)
    nj = d // tn
    br_spec = pl.BlockSpec((tm, bw), lambda i, j: (i, 0))
    return pl.pallas_call(
        _merge_kernel,
        grid=(m // tm, nj),
        in_specs=[pl.BlockSpec((tm, d), lambda i, j: (i, 0))] + [br_spec] * 4 + [
            pl.BlockSpec((None, d, tn), lambda i, j, n=n: (layer, 0, n * nj + j)) for n in range(4)
        ] + [pl.BlockSpec((None, 4, bw, tn), lambda i, j: (layer, 0, 0, j))],
        out_specs=pl.BlockSpec((tm, tn), lambda i, j: (i, j)),
        out_shape=jax.ShapeDtypeStruct((m, d), BF16),
        compiler_params=_params("parallel", "arbitrary"),
        name="merge",
    )(h, *branches, w_merge, w_merge, w_merge, w_merge, w_branch)


def _outproj_kernel(a_ref, w_ref, x_ref, g_ref, o_ref):
    y = jnp.dot(a_ref[...], w_ref[...], preferred_element_type=F32)
    o_ref[...] = x_ref[...] + g_ref[0] * y


def _outproj(a, w, layer, x, gate, rows_per_mod):
    m, d = a.shape
    r = gate.shape[1]
    tm = r if r > 1 else _tile(rows_per_mod, 1024)
    per = rows_per_mod // tm if r == 1 else 1
    tn = _tile(d, 1024)
    return pl.pallas_call(
        _outproj_kernel,
        grid=(m // tm, d // tn),
        in_specs=[
            pl.BlockSpec((tm, d), lambda i, j: (i, 0)),
            pl.BlockSpec((None, d, tn), lambda i, j: (layer, 0, j)),
            pl.BlockSpec((tm, tn), lambda i, j: (i, j)),
            pl.BlockSpec((1, r, tn), lambda i, j: (i // per, 0, j)),
        ],
        out_specs=pl.BlockSpec((tm, tn), lambda i, j: (i, j)),
        out_shape=jax.ShapeDtypeStruct((m, d), F32),
        compiler_params=_params("parallel", "arbitrary"),
        name="out_proj",
    )(a, w, x, gate)


def _final_norm_kernel(x_ref, w_ref, o_ref):
    x = x_ref[...]
    o_ref[...] = x * lax.rsqrt(jnp.mean(x * x, axis=-1, keepdims=True) + EPS) * w_ref[...]


def _final_norm(x, w):
    m, d = x.shape
    tr = _tile(m, 256)
    return pl.pallas_call(
        _final_norm_kernel,
        grid=(m // tr,),
        in_specs=[pl.BlockSpec((tr, d), lambda i: (i, 0)), pl.BlockSpec((1, d), lambda i: (0, 0))],
        out_specs=pl.BlockSpec((tr, d), lambda i: (i, 0)),
        out_shape=jax.ShapeDtypeStruct((m, d), F32),
        compiler_params=_params("parallel"),
        name="final_norm",
    )(x, w.reshape(1, d))


def _rope_tables(pos, dh, dqk):
    rope_dim = dqk // 4
    half = rope_dim // 2
    inv_freq = ROPE_THETA ** (-jnp.arange(half, dtype=F32) * (2.0 / rope_dim))
    ang = pos.astype(F32)[:, None] * inv_freq[None, :]
    cos, sin = jnp.cos(ang), jnp.sin(ang)
    n = pos.shape[0]
    pad = jnp.zeros((n, dqk - rope_dim), F32)
    comp_cos = jnp.concatenate([cos, cos, pad + 1.0], axis=1)
    comp_sa = jnp.concatenate([-sin, jnp.zeros_like(sin), pad], axis=1)
    comp_sb = jnp.concatenate([jnp.zeros_like(sin), sin, pad], axis=1)
    reps = dh // dqk
    return tuple(jnp.tile(t, (1, reps)) for t in (comp_cos, comp_sa, comp_sb))


def _tail_tile(state, width):
    nb, k, _ = state.shape
    return jnp.concatenate([jnp.zeros((nb, SUBLANES - k, width), F32), state.astype(F32)], axis=1)


def _pad_time(x, tp):
    nb, t = x.shape[:2]
    return jnp.pad(x, [(0, 0), (0, tp - t)] + [(0, 0)] * (x.ndim - 2))


def kernel(x_prompt, x_sample, c_prompt, c_sample, state_conv_a, state_gdn_conv, state_gdn, state_gla, cache_k, cache_v, page_table, w_ada, b_ada, w_in, conv_a_w, gdn_conv_w, gdn_a_log, gdn_dt_bias, gdn_norm_w, diff_lambda, diff_norm_w, gla_w_gk2, gla_b_gk, gla_norm_w, w_branch, w_out, final_norm_w):
    nbp, tpp, d = x_prompt.shape
    nbs, tvs, _ = x_sample.shape
    depth = w_in.shape[0]
    bw = d // 4
    gdn_h = gdn_a_log.shape[1]
    diff_h = cache_k.shape[3]
    dh = cache_k.shape[4]
    dqk = dh // 2
    gla_h = state_gla.shape[2]
    gla_dk = state_gla.shape[3]
    rank = gla_w_gk2.shape[1]
    conv_a_k = conv_a_w.shape[1]
    gdn_k = gdn_conv_w.shape[1]
    past_len = page_table.shape[1] * cache_k.shape[2]
    tps = SUBLANES
    assert tvs <= tps and 2 * gdn_h + rank <= LANES and bw // gdn_h == LANES and dh == LANES

    off_gdn = 4 * bw
    off_side = off_gdn + 4 * bw
    off_diff = off_side + 2 * gdn_h
    off_gla = off_diff + 4 * bw
    off_lr = off_gla + 3 * bw
    off_merge = off_lr + rank
    w_t = jnp.swapaxes(w_in, 1, 2)
    w_ab = _repack(w_t, 0, off_side)
    w_cd = _repack(w_t, off_diff, off_lr - off_diff)
    w_side = _side_weights(w_t, off_side, 2 * gdn_h, off_lr, rank)
    w_merge = _repack(w_t, off_merge, 4 * d)
    w_branch_b = w_branch.astype(BF16)
    w_out_b = w_out.astype(BF16)
    wgk_pad = jnp.concatenate(
        [jnp.zeros((depth, 2 * gdn_h, gla_h * gla_dk), F32), gla_w_gk2,
         jnp.zeros((depth, LANES - 2 * gdn_h - rank, gla_h * gla_dk), F32)], axis=1).astype(BF16)
    lane_pad = jnp.zeros((depth, LANES - gdn_h), F32)
    alog_pad = jnp.concatenate([gdn_a_log, lane_pad], axis=1)
    dtb_pad = jnp.concatenate([gdn_dt_bias, lane_pad], axis=1)

    pool, page = cache_k.shape[1:3]
    cache_k_rows = cache_k.reshape(depth, pool, page * diff_h, dh)
    cache_v_rows = cache_v.reshape(depth, pool, page * diff_h, dh)

    n_c = nbp + nbs
    c_rows = -(-n_c // SUBLANES) * SUBLANES
    c_all = jnp.concatenate([c_prompt, c_sample, jnp.zeros((c_rows - n_c, d), F32)], axis=0)
    mod = _ada(c_all, w_ada, b_ada)

    groups = []
    groups.append(dict(
        nb=nbp, tp=tpp, tv=tpp, x=x_prompt.reshape(nbp * tpp, d), mod_rows=slice(0, nbp), per_row=False,
        conv_a=jnp.zeros((depth, nbp, conv_a_k - 1, bw), F32),
        gdn_conv=jnp.zeros((depth, nbp, gdn_k - 1, 3 * bw), F32),
        gdn_s=jnp.zeros((depth, nbp) + state_gdn.shape[2:], F32),
        gla_s=jnp.zeros((depth, nbp) + state_gla.shape[2:], F32),
        tabs=_rope_tables(jnp.arange(tpp, dtype=jnp.int32), dh, dqk), paged=False))
    groups.append(dict(
        nb=nbs, tp=tps, tv=tvs, x=_pad_time(x_sample, tps).reshape(nbs * tps, d), mod_rows=slice(nbp, n_c),
        per_row=True, conv_a=state_conv_a, gdn_conv=state_gdn_conv, gdn_s=state_gdn, gla_s=state_gla,
        tabs=_rope_tables(past_len + jnp.arange(tps, dtype=jnp.int32), dh, dqk), paged=True))

    results = []
    for g in groups:
        nb, tp, tv = g["nb"], g["tp"], g["tv"]
        x = g["x"]
        outs = [[] for _ in range(4)]
        kv_stacks = None
        for l in range(depth):
            m_l = mod[l, g["mod_rows"]]
            if g["per_row"]:
                m_l = jnp.repeat(m_l, tp, axis=0).reshape(1, nb * tp, 3 * d)
            else:
                m_l = m_l.reshape(nb, 1, 3 * d)
            shift, scale, gate = m_l[..., :d], m_l[..., d:2 * d], m_l[..., 2 * d:]
            h = _prenorm(x, scale, shift, tp)
            z_ab = _matmul(h, w_ab, l)
            z_c = _matmul(h, w_cd, l, 0, 4 * bw)
            z_d = _matmul(h, w_cd, l, 4 * bw, 3 * bw)
            zs = _matmul(h, w_side, l, tn_pref=LANES)

            out_a, tail_a = _conva(z_ab, _tail_tile(g["conv_a"][l], bw), conv_a_w[l], nb, tp, bw)
            out_b, tail_b, s_gdn = _gdn(
                z_ab, zs, _tail_tile(g["gdn_conv"][l], 3 * bw), gdn_conv_w[l], alog_pad[l:l + 1],
                dtb_pad[l:l + 1], gdn_norm_w[l:l + 1], g["gdn_s"][l].astype(F32), nb, tp, tv, bw, gdn_h, 1)
            q_r, k_r, v_r, *kv_stacks = _rope(z_c, g["tabs"], nb, tp, bw, diff_h, 0, l, depth, kv_stacks)
            lam_init = 0.8 - 0.6 * math.exp(-0.3 * l)
            if g["paged"]:
                out_c = _decode(q_r, k_r, v_r, z_c, cache_k_rows, cache_v_rows, page_table, l, diff_lambda[l],
                                diff_norm_w[l:l + 1], nb, tv, bw, diff_h, lam_init, 3)
            else:
                out_c = _flash(q_r, k_r, v_r, z_c, diff_lambda[l], diff_norm_w[l:l + 1], nb, tp, bw, diff_h,
                               lam_init, 3 * diff_h)
            out_d, s_gla_t = _gla(
                z_d, zs, wgk_pad[l], gla_b_gk[l:l + 1], gla_norm_w[l:l + 1],
                jnp.swapaxes(g["gla_s"][l].astype(F32), -1, -2), nb, tp, tv, bw, gla_h, 0)

            merged = _merge(h, (out_a, out_b, out_c, out_d), w_merge, w_branch_b, l)
            x = _outproj(merged, w_out_b, l, x, gate, tp)

            last = tv - (tp - SUBLANES)
            outs[0].append(tail_a[:, last - (conv_a_k - 1):last])
            outs[1].append(tail_b[:, last - (gdn_k - 1):last])
            outs[2].append(s_gdn)
            outs[3].append(jnp.swapaxes(s_gla_t, -1, -2))
        y = _final_norm(x, final_norm_w).reshape(nb, tp, d)[:, :tv]
        new_k, new_v = (s.reshape(depth, nb, tp, diff_h, dh)[:, :, :tv] for s in kv_stacks)
        results.append((y, *[jnp.stack(o) for o in outs], new_k, new_v))

    (yp, *sp), (ys, *ss) = results
    return (yp, ys, *sp, *ss)
```

```python
import functools
import math

import jax
import jax.numpy as jnp
import numpy as np
from jax import lax
from jax.experimental import pallas as pl
from jax.experimental.pallas import tpu as pltpu

F32 = jnp.float32
BF16 = jnp.bfloat16
EPS = 1e-6
GDN_CHUNK = 64
GDN_CHUNKS_PER_STEP = 4
DECODE_PAGES_PER_STEP = 16
FLASH_BLOCK = 256
FLASH_KV_BLOCKS = 2
FLASH_HEADS = 4
GLA_CHUNK = 32
GLA_CHUNKS_PER_STEP = 8
GLA_NORMALIZER = 16.0
ROPE_THETA = 500000.0
SUBLANES = 8
LANES = 128
VMEM_LIMIT = 56 * 1024 * 1024
HIGHEST = lax.Precision.HIGHEST

_NT = (((1,), (1,)), ((), ()))
_TN = (((0,), (0,)), ((), ()))
_BNN = (((2,), (1,)), ((0,), (0,)))
_BNT = (((2,), (2,)), ((0,), (0,)))
_BTN = (((1,), (1,)), ((0,), (0,)))


def _params(*sem):
    return pltpu.CompilerParams(dimension_semantics=sem, vmem_limit_bytes=VMEM_LIMIT)


def _silu(x):
    return x * jax.nn.sigmoid(x)


def _softplus(x):
    return jnp.maximum(x, 0.0) + jnp.log(1.0 + jnp.exp(-jnp.abs(x)))


def _dot(a, b):
    return jnp.dot(a.astype(BF16), b.astype(BF16), preferred_element_type=F32)


def _dot_nt(a, b):
    return lax.dot_general(a.astype(BF16), b.astype(BF16), _NT, preferred_element_type=F32)


def _dot_tn(a, b):
    return lax.dot_general(a.astype(BF16), b.astype(BF16), _TN, preferred_element_type=F32)


def _bdot(a, b, dims=_BNN):
    return lax.dot_general(a.astype(BF16), b.astype(BF16), dims, preferred_element_type=F32)


def _tile(n, pref):
    if n <= pref:
        return n
    t = pref
    while n % t:
        t //= 2
    return t


def _ada_kernel(c_ref, w_ref, b_ref, o_ref):
    c = c_ref[...]
    o_ref[0] = _dot(_silu(c), w_ref[0]) + b_ref[0]


def _ada(c_all, w_ada, b_ada):
    depth, d, n = w_ada.shape
    rows = c_all.shape[0]
    tn = _tile(n, 1024)
    return pl.pallas_call(
        _ada_kernel,
        grid=(depth, n // tn),
        in_specs=[
            pl.BlockSpec((rows, d), lambda l, j: (0, 0)),
            pl.BlockSpec((1, d, tn), lambda l, j: (l, 0, j)),
            pl.BlockSpec((1, 1, tn), lambda l, j: (l, 0, j)),
        ],
        out_specs=pl.BlockSpec((1, rows, tn), lambda l, j: (l, 0, j)),
        out_shape=jax.ShapeDtypeStruct((depth, rows, n), F32),
        compiler_params=_params("parallel", "parallel"),
        name="ada_mod",
    )(c_all, w_ada, b_ada.reshape(depth, 1, n))


def _repack_kernel(a_ref, o_ref):
    o_ref[...] = a_ref[0].T.astype(BF16)


def _repack(w_t, start, width):
    depth, _, k = w_t.shape
    tn = _tile(width, 1024)
    tk = _tile(k, 1024)
    return pl.pallas_call(
        _repack_kernel,
        grid=(depth, k // tk, width // tn),
        in_specs=[pl.BlockSpec((pl.Element(1), pl.Element(tn), pl.Element(tk)),
                               lambda l, i, j: (l, pl.multiple_of(start + j * tn, SUBLANES),
                                                pl.multiple_of(i * tk, LANES)))],
        out_specs=pl.BlockSpec((None, tk, tn), lambda l, i, j: (l, i, j)),
        out_shape=jax.ShapeDtypeStruct((depth, k, width), BF16),
        compiler_params=_params("parallel", "parallel", "parallel"),
        name="repack",
    )(w_t)


def _side_kernel(a_ref, b_ref, o_ref, *, n_a, n_b):
    lane = lax.broadcasted_iota(jnp.int32, o_ref.shape, 1)
    o_ref[...] = jnp.where(lane < n_a, a_ref[0].T, jnp.where(lane < n_a + n_b, b_ref[0].T, 0.0)).astype(BF16)


def _side_weights(w_t, start_a, n_a, start_b, n_b):
    depth, _, k = w_t.shape
    assert n_a + n_b <= LANES
    tk = _tile(k, 512)
    return pl.pallas_call(
        functools.partial(_side_kernel, n_a=n_a, n_b=n_b),
        grid=(depth, k // tk),
        in_specs=[
            pl.BlockSpec((pl.Element(1), pl.Element(LANES), pl.Element(tk)), lambda l, i: (l, start_a, i * tk)),
            pl.BlockSpec((pl.Element(1), pl.Element(LANES), pl.Element(tk)),
                         lambda l, i: (l, start_b - n_a, i * tk)),
        ],
        out_specs=pl.BlockSpec((None, tk, LANES), lambda l, i: (l, i, 0)),
        out_shape=jax.ShapeDtypeStruct((depth, k, LANES), BF16),
        compiler_params=_params("parallel", "parallel"),
        name="repack_side",
    )(w_t, w_t)


def _prenorm_kernel(x_ref, sc_ref, sh_ref, h_ref):
    x = x_ref[...]
    y = x * lax.rsqrt(jnp.mean(x * x, axis=-1, keepdims=True) + EPS)
    h_ref[...] = (y * (1.0 + sc_ref[0]) + sh_ref[0]).astype(BF16)


def _prenorm(x, scale, shift, rows_per_mod):
    m, d = x.shape
    r = scale.shape[1]
    tr = r if r > 1 else _tile(rows_per_mod, 256)
    per = rows_per_mod // tr if r == 1 else 1
    return pl.pallas_call(
        _prenorm_kernel,
        grid=(m // tr,),
        in_specs=[
            pl.BlockSpec((tr, d), lambda i: (i, 0)),
            pl.BlockSpec((1, r, d), lambda i: (i // per, 0, 0)),
            pl.BlockSpec((1, r, d), lambda i: (i // per, 0, 0)),
        ],
        out_specs=pl.BlockSpec((tr, d), lambda i: (i, 0)),
        out_shape=jax.ShapeDtypeStruct((m, d), BF16),
        compiler_params=_params("parallel"),
        name="prenorm",
    )(x, scale, shift)


def _mm_kernel(a_ref, w_ref, o_ref):
    o_ref[...] = jnp.dot(a_ref[...], w_ref[...], preferred_element_type=F32).astype(o_ref.dtype)


def _matmul(a, w, layer, col0=0, n=None, out_dtype=F32, tm_pref=1024, tn_pref=1024):
    m, k = a.shape
    n = w.shape[2] if n is None else n
    tm = _tile(m, tm_pref)
    tn = _tile(math.gcd(n, col0) if col0 else n, tn_pref)
    return pl.pallas_call(
        _mm_kernel,
        grid=(m // tm, n // tn),
        in_specs=[
            pl.BlockSpec((tm, k), lambda i, j: (i, 0)),
            pl.BlockSpec((None, k, tn), lambda i, j: (layer, 0, col0 // tn + j)),
        ],
        out_specs=pl.BlockSpec((tm, tn), lambda i, j: (i, j)),
        out_shape=jax.ShapeDtypeStruct((m, n), out_dtype),
        compiler_params=_params("parallel", "arbitrary"),
        name="in_proj",
    )(a, w)


def _causal_conv(carry_ref, st_ref, w_ref, u):
    n = u.shape[0]

    @pl.when(pl.program_id(1) == 0)
    def _():
        carry_ref[...] = st_ref[0]

    ext = jnp.concatenate([carry_ref[...], u], axis=0)
    w = w_ref[...]
    taps = w.shape[0]
    y = w[taps - 1:taps] * u
    for k in range(1, taps):
        y = y + w[taps - 1 - k:taps - k] * pltpu.roll(ext, k, 0)[SUBLANES:SUBLANES + n]
    tail = u[n - SUBLANES:n]
    carry_ref[...] = tail
    return y, tail


def _valid_rows(c, t_valid):
    t = pl.program_id(1)
    row = lax.broadcasted_iota(jnp.int32, (c, 1), 0) + t * c
    return row < t_valid


def _tri(c):
    row = lax.broadcasted_iota(jnp.int32, (c, c), 0)
    col = lax.broadcasted_iota(jnp.int32, (c, c), 1)
    return row, col


def _conva_kernel(z_ref, st_ref, w_ref, o_ref, tail_ref, carry, *, bw):
    z = z_ref[...]
    a_h, a_b, a_c, a_z = (z[:, i * bw:(i + 1) * bw] for i in range(4))
    y, tail = _causal_conv(carry, st_ref, w_ref, a_c * a_h)
    o_ref[...] = (a_b * y * _silu(a_z)).astype(BF16)
    tail_ref[0] = tail


def _conva(z, state_tile, w, nb, tp, bw):
    tt = _tile(tp, 256)
    nt = tp // tt
    return pl.pallas_call(
        functools.partial(_conva_kernel, bw=bw),
        grid=(nb, nt),
        in_specs=[
            pl.BlockSpec((tt, 4 * bw), lambda b, t: (b * nt + t, 0)),
            pl.BlockSpec((1, SUBLANES, bw), lambda b, t: (b, 0, 0)),
            pl.BlockSpec(w.shape, lambda b, t: (0, 0)),
        ],
        out_specs=[
            pl.BlockSpec((tt, bw), lambda b, t: (b * nt + t, 0)),
            pl.BlockSpec((1, SUBLANES, bw), lambda b, t: (b, 0, 0)),
        ],
        out_shape=[
            jax.ShapeDtypeStruct((nb * tp, bw), BF16),
            jax.ShapeDtypeStruct((nb, SUBLANES, bw), F32),
        ],
        scratch_shapes=[pltpu.VMEM((SUBLANES, bw), F32)],
        compiler_params=_params("parallel", "arbitrary"),
        name="branch_conv",
    )(z, state_tile, w)


INVERSE_BASE = 8


def _inverse_masks(row, col, c):
    base = min(INVERSE_BASE, c)
    masks = [(row // base) == (col // base)]
    b = base
    while b < c:
        masks.append(((row // (2 * b)) == (col // (2 * b))) & ((row // b) != (col // b)))
        b *= 2
    return masks


def _unit_lower_inverse(lmat, eye, masks, mm):
    base = min(INVERSE_BASE, lmat.shape[-1])
    p = jnp.where(masks[0], lmat, 0.0)
    x = eye - p
    for _ in range(max(0, int(math.log2(base)) - 1)):
        p = mm(p, p)
        x = x + mm(x, p)
    for m in masks[1:]:
        x = x - mm(mm(x, jnp.where(m, lmat, 0.0)), x)
    return x


def _gdn_kernel(z_ref, zs_ref, cst_ref, cw_ref, alog_ref, dtb_ref, nw_ref, s0_ref,
                o_ref, tail_ref, sout_ref, carry, s_scr, *, bw, heads, t_valid, c):
    t = pl.program_id(1)
    nt = pl.num_programs(1)

    @pl.when(t == 0)
    def _():
        s_scr[...] = s0_ref[0]

    z = z_ref[...]
    r = z.shape[0]
    n_chunks = r // c
    dk = bw // heads
    gz = z[:, 3 * bw:4 * bw]
    y, tail = _causal_conv(carry, cst_ref, cw_ref, z[:, :3 * bw])
    tail_ref[0] = tail
    qkv = _silu(y)

    valid = _valid_rows(r, t_valid).astype(F32)
    zs = zs_ref[...]
    g = valid * (-jnp.exp(alog_ref[...]) * _softplus(zs + dtb_ref[...]))
    beta = valid * jax.nn.sigmoid(zs)
    row_r, col_r = _tri(r)
    same_chunk = (row_r // c) == (col_r // c)
    gsum = jnp.dot(((row_r >= col_r) & same_chunk).astype(F32), g, precision=HIGHEST,
                   preferred_element_type=F32)
    row, col = _tri(c)
    incl = row >= col
    strict = row > col
    eye = (row == col).astype(F32)
    inv_masks = _inverse_masks(row, col, c)
    r128, c128 = _tri(LANES)
    eye128 = (r128 == c128).astype(F32)
    rows = [slice(ci * c, (ci + 1) * c) for ci in range(n_chunks)]
    gsum_t = [lax.dot_general(eye128, gsum[rs], _NT, precision=HIGHEST, preferred_element_type=F32)
              for rs in rows]

    def per(fn):
        return jnp.stack([fn(ci, rows[ci], h) for ci in range(n_chunks) for h in range(heads)])

    q = per(lambda ci, rs, h: qkv[rs, h * dk:(h + 1) * dk])
    k = per(lambda ci, rs, h: qkv[rs, bw + h * dk:bw + (h + 1) * dk])
    v = per(lambda ci, rs, h: qkv[rs, 2 * bw + h * dk:2 * bw + (h + 1) * dk])
    gcol = per(lambda ci, rs, h: jnp.broadcast_to(gsum[rs, h:h + 1], (c, dk)))
    bcol = per(lambda ci, rs, h: jnp.broadcast_to(beta[rs, heads + h:heads + h + 1], (c, dk)))
    vcol = per(lambda ci, rs, h: jnp.broadcast_to(valid[rs], (c, dk)))
    grow = per(lambda ci, rs, h: gsum_t[ci][h:h + 1, :])

    qn = q * lax.rsqrt(jnp.sum(q * q, axis=-1, keepdims=True) + EPS) * (dk ** -0.5)
    kn = vcol * (k * lax.rsqrt(jnp.sum(k * k, axis=-1, keepdims=True) + EPS))
    eg = jnp.exp(gcol)
    decay = jnp.exp(jnp.where(incl, gcol[:, :, :c] - grow, 0.0))
    kb = kn * bcol
    kq = _bdot(jnp.concatenate([kb, qn], axis=1), kn, _BNT)
    lmat = jnp.where(strict, kq[:, :c] * decay, 0.0)
    qk = jnp.where(incl, kq[:, c:] * decay, 0.0)
    tinv = _unit_lower_inverse(lmat, eye, inv_masks, _bdot)
    sol = _bdot(tinv, jnp.concatenate([v * bcol, kb * eg], axis=2))
    u = sol[:, :, :dk]
    wq = jnp.concatenate([sol[:, :, dk:], qn * eg], axis=1)
    g_last = gcol[:, c - 1:c, :]
    k_dec = kn * jnp.exp(g_last - gcol)
    s_decay = jnp.exp(g_last)

    s = s_scr[...]
    for ci in range(n_chunks):
        sel = slice(ci * heads, (ci + 1) * heads)
        ws = _bdot(wq[sel], s)
        v_new = u[sel] - ws[:, :c]
        o = ws[:, c:] + _bdot(qk[sel], v_new)
        s = s * s_decay[sel] + _bdot(k_dec[sel], v_new, _BTN)
        on = o * lax.rsqrt(jnp.mean(o * o, axis=-1, keepdims=True) + EPS) * nw_ref[...]
        for h in range(heads):
            sl = slice(h * dk, (h + 1) * dk)
            o_ref[rows[ci], sl] = (on[h] * _silu(gz[rows[ci], sl])).astype(BF16)
    s_scr[...] = s

    @pl.when(t == nt - 1)
    def _():
        sout_ref[0] = s


def _gdn(z, zs, conv_tile, conv_w, alog, dtb, norm_w, s0, nb, tp, t_valid, bw, heads, col_block):
    c = min(GDN_CHUNK, tp)
    r = min(GDN_CHUNK * GDN_CHUNKS_PER_STEP, tp)
    nt = tp // r
    dk = bw // heads
    return pl.pallas_call(
        functools.partial(_gdn_kernel, bw=bw, heads=heads, t_valid=t_valid, c=c),
        grid=(nb, nt),
        in_specs=[
            pl.BlockSpec((r, 4 * bw), lambda b, t: (b * nt + t, col_block)),
            pl.BlockSpec((r, LANES), lambda b, t: (b * nt + t, 0)),
            pl.BlockSpec((1, SUBLANES, 3 * bw), lambda b, t: (b, 0, 0)),
            pl.BlockSpec(conv_w.shape, lambda b, t: (0, 0)),
            pl.BlockSpec((1, LANES), lambda b, t: (0, 0)),
            pl.BlockSpec((1, LANES), lambda b, t: (0, 0)),
            pl.BlockSpec((1, dk), lambda b, t: (0, 0)),
            pl.BlockSpec((1, heads, dk, dk), lambda b, t: (b, 0, 0, 0)),
        ],
        out_specs=[
            pl.BlockSpec((r, bw), lambda b, t: (b * nt + t, 0)),
            pl.BlockSpec((1, SUBLANES, 3 * bw), lambda b, t: (b, 0, 0)),
            pl.BlockSpec((1, heads, dk, dk), lambda b, t: (b, 0, 0, 0)),
        ],
        out_shape=[
            jax.ShapeDtypeStruct((nb * tp, bw), BF16),
            jax.ShapeDtypeStruct((nb, SUBLANES, 3 * bw), F32),
            jax.ShapeDtypeStruct((nb, heads, dk, dk), F32),
        ],
        scratch_shapes=[pltpu.VMEM((SUBLANES, 3 * bw), F32), pltpu.VMEM((heads, dk, dk), F32)],
        compiler_params=_params("parallel", "arbitrary"),
        name="branch_gdn",
    )(z, zs, conv_tile, conv_w, alog, dtb, norm_w, s0)


def _gla_kernel(z_ref, zs_ref, wgk_ref, bgk_ref, nw_ref, s0_ref, o_ref, sout_ref, s_scr,
                *, bw, heads, t_valid, c):
    t = pl.program_id(1)
    nt = pl.num_programs(1)

    @pl.when(t == 0)
    def _():
        s_scr[...] = s0_ref[0]

    z = z_ref[...]
    r = z.shape[0]
    n_chunks = r // c
    dk = bw // (2 * heads)
    dv = bw // heads
    valid = _valid_rows(r, t_valid).astype(F32)
    gk = valid * (-_softplus(-(_dot(zs_ref[...], wgk_ref[...]) + bgk_ref[...])) / GLA_NORMALIZER)
    row_r, col_r = _tri(r)
    same_chunk = (row_r // c) == (col_r // c)
    gsum = jnp.dot(((row_r >= col_r) & same_chunk).astype(F32), gk, precision=HIGHEST,
                   preferred_element_type=F32)
    row, col = _tri(c)
    incl = row >= col
    lz = z[:, 2 * bw:3 * bw]
    rows = [slice(ci * c, (ci + 1) * c) for ci in range(n_chunks)]

    def per(fn):
        return jnp.stack([fn(rows[ci], h) for ci in range(n_chunks) for h in range(heads)])

    q = per(lambda rs, h: z[rs, h * dk:(h + 1) * dk]) * (dk ** -0.5)
    k = per(lambda rs, h: z[rs, bw // 2 + h * dk:bw // 2 + (h + 1) * dk] * valid[rs])
    v = per(lambda rs, h: z[rs, bw + h * dv:bw + (h + 1) * dv])
    g = per(lambda rs, h: gsum[rs, h * dk:(h + 1) * dk])
    q_dec = q * jnp.exp(g)
    attn = jnp.where(incl, _bdot(q_dec, k * jnp.exp(-g), _BNT), 0.0)
    o_intra = _bdot(attn, v)
    g_last = g[:, c - 1:c, :]
    k_dec = k * jnp.exp(g_last - g)
    s_decay = jnp.exp(g_last)

    s_t = s_scr[...]
    for ci in range(n_chunks):
        sel = slice(ci * heads, (ci + 1) * heads)
        o = o_intra[sel] + _bdot(q_dec[sel], s_t, _BNT)
        s_t = s_t * s_decay[sel] + _bdot(v[sel], k_dec[sel], _BTN)
        on = o * lax.rsqrt(jnp.mean(o * o, axis=-1, keepdims=True) + EPS) * nw_ref[...]
        for h in range(heads):
            sl = slice(h * dv, (h + 1) * dv)
            o_ref[rows[ci], sl] = (on[h] * _silu(lz[rows[ci], sl])).astype(BF16)
    s_scr[...] = s_t

    @pl.when(t == nt - 1)
    def _():
        sout_ref[0] = s_t


def _gla(z, zs, wgk_pad, bgk, norm_w, s0_t, nb, tp, t_valid, bw, heads, col_block):
    c = min(GLA_CHUNK, tp)
    r = min(GLA_CHUNK * GLA_CHUNKS_PER_STEP, tp)
    nt = tp // r
    dk = bw // (2 * heads)
    dv = bw // heads
    return pl.pallas_call(
        functools.partial(_gla_kernel, bw=bw, heads=heads, t_valid=t_valid, c=c),
        grid=(nb, nt),
        in_specs=[
            pl.BlockSpec((r, 3 * bw), lambda b, t: (b * nt + t, col_block)),
            pl.BlockSpec((r, LANES), lambda b, t: (b * nt + t, 0)),
            pl.BlockSpec(wgk_pad.shape, lambda b, t: (0, 0)),
            pl.BlockSpec(bgk.shape, lambda b, t: (0, 0)),
            pl.BlockSpec((1, dv), lambda b, t: (0, 0)),
            pl.BlockSpec((1, heads, dv, dk), lambda b, t: (b, 0, 0, 0)),
        ],
        out_specs=[
            pl.BlockSpec((r, bw), lambda b, t: (b * nt + t, 0)),
            pl.BlockSpec((1, heads, dv, dk), lambda b, t: (b, 0, 0, 0)),
        ],
        out_shape=[
            jax.ShapeDtypeStruct((nb * tp, bw), BF16),
            jax.ShapeDtypeStruct((nb, heads, dv, dk), F32),
        ],
        scratch_shapes=[pltpu.VMEM((heads, dv, dk), F32)],
        compiler_params=_params("parallel", "arbitrary"),
        name="branch_gla",
    )(z, zs, wgk_pad, bgk, norm_w, s0_t)


def _rope_kernel(z_ref, cos_ref, sa_ref, sb_ref, *refs, bw, heads, half, scale):
    q_ref, k_ref, v_ref, nk_ref, nv_ref = refs[-5:]
    z = z_ref[...]
    cos = cos_ref[...]
    sa = sa_ref[...]
    sb = sb_ref[...]
    dh = bw // heads

    def rope(x):
        return x * cos + pltpu.roll(x, dh - half, 1) * sa + pltpu.roll(x, half, 1) * sb

    for h in range(heads):
        sl = slice(h * dh, (h + 1) * dh)
        q = rope(z[:, sl])
        k = rope(z[:, bw + h * dh:bw + (h + 1) * dh])
        q_ref[:, sl] = (q * scale).astype(BF16)
        k_ref[:, sl] = k.astype(BF16)
        nk_ref[:, sl] = k
    v = z[:, 2 * bw:3 * bw]
    v_ref[...] = v.astype(BF16)
    nv_ref[...] = v


def _rope(z, tabs, nb, tp, bw, heads, col_block, layer, depth, stacks):
    tt = _tile(tp, 256)
    nt = tp // tt
    dh = bw // heads
    dqk = dh // 2
    kern = functools.partial(_rope_kernel, bw=bw, heads=heads, half=dqk // 8, scale=dqk ** -0.5)
    rows = nb * tp
    spec = pl.BlockSpec((tt, bw), lambda i: (i, 0))
    tspec = pl.BlockSpec((tt, dh), lambda i: (i % nt, 0))
    return pl.pallas_call(
        kern,
        grid=(rows // tt,),
        in_specs=[pl.BlockSpec((tt, 4 * bw), lambda i: (i, col_block)), tspec, tspec, tspec]
        + ([pl.BlockSpec(memory_space=pl.ANY)] * 2 if stacks else []),
        out_specs=[spec] * 3 + [pl.BlockSpec((None, tt, bw), lambda i: (layer, i, 0))] * 2,
        out_shape=[jax.ShapeDtypeStruct((rows, bw), BF16)] * 3
        + [jax.ShapeDtypeStruct((depth, rows, bw), F32)] * 2,
        input_output_aliases={4: 3, 5: 4} if stacks else {},
        compiler_params=_params("parallel"),
        name="diff_rope",
    )(z, *tabs, *(stacks or ()))


def _lambda(lam_ref, lam_init):
    lf = lam_ref[...]
    a = jnp.sum(lf[0:1] * lf[1:2], axis=-1, keepdims=True)
    b = jnp.sum(lf[2:3] * lf[3:4], axis=-1, keepdims=True)
    return jnp.exp(a) - jnp.exp(b) + lam_init


def _stack_components(q, dqk):
    lane = lax.broadcasted_iota(jnp.int32, q.shape, 1)
    zero = jnp.zeros_like(q)
    return jnp.concatenate([jnp.where(lane < dqk, q, zero), jnp.where(lane >= dqk, q, zero)], axis=0)


def _online_update(s, v, m, l, acc, mm=_dot):
    m_new = jnp.maximum(m, jnp.max(s, axis=-1, keepdims=True))
    alpha = jnp.exp(m - m_new)
    p = jnp.exp(s - m_new)
    return m_new, alpha * l + jnp.sum(p, axis=-1, keepdims=True), alpha * acc + mm(p, v)


def _attn_finish(acc, l, n, lam, lam_init, nw, dz):
    o = acc[:n] / l[:n] - lam * (acc[n:] / l[n:])
    on = o * lax.rsqrt(jnp.mean(o * o, axis=-1, keepdims=True) + EPS) * nw * (1.0 - lam_init)
    return (on * _silu(dz)).astype(BF16)


def _flash_kernel(q_ref, k_ref, v_ref, dz_ref, lam_ref, nw_ref, o_ref, *, dqk, lam_init, hb, tk):
    i = pl.program_id(2)
    tq = q_ref.shape[0]
    dh = 2 * dqk
    heads = [slice(h * dh, (h + 1) * dh) for h in range(hb)]
    qs = jnp.stack([_stack_components(q_ref[:, sl], dqk) for sl in heads])

    def block(j):
        start = pl.multiple_of(j * tk, tk)
        kb = k_ref[pl.ds(start, tk), :]
        vb = v_ref[pl.ds(start, tk), :]
        s = _bdot(qs, jnp.stack([kb[:, sl] for sl in heads]), _BNT)
        return s, jnp.stack([vb[:, sl] for sl in heads])

    def body(j, carry):
        s, v = block(j)
        return _online_update(s, v, *carry, mm=_bdot)

    init = (jnp.full((hb, 2 * tq, 1), -jnp.inf, F32), jnp.zeros((hb, 2 * tq, 1), F32),
            jnp.zeros((hb, 2 * tq, dh), F32))
    n_past = (i * tq) // tk
    carry = lax.fori_loop(0, n_past, body, init)
    s, v = block(n_past)
    row = i * tq + lax.broadcasted_iota(jnp.int32, (1,) + s.shape[1:], 1) % tq
    col = n_past * tk + lax.broadcasted_iota(jnp.int32, (1,) + s.shape[1:], 2)
    m, l, acc = _online_update(jnp.where(col <= row, s, -jnp.inf), v, *carry, mm=_bdot)
    lam = _lambda(lam_ref, lam_init)
    for h, sl in enumerate(heads):
        o_ref[:, sl] = _attn_finish(acc[h], l[h], tq, lam, lam_init, nw_ref[...], dz_ref[:, sl])


def _flash(q, k, v, z, lam, norm_w, nb, tp, bw, heads, lam_init, dz_col0):
    dh = bw // heads
    hb = _tile(heads, FLASH_HEADS)
    tq = _tile(tp, FLASH_BLOCK)
    nq = tp // tq
    tk = FLASH_KV_BLOCKS * tq if tp % (FLASH_KV_BLOCKS * tq) == 0 else tq
    kern = functools.partial(_flash_kernel, dqk=dh // 2, lam_init=lam_init, hb=hb, tk=tk)
    return pl.pallas_call(
        kern,
        grid=(nb, heads // hb, nq),
        in_specs=[
            pl.BlockSpec((tq, hb * dh), lambda b, h, i: (b * nq + i, h)),
            pl.BlockSpec((tp, hb * dh), lambda b, h, i: (b, h)),
            pl.BlockSpec((tp, hb * dh), lambda b, h, i: (b, h)),
            pl.BlockSpec((tq, hb * dh), lambda b, h, i: (b * nq + i, dz_col0 // hb + h)),
            pl.BlockSpec(lam.shape, lambda b, h, i: (0, 0)),
            pl.BlockSpec((1, dh), lambda b, h, i: (0, 0)),
        ],
        out_specs=pl.BlockSpec((tq, hb * dh), lambda b, h, i: (b * nq + i, h)),
        out_shape=jax.ShapeDtypeStruct((nb * tp, bw), BF16),
        compiler_params=_params("parallel", "parallel", "arbitrary"),
        name="diff_flash",
    )(q, k, v, z, lam, norm_w)


def _decode_kernel(pt_ref, q_ref, *refs, heads, dqk, t_valid, lam_init, pps):
    kc_refs, vc_refs = refs[:pps], refs[pps:2 * pps]
    kn_ref, vn_ref, dz_ref, lam_ref, nw_ref, o_ref, m_scr, l_scr, acc_scr = refs[2 * pps:]
    j = pl.program_id(1)
    nj = pl.num_programs(1)
    n = q_ref.shape[0]
    dh = 2 * dqk
    rows = 2 * n * heads
    head_lanes = [slice(h * dh, (h + 1) * dh) for h in range(heads)]

    @pl.when(j == 0)
    def _():
        m_scr[...] = jnp.full(m_scr.shape, -jnp.inf, F32)
        l_scr[...] = jnp.zeros(l_scr.shape, F32)
        acc_scr[...] = jnp.zeros(acc_scr.shape, F32)

    q_all = jnp.concatenate([_stack_components(q_ref[:, sl].astype(F32), dqk) for sl in head_lanes], axis=0)
    keys = kc_refs[0].shape[0]
    row_head = lax.broadcasted_iota(jnp.int32, (rows, keys), 0) // (2 * n)
    col_head = lax.broadcasted_iota(jnp.int32, (rows, keys), 1) % heads
    same_head = row_head == col_head
    m, l, acc = m_scr[...], l_scr[...], acc_scr[...]
    scores = [jnp.where(same_head, _dot_nt(q_all, kc_ref[...]), -jnp.inf) for kc_ref in kc_refs]
    m_new = m
    for s in scores:
        m_new = jnp.maximum(m_new, jnp.max(s, axis=-1, keepdims=True))
    alpha = jnp.exp(m - m_new)
    l = alpha * l
    acc = alpha * acc
    for s, vc_ref in zip(scores, vc_refs):
        p = jnp.exp(s - m_new)
        l = l + jnp.sum(p, axis=-1, keepdims=True)
        acc = acc + _dot(p, vc_ref[...])
    state = (m_new, l, acc)
    m_scr[...], l_scr[...], acc_scr[...] = state

    @pl.when(j == nj - 1)
    def _():
        k_new = jnp.concatenate([kn_ref[:, sl].astype(F32) for sl in head_lanes], axis=0)
        v_new = jnp.concatenate([vn_ref[:, sl].astype(F32) for sl in head_lanes], axis=0)
        r2 = lax.broadcasted_iota(jnp.int32, (rows, n * heads), 0)
        c2 = lax.broadcasted_iota(jnp.int32, (rows, n * heads), 1)
        visible = ((r2 // (2 * n)) == (c2 // n)) & ((c2 % n) <= (r2 % n)) & ((c2 % n) < t_valid)
        s = jnp.where(visible, _dot_nt(q_all, k_new), -jnp.inf)
        m, l, acc = _online_update(s, v_new, *state)
        lam = _lambda(lam_ref, lam_init)
        for h, sl in enumerate(head_lanes):
            blk = slice(h * 2 * n, (h + 1) * 2 * n)
            o_ref[:, sl] = _attn_finish(acc[blk], l[blk], n, lam, lam_init, nw_ref[...], dz_ref[:, sl])


def _decode(q, k_new, v_new, z, cache_k, cache_v, page_table, layer, lam, norm_w, nb, t_valid, bw, heads,
            lam_init, dz_col_block):
    n_pages = page_table.shape[1]
    dh = bw // heads
    pps = _tile(n_pages, DECODE_PAGES_PER_STEP)
    kern = functools.partial(_decode_kernel, heads=heads, dqk=dh // 2, t_valid=t_valid, lam_init=lam_init, pps=pps)
    row_spec = pl.BlockSpec((SUBLANES, bw), lambda b, j, pt: (b, 0))
    cache_specs = [
        pl.BlockSpec((None, None, cache_k.shape[2], dh),
                     lambda b, j, pt, i=i: (layer, pt[b * n_pages + j * pps + i], 0, 0))
        for i in range(pps)
    ]
    rows = 2 * SUBLANES * heads
    grid_spec = pltpu.PrefetchScalarGridSpec(
        num_scalar_prefetch=1,
        grid=(nb, n_pages // pps),
        in_specs=[row_spec] + cache_specs + cache_specs + [
            row_spec, row_spec,
            pl.BlockSpec((SUBLANES, bw), lambda b, j, pt: (b, dz_col_block)),
            pl.BlockSpec(lam.shape, lambda b, j, pt: (0, 0)),
            pl.BlockSpec((1, dh), lambda b, j, pt: (0, 0)),
        ],
        out_specs=row_spec,
        scratch_shapes=[pltpu.VMEM((rows, 1), F32), pltpu.VMEM((rows, 1), F32), pltpu.VMEM((rows, dh), F32)],
    )
    return pl.pallas_call(
        kern,
        grid_spec=grid_spec,
        out_shape=jax.ShapeDtypeStruct((nb * SUBLANES, bw), BF16),
        compiler_params=_params("parallel", "arbitrary"),
        name="diff_decode",
    )(page_table.reshape(-1), q, *([cache_k] * pps), *([cache_v] * pps), k_new, v_new, z, lam, norm_w)


def _merge_kernel(h_ref, a_ref, b_ref, c_ref, d_ref, m0_ref, m1_ref, m2_ref, m3_ref, wb_ref, o_ref):
    h = h_ref[...]
    acc = None
    for n, (br, wm) in enumerate(zip((a_ref, b_ref, c_ref, d_ref), (m0_ref, m1_ref, m2_ref, m3_ref))):
        gate = jax.nn.sigmoid(jnp.dot(h, wm[...], preferred_element_type=F32))
        term = gate * jnp.dot(br[...], wb_ref[n], preferred_element_type=F32)
        acc = term if acc is None else acc + term
    o_ref[...] = acc.astype(BF16)


def _merge(h, branches, w_merge, w_branch, layer):
    m, d = h.shape
    bw = branches[0].shape[1]
    tm = _tile(m, 512)
    tn = _tile(d, 256)
    nj = d // tn
    br_spec = pl.BlockSpec((tm, bw), lambda i, j: (i, 0))
    return pl.pallas_call(
        _merge_kernel,
        grid=(m // tm, nj),
        in_specs=[pl.BlockSpec((tm, d), lambda i, j: (i, 0))] + [br_spec] * 4 + [
            pl.BlockSpec((None, d, tn), lambda i, j, n=n: (layer, 0, n * nj + j)) for n in range(4)
        ] + [pl.BlockSpec((None, 4, bw, tn), lambda i, j: (layer, 0, 0, j))],
        out_specs=pl.BlockSpec((tm, tn), lambda i, j: (i, j)),
        out_shape=jax.ShapeDtypeStruct((m, d), BF16),
        compiler_params=_params("parallel", "arbitrary"),
        name="merge",
    )(h, *branches, w_merge, w_merge, w_merge, w_merge, w_branch)


def _outproj_kernel(a_ref, w_ref, x_ref, g_ref, o_ref):
    y = jnp.dot(a_ref[...], w_ref[...], preferred_element_type=F32)
    o_ref[...] = x_ref[...] + g_ref[0] * y


def _outproj(a, w, layer, x, gate, rows_per_mod):
    m, d = a.shape
    r = gate.shape[1]
    tm = r if r > 1 else _tile(rows_per_mod, 1024)
    per = rows_per_mod // tm if r == 1 else 1
    tn = _tile(d, 1024)
    return pl.pallas_call(
        _outproj_kernel,
        grid=(m // tm, d // tn),
        in_specs=[
            pl.BlockSpec((tm, d), lambda i, j: (i, 0)),
            pl.BlockSpec((None, d, tn), lambda i, j: (layer, 0, j)),
            pl.BlockSpec((tm, tn), lambda i, j: (i, j)),
            pl.BlockSpec((1, r, tn), lambda i, j: (i // per, 0, j)),
        ],
        out_specs=pl.BlockSpec((tm, tn), lambda i, j: (i, j)),
        out_shape=jax.ShapeDtypeStruct((m, d), F32),
        compiler_params=_params("parallel", "arbitrary"),
        name="out_proj",
    )(a, w, x, gate)


def _final_norm_kernel(x_ref, w_ref, o_ref):
    x = x_ref[...]
    o_ref[...] = x * lax.rsqrt(jnp.mean(x * x, axis=-1, keepdims=True) + EPS) * w_ref[...]


def _final_norm(x, w):
    m, d = x.shape
    tr = _tile(m, 256)
    return pl.pallas_call(
        _final_norm_kernel,
        grid=(m // tr,),
        in_specs=[pl.BlockSpec((tr, d), lambda i: (i, 0)), pl.BlockSpec((1, d), lambda i: (0, 0))],
        out_specs=pl.BlockSpec((tr, d), lambda i: (i, 0)),
        out_shape=jax.ShapeDtypeStruct((m, d), F32),
        compiler_params=_params("parallel"),
        name="final_norm",
    )(x, w.reshape(1, d))


def _rope_tables(pos, dh, dqk):
    rope_dim = dqk // 4
    half = rope_dim // 2
    inv_freq = ROPE_THETA ** (-jnp.arange(half, dtype=F32) * (2.0 / rope_dim))
    ang = pos.astype(F32)[:, None] * inv_freq[None, :]
    cos, sin = jnp.cos(ang), jnp.sin(ang)
    n = pos.shape[0]
    pad = jnp.zeros((n, dqk - rope_dim), F32)
    comp_cos = jnp.concatenate([cos, cos, pad + 1.0], axis=1)
    comp_sa = jnp.concatenate([-sin, jnp.zeros_like(sin), pad], axis=1)
    comp_sb = jnp.concatenate([jnp.zeros_like(sin), sin, pad], axis=1)
    reps = dh // dqk
    return tuple(jnp.tile(t, (1, reps)) for t in (comp_cos, comp_sa, comp_sb))


def _tail_tile(state, width):
    nb, k, _ = state.shape
    return jnp.concatenate([jnp.zeros((nb, SUBLANES - k, width), F32), state.astype(F32)], axis=1)


def _pad_time(x, tp):
    nb, t = x.shape[:2]
    return jnp.pad(x, [(0, 0), (0, tp - t)] + [(0, 0)] * (x.ndim - 2))


def kernel(x_prompt, x_sample, c_prompt, c_sample, state_conv_a, state_gdn_conv, state_gdn, state_gla, cache_k, cache_v, page_table, w_ada, b_ada, w_in, conv_a_w, gdn_conv_w, gdn_a_log, gdn_dt_bias, gdn_norm_w, diff_lambda, diff_norm_w, gla_w_gk2, gla_b_gk, gla_norm_w, w_branch, w_out, final_norm_w):
    nbp, tpp, d = x_prompt.shape
    nbs, tvs, _ = x_sample.shape
    depth = w_in.shape[0]
    bw = d // 4
    gdn_h = gdn_a_log.shape[1]
    diff_h = cache_k.shape[3]
    dh = cache_k.shape[4]
    dqk = dh // 2
    gla_h = state_gla.shape[2]
    gla_dk = state_gla.shape[3]
    rank = gla_w_gk2.shape[1]
    conv_a_k = conv_a_w.shape[1]
    gdn_k = gdn_conv_w.shape[1]
    past_len = page_table.shape[1] * cache_k.shape[2]
    tps = SUBLANES
    assert tvs <= tps and 2 * gdn_h + rank <= LANES and bw // gdn_h == LANES and dh == LANES

    off_gdn = 4 * bw
    off_side = off_gdn + 4 * bw
    off_diff = off_side + 2 * gdn_h
    off_gla = off_diff + 4 * bw
    off_lr = off_gla + 3 * bw
    off_merge = off_lr + rank
    w_t = jnp.swapaxes(w_in, 1, 2)
    w_ab = _repack(w_t, 0, off_side)
    w_cd = _repack(w_t, off_diff, off_lr - off_diff)
    w_side = _side_weights(w_t, off_side, 2 * gdn_h, off_lr, rank)
    w_merge = _repack(w_t, off_merge, 4 * d)
    w_branch_b = w_branch.astype(BF16)
    w_out_b = w_out.astype(BF16)
    wgk_pad = jnp.concatenate(
        [jnp.zeros((depth, 2 * gdn_h, gla_h * gla_dk), F32), gla_w_gk2,
         jnp.zeros((depth, LANES - 2 * gdn_h - rank, gla_h * gla_dk), F32)], axis=1).astype(BF16)
    lane_pad = jnp.zeros((depth, LANES - gdn_h), F32)
    alog_pad = jnp.concatenate([gdn_a_log, lane_pad], axis=1)
    dtb_pad = jnp.concatenate([gdn_dt_bias, lane_pad], axis=1)

    pool, page = cache_k.shape[1:3]
    cache_k_rows = cache_k.reshape(depth, pool, page * diff_h, dh)
    cache_v_rows = cache_v.reshape(depth, pool, page * diff_h, dh)

    n_c = nbp + nbs
    c_rows = -(-n_c // SUBLANES) * SUBLANES
    c_all = jnp.concatenate([c_prompt, c_sample, jnp.zeros((c_rows - n_c, d), F32)], axis=0)
    mod = _ada(c_all, w_ada, b_ada)

    groups = []
    groups.append(dict(
        nb=nbp, tp=tpp, tv=tpp, x=x_prompt.reshape(nbp * tpp, d), mod_rows=slice(0, nbp), per_row=False,
        conv_a=jnp.zeros((depth, nbp, conv_a_k - 1, bw), F32),
        gdn_conv=jnp.zeros((depth, nbp, gdn_k - 1, 3 * bw), F32),
        gdn_s=jnp.zeros((depth, nbp) + state_gdn.shape[2:], F32),
        gla_s=jnp.zeros((depth, nbp) + state_gla.shape[2:], F32),
        tabs=_rope_tables(jnp.arange(tpp, dtype=jnp.int32), dh, dqk), paged=False))
    groups.append(dict(
        nb=nbs, tp=tps, tv=tvs, x=_pad_time(x_sample, tps).reshape(nbs * tps, d), mod_rows=slice(nbp, n_c),
        per_row=True, conv_a=state_conv_a, gdn_conv=state_gdn_conv, gdn_s=state_gdn, gla_s=state_gla,
        tabs=_rope_tables(past_len + jnp.arange(tps, dtype=jnp.int32), dh, dqk), paged=True))

    results = []
    for g in groups:
        nb, tp, tv = g["nb"], g["tp"], g["tv"]
        x = g["x"]
        outs = [[] for _ in range(4)]
        kv_stacks = None
        for l in range(depth):
            m_l = mod[l, g["mod_rows"]]
            if g["per_row"]:
                m_l = jnp.repeat(m_l, tp, axis=0).reshape(1, nb * tp, 3 * d)
            else:
                m_l = m_l.reshape(nb, 1, 3 * d)
            shift, scale, gate = m_l[..., :d], m_l[..., d:2 * d], m_l[..., 2 * d:]
            h = _prenorm(x, scale, shift, tp)
            z_ab = _matmul(h, w_ab, l)
            z_c = _matmul(h, w_cd, l, 0, 4 * bw)
            z_d = _matmul(h, w_cd, l, 4 * bw, 3 * bw)
            zs = _matmul(h, w_side, l, tn_pref=LANES)

            out_a, tail_a = _conva(z_ab, _tail_tile(g["conv_a"][l], bw), conv_a_w[l], nb, tp, bw)
            out_b, tail_b, s_gdn = _gdn(
                z_ab, zs, _tail_tile(g["gdn_conv"][l], 3 * bw), gdn_conv_w[l], alog_pad[l:l + 1],
                dtb_pad[l:l + 1], gdn_norm_w[l:l + 1], g["gdn_s"][l].astype(F32), nb, tp, tv, bw, gdn_h, 1)
            q_r, k_r, v_r, *kv_stacks = _rope(z_c, g["tabs"], nb, tp, bw, diff_h, 0, l, depth, kv_stacks)
            lam_init = 0.8 - 0.6 * math.exp(-0.3 * l)
            if g["paged"]:
                out_c = _decode(q_r, k_r, v_r, z_c, cache_k_rows, cache_v_rows, page_table, l, diff_lambda[l],
                                diff_norm_w[l:l + 1], nb, tv, bw, diff_h, lam_init, 3)
            else:
                out_c = _flash(q_r, k_r, v_r, z_c, diff_lambda[l], diff_norm_w[l:l + 1], nb, tp, bw, diff_h,
                               lam_init, 3 * diff_h)
            out_d, s_gla_t = _gla(
                z_d, zs, wgk_pad[l], gla_b_gk[l:l + 1], gla_norm_w[l:l + 1],
                jnp.swapaxes(g["gla_s"][l].astype(F32), -1, -2), nb, tp, tv, bw, gla_h, 0)

            merged = _merge(h, (out_a, out_b, out_c, out_d), w_merge, w_branch_b, l)
            x = _outproj(merged, w_out_b, l, x, gate, tp)

            last = tv - (tp - SUBLANES)
            outs[0].append(tail_a[:, last - (conv_a_k - 1):last])
            outs[1].append(tail_b[:, last - (gdn_k - 1):last])
            outs[2].append(s_gdn)
            outs[3].append(jnp.swapaxes(s_gla_t, -1, -2))
        y = _final_norm(x, final_norm_w).reshape(nb, tp, d)[:, :tv]
        new_k, new_v = (s.reshape(depth, nb, tp, diff_h, dh)[:, :, :tv] for s in kv_stacks)
        results.append((y, *[jnp.stack(o) for o in outs], new_k, new_v))

    (yp, *sp), (ys, *ss) = results
    return (yp, ys, *sp, *ss)
```

```python
import functools
import math

import jax
import jax.numpy as jnp
import numpy as np
from jax import lax
from jax.experimental import pallas as pl
from jax.experimental.pallas import tpu as pltpu

F32 = jnp.float32
BF16 = jnp.bfloat16
EPS = 1e-6
GDN_CHUNK = 64
GDN_CHUNKS_PER_STEP = 4
DECODE_PAGES_PER_STEP = 16
FLASH_BLOCK = 256
FLASH_KV_BLOCKS = 2
FLASH_HEADS = 4
GLA_CHUNK = 32
GLA_CHUNKS_PER_STEP = 8
GLA_NORMALIZER = 16.0
ROPE_THETA = 500000.0
SUBLANES = 8
LANES = 128
VMEM_LIMIT = 56 * 1024 * 1024
HIGHEST = lax.Precision.HIGHEST

_NT = (((1,), (1,)), ((), ()))
_TN = (((0,), (0,)), ((), ()))
_BNN = (((2,), (1,)), ((0,), (0,)))
_BNT = (((2,), (2,)), ((0,), (0,)))
_BTN = (((1,), (1,)), ((0,), (0,)))


def _params(*sem):
    return pltpu.CompilerParams(dimension_semantics=sem, vmem_limit_bytes=VMEM_LIMIT)


def _silu(x):
    return x * jax.nn.sigmoid(x)


def _softplus(x):
    return jnp.maximum(x, 0.0) + jnp.log(1.0 + jnp.exp(-jnp.abs(x)))


def _dot(a, b):
    return jnp.dot(a.astype(BF16), b.astype(BF16), preferred_element_type=F32)


def _dot_nt(a, b):
    return lax.dot_general(a.astype(BF16), b.astype(BF16), _NT, preferred_element_type=F32)


def _dot_tn(a, b):
    return lax.dot_general(a.astype(BF16), b.astype(BF16), _TN, preferred_element_type=F32)


def _bdot(a, b, dims=_BNN):
    return lax.dot_general(a.astype(BF16), b.astype(BF16), dims, preferred_element_type=F32)


def _tile(n, pref):
    if n <= pref:
        return n
    t = pref
    while n % t:
        t //= 2
    return t


def _ada_kernel(c_ref, w_ref, b_ref, o_ref):
    c = c_ref[...]
    o_ref[0] = _dot(_silu(c), w_ref[0]) + b_ref[0]


def _ada(c_all, w_ada, b_ada):
    depth, d, n = w_ada.shape
    rows = c_all.shape[0]
    tn = _tile(n, 1024)
    return pl.pallas_call(
        _ada_kernel,
        grid=(depth, n // tn),
        in_specs=[
            pl.BlockSpec((rows, d), lambda l, j: (0, 0)),
            pl.BlockSpec((1, d, tn), lambda l, j: (l, 0, j)),
            pl.BlockSpec((1, 1, tn), lambda l, j: (l, 0, j)),
        ],
        out_specs=pl.BlockSpec((1, rows, tn), lambda l, j: (l, 0, j)),
        out_shape=jax.ShapeDtypeStruct((depth, rows, n), F32),
        compiler_params=_params("parallel", "parallel"),
        name="ada_mod",
    )(c_all, w_ada, b_ada.reshape(depth, 1, n))


def _repack_kernel(a_ref, o_ref):
    o_ref[...] = a_ref[0].T.astype(BF16)


def _repack(w_t, start, width):
    depth, _, k = w_t.shape
    tn = _tile(width, 1024)
    tk = _tile(k, 1024)
    return pl.pallas_call(
        _repack_kernel,
        grid=(depth, k // tk, width // tn),
        in_specs=[pl.BlockSpec((pl.Element(1), pl.Element(tn), pl.Element(tk)),
                               lambda l, i, j: (l, pl.multiple_of(start + j * tn, SUBLANES),
                                                pl.multiple_of(i * tk, LANES)))],
        out_specs=pl.BlockSpec((None, tk, tn), lambda l, i, j: (l, i, j)),
        out_shape=jax.ShapeDtypeStruct((depth, k, width), BF16),
        compiler_params=_params("parallel", "parallel", "parallel"),
        name="repack",
    )(w_t)


def _side_kernel(a_ref, b_ref, o_ref, *, n_a, n_b):
    lane = lax.broadcasted_iota(jnp.int32, o_ref.shape, 1)
    o_ref[...] = jnp.where(lane < n_a, a_ref[0].T, jnp.where(lane < n_a + n_b, b_ref[0].T, 0.0)).astype(BF16)


def _side_weights(w_t, start_a, n_a, start_b, n_b):
    depth, _, k = w_t.shape
    assert n_a + n_b <= LANES
    tk = _tile(k, 512)
    return pl.pallas_call(
        functools.partial(_side_kernel, n_a=n_a, n_b=n_b),
        grid=(depth, k // tk),
        in_specs=[
            pl.BlockSpec((pl.Element(1), pl.Element(LANES), pl.Element(tk)), lambda l, i: (l, start_a, i * tk)),
            pl.BlockSpec((pl.Element(1), pl.Element(LANES), pl.Element(tk)),
                         lambda l, i: (l, start_b - n_a, i * tk)),
        ],
        out_specs=pl.BlockSpec((None, tk, LANES), lambda l, i: (l, i, 0)),
        out_shape=jax.ShapeDtypeStruct((depth, k, LANES), BF16),
        compiler_params=_params("parallel", "parallel"),
        name="repack_side",
    )(w_t, w_t)


def _prenorm_kernel(x_ref, sc_ref, sh_ref, h_ref):
    x = x_ref[...]
    y = x * lax.rsqrt(jnp.mean(x * x, axis=-1, keepdims=True) + EPS)
    h_ref[...] = (y * (1.0 + sc_ref[0]) + sh_ref[0]).astype(BF16)


def _prenorm(x, scale, shift, rows_per_mod):
    m, d = x.shape
    r = scale.shape[1]
    tr = r if r > 1 else _tile(rows_per_mod, 256)
    per = rows_per_mod // tr if r == 1 else 1
    return pl.pallas_call(
        _prenorm_kernel,
        grid=(m // tr,),
        in_specs=[
            pl.BlockSpec((tr, d), lambda i: (i, 0)),
            pl.BlockSpec((1, r, d), lambda i: (i // per, 0, 0)),
            pl.BlockSpec((1, r, d), lambda i: (i // per, 0, 0)),
        ],
        out_specs=pl.BlockSpec((tr, d), lambda i: (i, 0)),
        out_shape=jax.ShapeDtypeStruct((m, d), BF16),
        compiler_params=_params("parallel"),
        name="prenorm",
    )(x, scale, shift)


def _mm_kernel(a_ref, w_ref, o_ref):
    o_ref[...] = jnp.dot(a_ref[...], w_ref[...], preferred_element_type=F32).astype(o_ref.dtype)


def _matmul(a, w, layer, col0=0, n=None, out_dtype=F32, tm_pref=1024, tn_pref=1024):
    m, k = a.shape
    n = w.shape[2] if n is None else n
    tm = _tile(m, tm_pref)
    tn = _tile(math.gcd(n, col0) if col0 else n, tn_pref)
    return pl.pallas_call(
        _mm_kernel,
        grid=(m // tm, n // tn),
        in_specs=[
            pl.BlockSpec((tm, k), lambda i, j: (i, 0)),
            pl.BlockSpec((None, k, tn), lambda i, j: (layer, 0, col0 // tn + j)),
        ],
        out_specs=pl.BlockSpec((tm, tn), lambda i, j: (i, j)),
        out_shape=jax.ShapeDtypeStruct((m, n), out_dtype),
        compiler_params=_params("parallel", "arbitrary"),
        name="in_proj",
    )(a, w)


def _mm_t_kernel(a_ref, w_ref, o_ref):
    o_ref[...] = lax.dot_general(a_ref[...], w_ref[0].astype(BF16), _NT, preferred_element_type=F32)


def _matmul_t(a, w_t, layer, row0, n, tm_pref=1024, tn_pref=512):
    m, k = a.shape
    tm = _tile(m, tm_pref)
    tn = _tile(n, tn_pref)
    return pl.pallas_call(
        _mm_t_kernel,
        grid=(m // tm, n // tn),
        in_specs=[
            pl.BlockSpec((tm, k), lambda i, j: (i, 0)),
            pl.BlockSpec((pl.Element(1), pl.Element(tn), pl.Element(k)),
                         lambda i, j: (layer, pl.multiple_of(row0 + j * tn, SUBLANES), 0)),
        ],
        out_specs=pl.BlockSpec((tm, tn), lambda i, j: (i, j)),
        out_shape=jax.ShapeDtypeStruct((m, n), F32),
        compiler_params=_params("parallel", "arbitrary"),
        name="in_proj_t",
    )(a, w_t)


def _causal_conv(carry_ref, st_ref, w_ref, u):
    n = u.shape[0]

    @pl.when(pl.program_id(1) == 0)
    def _():
        carry_ref[...] = st_ref[0]

    ext = jnp.concatenate([carry_ref[...], u], axis=0)
    w = w_ref[...]
    taps = w.shape[0]
    y = w[taps - 1:taps] * u
    for k in range(1, taps):
        y = y + w[taps - 1 - k:taps - k] * pltpu.roll(ext, k, 0)[SUBLANES:SUBLANES + n]
    tail = u[n - SUBLANES:n]
    carry_ref[...] = tail
    return y, tail


def _valid_rows(c, t_valid):
    t = pl.program_id(1)
    row = lax.broadcasted_iota(jnp.int32, (c, 1), 0) + t * c
    return row < t_valid


def _tri(c):
    row = lax.broadcasted_iota(jnp.int32, (c, c), 0)
    col = lax.broadcasted_iota(jnp.int32, (c, c), 1)
    return row, col


def _conva_kernel(z_ref, st_ref, w_ref, o_ref, tail_ref, carry, *, bw):
    z = z_ref[...]
    a_h, a_b, a_c, a_z = (z[:, i * bw:(i + 1) * bw] for i in range(4))
    y, tail = _causal_conv(carry, st_ref, w_ref, a_c * a_h)
    o_ref[...] = (a_b * y * _silu(a_z)).astype(BF16)
    tail_ref[0] = tail


def _conva(z, state_tile, w, nb, tp, bw):
    tt = _tile(tp, 256)
    nt = tp // tt
    return pl.pallas_call(
        functools.partial(_conva_kernel, bw=bw),
        grid=(nb, nt),
        in_specs=[
            pl.BlockSpec((tt, 4 * bw), lambda b, t: (b * nt + t, 0)),
            pl.BlockSpec((1, SUBLANES, bw), lambda b, t: (b, 0, 0)),
            pl.BlockSpec(w.shape, lambda b, t: (0, 0)),
        ],
        out_specs=[
            pl.BlockSpec((tt, bw), lambda b, t: (b * nt + t, 0)),
            pl.BlockSpec((1, SUBLANES, bw), lambda b, t: (b, 0, 0)),
        ],
        out_shape=[
            jax.ShapeDtypeStruct((nb * tp, bw), BF16),
            jax.ShapeDtypeStruct((nb, SUBLANES, bw), F32),
        ],
        scratch_shapes=[pltpu.VMEM((SUBLANES, bw), F32)],
        compiler_params=_params("parallel", "arbitrary"),
        name="branch_conv",
    )(z, state_tile, w)


INVERSE_BASE = 8


def _inverse_masks(row, col, c):
    base = min(INVERSE_BASE, c)
    masks = [(row // base) == (col // base)]
    b = base
    while b < c:
        masks.append(((row // (2 * b)) == (col // (2 * b))) & ((row // b) != (col // b)))
        b *= 2
    return masks


def _unit_lower_inverse(lmat, eye, masks, mm):
    base = min(INVERSE_BASE, lmat.shape[-1])
    p = jnp.where(masks[0], lmat, 0.0)
    x = eye - p
    for _ in range(max(0, int(math.log2(base)) - 1)):
        p = mm(p, p)
        x = x + mm(x, p)
    for m in masks[1:]:
        x = x - mm(mm(x, jnp.where(m, lmat, 0.0)), x)
    return x


def _gdn_kernel(z_ref, zs_ref, cst_ref, cw_ref, alog_ref, dtb_ref, nw_ref, s0_ref,
                o_ref, tail_ref, sout_ref, carry, s_scr, *, bw, heads, t_valid, c):
    t = pl.program_id(1)
    nt = pl.num_programs(1)

    @pl.when(t == 0)
    def _():
        s_scr[...] = s0_ref[0]

    z = z_ref[...]
    r = z.shape[0]
    n_chunks = r // c
    dk = bw // heads
    gz = z[:, 3 * bw:4 * bw]
    y, tail = _causal_conv(carry, cst_ref, cw_ref, z[:, :3 * bw])
    tail_ref[0] = tail
    qkv = _silu(y)

    valid = _valid_rows(r, t_valid).astype(F32)
    zs = zs_ref[...]
    g = valid * (-jnp.exp(alog_ref[...]) * _softplus(zs + dtb_ref[...]))
    beta = valid * jax.nn.sigmoid(zs)
    row_r, col_r = _tri(r)
    same_chunk = (row_r // c) == (col_r // c)
    gsum = jnp.dot(((row_r >= col_r) & same_chunk).astype(F32), g, precision=HIGHEST,
                   preferred_element_type=F32)
    row, col = _tri(c)
    incl = row >= col
    strict = row > col
    eye = (row == col).astype(F32)
    inv_masks = _inverse_masks(row, col, c)
    r128, c128 = _tri(LANES)
    eye128 = (r128 == c128).astype(F32)
    rows = [slice(ci * c, (ci + 1) * c) for ci in range(n_chunks)]
    gsum_t = [lax.dot_general(eye128, gsum[rs], _NT, precision=HIGHEST, preferred_element_type=F32)
              for rs in rows]

    def per(fn):
        return jnp.stack([fn(ci, rows[ci], h) for ci in range(n_chunks) for h in range(heads)])

    q = per(lambda ci, rs, h: qkv[rs, h * dk:(h + 1) * dk])
    k = per(lambda ci, rs, h: qkv[rs, bw + h * dk:bw + (h + 1) * dk])
    v = per(lambda ci, rs, h: qkv[rs, 2 * bw + h * dk:2 * bw + (h + 1) * dk])
    gcol = per(lambda ci, rs, h: jnp.broadcast_to(gsum[rs, h:h + 1], (c, dk)))
    bcol = per(lambda ci, rs, h: jnp.broadcast_to(beta[rs, heads + h:heads + h + 1], (c, dk)))
    vcol = per(lambda ci, rs, h: jnp.broadcast_to(valid[rs], (c, dk)))
    grow = per(lambda ci, rs, h: gsum_t[ci][h:h + 1, :])

    qn = q * lax.rsqrt(jnp.sum(q * q, axis=-1, keepdims=True) + EPS) * (dk ** -0.5)
    kn = vcol * (k * lax.rsqrt(jnp.sum(k * k, axis=-1, keepdims=True) + EPS))
    eg = jnp.exp(gcol)
    decay = jnp.exp(jnp.where(incl, gcol[:, :, :c] - grow, 0.0))
    kb = kn * bcol
    kq = _bdot(jnp.concatenate([kb, qn], axis=1), kn, _BNT)
    lmat = jnp.where(strict, kq[:, :c] * decay, 0.0)
    qk = jnp.where(incl, kq[:, c:] * decay, 0.0)
    tinv = _unit_lower_inverse(lmat, eye, inv_masks, _bdot)
    sol = _bdot(tinv, jnp.concatenate([v * bcol, kb * eg], axis=2))
    u = sol[:, :, :dk]
    wq = jnp.concatenate([sol[:, :, dk:], qn * eg], axis=1)
    g_last = gcol[:, c - 1:c, :]
    k_dec = kn * jnp.exp(g_last - gcol)
    s_decay = jnp.exp(g_last)

    s = s_scr[...]
    for ci in range(n_chunks):
        sel = slice(ci * heads, (ci + 1) * heads)
        ws = _bdot(wq[sel], s)
        v_new = u[sel] - ws[:, :c]
        o = ws[:, c:] + _bdot(qk[sel], v_new)
        s = s * s_decay[sel] + _bdot(k_dec[sel], v_new, _BTN)
        on = o * lax.rsqrt(jnp.mean(o * o, axis=-1, keepdims=True) + EPS) * nw_ref[...]
        for h in range(heads):
            sl = slice(h * dk, (h + 1) * dk)
            o_ref[rows[ci], sl] = (on[h] * _silu(gz[rows[ci], sl])).astype(BF16)
    s_scr[...] = s

    @pl.when(t == nt - 1)
    def _():
        sout_ref[0] = s


def _gdn(z, zs, conv_tile, conv_w, alog, dtb, norm_w, s0, nb, tp, t_valid, bw, heads, col_block):
    c = min(GDN_CHUNK, tp)
    r = min(GDN_CHUNK * GDN_CHUNKS_PER_STEP, tp)
    nt = tp // r
    dk = bw // heads
    return pl.pallas_call(
        functools.partial(_gdn_kernel, bw=bw, heads=heads, t_valid=t_valid, c=c),
        grid=(nb, nt),
        in_specs=[
            pl.BlockSpec((r, 4 * bw), lambda b, t: (b * nt + t, col_block)),
            pl.BlockSpec((r, LANES), lambda b, t: (b * nt + t, 0)),
            pl.BlockSpec((1, SUBLANES, 3 * bw), lambda b, t: (b, 0, 0)),
            pl.BlockSpec(conv_w.shape, lambda b, t: (0, 0)),
            pl.BlockSpec((1, LANES), lambda b, t: (0, 0)),
            pl.BlockSpec((1, LANES), lambda b, t: (0, 0)),
            pl.BlockSpec((1, dk), lambda b, t: (0, 0)),
            pl.BlockSpec((1, heads, dk, dk), lambda b, t: (b, 0, 0, 0)),
        ],
        out_specs=[
            pl.BlockSpec((r, bw), lambda b, t: (b * nt + t, 0)),
            pl.BlockSpec((1, SUBLANES, 3 * bw), lambda b, t: (b, 0, 0)),
            pl.BlockSpec((1, heads, dk, dk), lambda b, t: (b, 0, 0, 0)),
        ],
        out_shape=[
            jax.ShapeDtypeStruct((nb * tp, bw), BF16),
            jax.ShapeDtypeStruct((nb, SUBLANES, 3 * bw), F32),
            jax.ShapeDtypeStruct((nb, heads, dk, dk), F32),
        ],
        scratch_shapes=[pltpu.VMEM((SUBLANES, 3 * bw), F32), pltpu.VMEM((heads, dk, dk), F32)],
        compiler_params=_params("parallel", "arbitrary"),
        name="branch_gdn",
    )(z, zs, conv_tile, conv_w, alog, dtb, norm_w, s0)


def _gla_kernel(z_ref, zs_ref, wgk_ref, bgk_ref, nw_ref, s0_ref, o_ref, sout_ref, s_scr,
                *, bw, heads, t_valid, c):
    t = pl.program_id(1)
    nt = pl.num_programs(1)

    @pl.when(t == 0)
    def _():
        s_scr[...] = s0_ref[0]

    z = z_ref[...]
    r = z.shape[0]
    n_chunks = r // c
    dk = bw // (2 * heads)
    dv = bw // heads
    valid = _valid_rows(r, t_valid).astype(F32)
    gk = valid * (-_softplus(-(_dot(zs_ref[...], wgk_ref[...]) + bgk_ref[...])) / GLA_NORMALIZER)
    row_r, col_r = _tri(r)
    same_chunk = (row_r // c) == (col_r // c)
    gsum = jnp.dot(((row_r >= col_r) & same_chunk).astype(F32), gk, precision=HIGHEST,
                   preferred_element_type=F32)
    row, col = _tri(c)
    incl = row >= col
    lz = z[:, 2 * bw:3 * bw]
    rows = [slice(ci * c, (ci + 1) * c) for ci in range(n_chunks)]

    def per(fn):
        return jnp.stack([fn(rows[ci], h) for ci in range(n_chunks) for h in range(heads)])

    q = per(lambda rs, h: z[rs, h * dk:(h + 1) * dk]) * (dk ** -0.5)
    k = per(lambda rs, h: z[rs, bw // 2 + h * dk:bw // 2 + (h + 1) * dk] * valid[rs])
    v = per(lambda rs, h: z[rs, bw + h * dv:bw + (h + 1) * dv])
    g = per(lambda rs, h: gsum[rs, h * dk:(h + 1) * dk])
    q_dec = q * jnp.exp(g)
    attn = jnp.where(incl, _bdot(q_dec, k * jnp.exp(-g), _BNT), 0.0)
    o_intra = _bdot(attn, v)
    g_last = g[:, c - 1:c, :]
    k_dec = k * jnp.exp(g_last - g)
    s_decay = jnp.exp(g_last)

    s_t = s_scr[...]
    for ci in range(n_chunks):
        sel = slice(ci * heads, (ci + 1) * heads)
        o = o_intra[sel] + _bdot(q_dec[sel], s_t, _BNT)
        s_t = s_t * s_decay[sel] + _bdot(v[sel], k_dec[sel], _BTN)
        on = o * lax.rsqrt(jnp.mean(o * o, axis=-1, keepdims=True) + EPS) * nw_ref[...]
        for h in range(heads):
            sl = slice(h * dv, (h + 1) * dv)
            o_ref[rows[ci], sl] = (on[h] * _silu(lz[rows[ci], sl])).astype(BF16)
    s_scr[...] = s_t

    @pl.when(t == nt - 1)
    def _():
        sout_ref[0] = s_t


def _gla(z, zs, wgk_pad, bgk, norm_w, s0_t, nb, tp, t_valid, bw, heads, col_block):
    c = min(GLA_CHUNK, tp)
    r = min(GLA_CHUNK * GLA_CHUNKS_PER_STEP, tp)
    nt = tp // r
    dk = bw // (2 * heads)
    dv = bw // heads
    return pl.pallas_call(
        functools.partial(_gla_kernel, bw=bw, heads=heads, t_valid=t_valid, c=c),
        grid=(nb, nt),
        in_specs=[
            pl.BlockSpec((r, 3 * bw), lambda b, t: (b * nt + t, col_block)),
            pl.BlockSpec((r, LANES), lambda b, t: (b * nt + t, 0)),
            pl.BlockSpec(wgk_pad.shape, lambda b, t: (0, 0)),
            pl.BlockSpec(bgk.shape, lambda b, t: (0, 0)),
            pl.BlockSpec((1, dv), lambda b, t: (0, 0)),
            pl.BlockSpec((1, heads, dv, dk), lambda b, t: (b, 0, 0, 0)),
        ],
        out_specs=[
            pl.BlockSpec((r, bw), lambda b, t: (b * nt + t, 0)),
            pl.BlockSpec((1, heads, dv, dk), lambda b, t: (b, 0, 0, 0)),
        ],
        out_shape=[
            jax.ShapeDtypeStruct((nb * tp, bw), BF16),
            jax.ShapeDtypeStruct((nb, heads, dv, dk), F32),
        ],
        scratch_shapes=[pltpu.VMEM((heads, dv, dk), F32)],
        compiler_params=_params("parallel", "arbitrary"),
        name="branch_gla",
    )(z, zs, wgk_pad, bgk, norm_w, s0_t)


def _rope_kernel(z_ref, cos_ref, sa_ref, sb_ref, *refs, bw, heads, half, scale):
    q_ref, k_ref, v_ref, nk_ref, nv_ref = refs[-5:]
    z = z_ref[...]
    cos = cos_ref[...]
    sa = sa_ref[...]
    sb = sb_ref[...]
    dh = bw // heads

    def rope(x):
        return x * cos + pltpu.roll(x, dh - half, 1) * sa + pltpu.roll(x, half, 1) * sb

    for h in range(heads):
        sl = slice(h * dh, (h + 1) * dh)
        q = rope(z[:, sl])
        k = rope(z[:, bw + h * dh:bw + (h + 1) * dh])
        q_ref[:, sl] = (q * scale).astype(BF16)
        k_ref[:, sl] = k.astype(BF16)
        nk_ref[:, sl] = k
    v = z[:, 2 * bw:3 * bw]
    v_ref[...] = v.astype(BF16)
    nv_ref[...] = v


def _rope(z, tabs, nb, tp, bw, heads, col_block, layer, depth, stacks):
    tt = _tile(tp, 256)
    nt = tp // tt
    dh = bw // heads
    dqk = dh // 2
    kern = functools.partial(_rope_kernel, bw=bw, heads=heads, half=dqk // 8, scale=dqk ** -0.5)
    rows = nb * tp
    spec = pl.BlockSpec((tt, bw), lambda i: (i, 0))
    tspec = pl.BlockSpec((tt, dh), lambda i: (i % nt, 0))
    return pl.pallas_call(
        kern,
        grid=(rows // tt,),
        in_specs=[pl.BlockSpec((tt, 4 * bw), lambda i: (i, col_block)), tspec, tspec, tspec]
        + ([pl.BlockSpec(memory_space=pl.ANY)] * 2 if stacks else []),
        out_specs=[spec] * 3 + [pl.BlockSpec((None, tt, bw), lambda i: (layer, i, 0))] * 2,
        out_shape=[jax.ShapeDtypeStruct((rows, bw), BF16)] * 3
        + [jax.ShapeDtypeStruct((depth, rows, bw), F32)] * 2,
        input_output_aliases={4: 3, 5: 4} if stacks else {},
        compiler_params=_params("parallel"),
        name="diff_rope",
    )(z, *tabs, *(stacks or ()))


def _lambda(lam_ref, lam_init):
    lf = lam_ref[...]
    a = jnp.sum(lf[0:1] * lf[1:2], axis=-1, keepdims=True)
    b = jnp.sum(lf[2:3] * lf[3:4], axis=-1, keepdims=True)
    return jnp.exp(a) - jnp.exp(b) + lam_init


def _stack_components(q, dqk):
    lane = lax.broadcasted_iota(jnp.int32, q.shape, 1)
    zero = jnp.zeros_like(q)
    return jnp.concatenate([jnp.where(lane < dqk, q, zero), jnp.where(lane >= dqk, q, zero)], axis=0)


def _online_update(s, v, m, l, acc, mm=_dot):
    m_new = jnp.maximum(m, jnp.max(s, axis=-1, keepdims=True))
    alpha = jnp.exp(m - m_new)
    p = jnp.exp(s - m_new)
    return m_new, alpha * l + jnp.sum(p, axis=-1, keepdims=True), alpha * acc + mm(p, v)


def _attn_finish(acc, l, n, lam, lam_init, nw, dz):
    o = acc[:n] / l[:n] - lam * (acc[n:] / l[n:])
    on = o * lax.rsqrt(jnp.mean(o * o, axis=-1, keepdims=True) + EPS) * nw * (1.0 - lam_init)
    return (on * _silu(dz)).astype(BF16)


def _flash_kernel(q_ref, k_ref, v_ref, dz_ref, lam_ref, nw_ref, o_ref, *, dqk, lam_init, hb, tk):
    i = pl.program_id(2)
    tq = q_ref.shape[0]
    dh = 2 * dqk
    heads = [slice(h * dh, (h + 1) * dh) for h in range(hb)]
    qs = jnp.stack([_stack_components(q_ref[:, sl], dqk) for sl in heads])

    def block(j):
        start = pl.multiple_of(j * tk, tk)
        kb = k_ref[pl.ds(start, tk), :]
        vb = v_ref[pl.ds(start, tk), :]
        s = _bdot(qs, jnp.stack([kb[:, sl] for sl in heads]), _BNT)
        return s, jnp.stack([vb[:, sl] for sl in heads])

    def body(j, carry):
        s, v = block(j)
        return _online_update(s, v, *carry, mm=_bdot)

    init = (jnp.full((hb, 2 * tq, 1), -jnp.inf, F32), jnp.zeros((hb, 2 * tq, 1), F32),
            jnp.zeros((hb, 2 * tq, dh), F32))
    n_past = (i * tq) // tk
    carry = lax.fori_loop(0, n_past, body, init)
    s, v = block(n_past)
    row = i * tq + lax.broadcasted_iota(jnp.int32, (1,) + s.shape[1:], 1) % tq
    col = n_past * tk + lax.broadcasted_iota(jnp.int32, (1,) + s.shape[1:], 2)
    m, l, acc = _online_update(jnp.where(col <= row, s, -jnp.inf), v, *carry, mm=_bdot)
    lam = _lambda(lam_ref, lam_init)
    for h, sl in enumerate(heads):
        o_ref[:, sl] = _attn_finish(acc[h], l[h], tq, lam, lam_init, nw_ref[...], dz_ref[:, sl])


def _flash(q, k, v, z, lam, norm_w, nb, tp, bw, heads, lam_init, dz_col0):
    dh = bw // heads
    hb = _tile(heads, FLASH_HEADS)
    tq = _tile(tp, FLASH_BLOCK)
    nq = tp // tq
    tk = FLASH_KV_BLOCKS * tq if tp % (FLASH_KV_BLOCKS * tq) == 0 else tq
    kern = functools.partial(_flash_kernel, dqk=dh // 2, lam_init=lam_init, hb=hb, tk=tk)
    return pl.pallas_call(
        kern,
        grid=(nb, heads // hb, nq),
        in_specs=[
            pl.BlockSpec((tq, hb * dh), lambda b, h, i: (b * nq + i, h)),
            pl.BlockSpec((tp, hb * dh), lambda b, h, i: (b, h)),
            pl.BlockSpec((tp, hb * dh), lambda b, h, i: (b, h)),
            pl.BlockSpec((tq, hb * dh), lambda b, h, i: (b * nq + i, dz_col0 // hb + h)),
            pl.BlockSpec(lam.shape, lambda b, h, i: (0, 0)),
            pl.BlockSpec((1, dh), lambda b, h, i: (0, 0)),
        ],
        out_specs=pl.BlockSpec((tq, hb * dh), lambda b, h, i: (b * nq + i, h)),
        out_shape=jax.ShapeDtypeStruct((nb * tp, bw), BF16),
        compiler_params=_params("parallel", "parallel", "arbitrary"),
        name="diff_flash",
    )(q, k, v, z, lam, norm_w)


def _decode_kernel(pt_ref, q_ref, *refs, heads, dqk, t_valid, lam_init, pps):
    kc_refs, vc_refs = refs[:pps], refs[pps:2 * pps]
    kn_ref, vn_ref, dz_ref, lam_ref, nw_ref, o_ref, m_scr, l_scr, acc_scr = refs[2 * pps:]
    j = pl.program_id(1)
    nj = pl.num_programs(1)
    n = q_ref.shape[0]
    dh = 2 * dqk
    rows = 2 * n * heads
    head_lanes = [slice(h * dh, (h + 1) * dh) for h in range(heads)]

    @pl.when(j == 0)
    def _():
        m_scr[...] = jnp.full(m_scr.shape, -jnp.inf, F32)
        l_scr[...] = jnp.zeros(l_scr.shape, F32)
        acc_scr[...] = jnp.zeros(acc_scr.shape, F32)

    q_all = jnp.concatenate([_stack_components(q_ref[:, sl].astype(F32), dqk) for sl in head_lanes], axis=0)
    keys = kc_refs[0].shape[0]
    row_head = lax.broadcasted_iota(jnp.int32, (rows, keys), 0) // (2 * n)
    col_head = lax.broadcasted_iota(jnp.int32, (rows, keys), 1) % heads
    same_head = row_head == col_head
    m, l, acc = m_scr[...], l_scr[...], acc_scr[...]
    scores = [jnp.where(same_head, _dot_nt(q_all, kc_ref[...]), -jnp.inf) for kc_ref in kc_refs]
    m_new = m
    for s in scores:
        m_new = jnp.maximum(m_new, jnp.max(s, axis=-1, keepdims=True))
    alpha = jnp.exp(m - m_new)
    l = alpha * l
    acc = alpha * acc
    for s, vc_ref in zip(scores, vc_refs):
        p = jnp.exp(s - m_new)
        l = l + jnp.sum(p, axis=-1, keepdims=True)
        acc = acc + _dot(p, vc_ref[...])
    state = (m_new, l, acc)
    m_scr[...], l_scr[...], acc_scr[...] = state

    @pl.when(j == nj - 1)
    def _():
        k_new = jnp.concatenate([kn_ref[:, sl].astype(F32) for sl in head_lanes], axis=0)
        v_new = jnp.concatenate([vn_ref[:, sl].astype(F32) for sl in head_lanes], axis=0)
        r2 = lax.broadcasted_iota(jnp.int32, (rows, n * heads), 0)
        c2 = lax.broadcasted_iota(jnp.int32, (rows, n * heads), 1)
        visible = ((r2 // (2 * n)) == (c2 // n)) & ((c2 % n) <= (r2 % n)) & ((c2 % n) < t_valid)
        s = jnp.where(visible, _dot_nt(q_all, k_new), -jnp.inf)
        m, l, acc = _online_update(s, v_new, *state)
        lam = _lambda(lam_ref, lam_init)
        for h, sl in enumerate(head_lanes):
            blk = slice(h * 2 * n, (h + 1) * 2 * n)
            o_ref[:, sl] = _attn_finish(acc[blk], l[blk], n, lam, lam_init, nw_ref[...], dz_ref[:, sl])


def _decode(q, k_new, v_new, z, cache_k, cache_v, page_table, layer, lam, norm_w, nb, t_valid, bw, heads,
            lam_init, dz_col_block):
    n_pages = page_table.shape[1]
    dh = bw // heads
    pps = _tile(n_pages, DECODE_PAGES_PER_STEP)
    kern = functools.partial(_decode_kernel, heads=heads, dqk=dh // 2, t_valid=t_valid, lam_init=lam_init, pps=pps)
    row_spec = pl.BlockSpec((SUBLANES, bw), lambda b, j, pt: (b, 0))
    cache_specs = [
        pl.BlockSpec((None, None, cache_k.shape[2], dh),
                     lambda b, j, pt, i=i: (layer, pt[b * n_pages + j * pps + i], 0, 0))
        for i in range(pps)
    ]
    rows = 2 * SUBLANES * heads
    grid_spec = pltpu.PrefetchScalarGridSpec(
        num_scalar_prefetch=1,
        grid=(nb, n_pages // pps),
        in_specs=[row_spec] + cache_specs + cache_specs + [
            row_spec, row_spec,
            pl.BlockSpec((SUBLANES, bw), lambda b, j, pt: (b, dz_col_block)),
            pl.BlockSpec(lam.shape, lambda b, j, pt: (0, 0)),
            pl.BlockSpec((1, dh), lambda b, j, pt: (0, 0)),
        ],
        out_specs=row_spec,
        scratch_shapes=[pltpu.VMEM((rows, 1), F32), pltpu.VMEM((rows, 1), F32), pltpu.VMEM((rows, dh), F32)],
    )
    return pl.pallas_call(
        kern,
        grid_spec=grid_spec,
        out_shape=jax.ShapeDtypeStruct((nb * SUBLANES, bw), BF16),
        compiler_params=_params("parallel", "arbitrary"),
        name="diff_decode",
    )(page_table.reshape(-1), q, *([cache_k] * pps), *([cache_v] * pps), k_new, v_new, z, lam, norm_w)


def _merge_kernel(h_ref, a_ref, b_ref, c_ref, d_ref, m0_ref, m1_ref, m2_ref, m3_ref, wb_ref, o_ref):
    h = h_ref[...]
    acc = None
    for n, (br, wm) in enumerate(zip((a_ref, b_ref, c_ref, d_ref), (m0_ref, m1_ref, m2_ref, m3_ref))):
        gate = jax.nn.sigmoid(jnp.dot(h, wm[...], preferred_element_type=F32))
        term = gate * jnp.dot(br[...], wb_ref[n], preferred_element_type=F32)
        acc = term if acc is None else acc + term
    o_ref[...] = acc.astype(BF16)


def _merge(h, branches, w_merge, w_branch, layer):
    m, d = h.shape
    bw = branches[0].shape[1]
    tm = _tile(m, 512)
    tn = _tile(d, 256)
    nj = d // tn
    br_spec = pl.BlockSpec((tm, bw), lambda i, j: (i, 0))
    return pl.pallas_call(
        _merge_kernel,
        grid=(m // tm, nj),
        in_specs=[pl.BlockSpec((tm, d), lambda i, j: (i, 0))] + [br_spec] * 4 + [
            pl.BlockSpec((None, d, tn), lambda i, j, n=n: (layer, 0, n * nj + j)) for n in range(4)
        ] + [pl.BlockSpec((None, 4, bw, tn), lambda i, j: (layer, 0, 0, j))],
        out_specs=pl.BlockSpec((tm, tn), lambda i, j: (i, j)),
        out_shape=jax.ShapeDtypeStruct((m, d), BF16),
        compiler_params=_params("parallel", "arbitrary"),
        name="merge",
    )(h, *branches, w_merge, w_merge, w_merge, w_merge, w_branch)


def _outproj_kernel(a_ref, w_ref, x_ref, g_ref, o_ref):
    y = jnp.dot(a_ref[...], w_ref[...], preferred_element_type=F32)
    o_ref[...] = x_ref[...] + g_ref[0] * y


def _outproj(a, w, layer, x, gate, rows_per_mod):
    m, d = a.shape
    r = gate.shape[1]
    tm = r if r > 1 else _tile(rows_per_mod, 1024)
    per = rows_per_mod // tm if r == 1 else 1
    tn = _tile(d, 1024)
    return pl.pallas_call(
        _outproj_kernel,
        grid=(m // tm, d // tn),
        in_specs=[
            pl.BlockSpec((tm, d), lambda i, j: (i, 0)),
            pl.BlockSpec((None, d, tn), lambda i, j: (layer, 0, j)),
            pl.BlockSpec((tm, tn), lambda i, j: (i, j)),
            pl.BlockSpec((1, r, tn), lambda i, j: (i // per, 0, j)),
        ],
        out_specs=pl.BlockSpec((tm, tn), lambda i, j: (i, j)),
        out_shape=jax.ShapeDtypeStruct((m, d), F32),
        compiler_params=_params("parallel", "arbitrary"),
        name="out_proj",
    )(a, w, x, gate)


def _final_norm_kernel(x_ref, w_ref, o_ref):
    x = x_ref[...]
    o_ref[...] = x * lax.rsqrt(jnp.mean(x * x, axis=-1, keepdims=True) + EPS) * w_ref[...]


def _final_norm(x, w):
    m, d = x.shape
    tr = _tile(m, 256)
    return pl.pallas_call(
        _final_norm_kernel,
        grid=(m // tr,),
        in_specs=[pl.BlockSpec((tr, d), lambda i: (i, 0)), pl.BlockSpec((1, d), lambda i: (0, 0))],
        out_specs=pl.BlockSpec((tr, d), lambda i: (i, 0)),
        out_shape=jax.ShapeDtypeStruct((m, d), F32),
        compiler_params=_params("parallel"),
        name="final_norm",
    )(x, w.reshape(1, d))


def _rope_tables(pos, dh, dqk):
    rope_dim = dqk // 4
    half = rope_dim // 2
    inv_freq = ROPE_THETA ** (-jnp.arange(half, dtype=F32) * (2.0 / rope_dim))
    ang = pos.astype(F32)[:, None] * inv_freq[None, :]
    cos, sin = jnp.cos(ang), jnp.sin(ang)
    n = pos.shape[0]
    pad = jnp.zeros((n, dqk - rope_dim), F32)
    comp_cos = jnp.concatenate([cos, cos, pad + 1.0], axis=1)
    comp_sa = jnp.concatenate([-sin, jnp.zeros_like(sin), pad], axis=1)
    comp_sb = jnp.concatenate([jnp.zeros_like(sin), sin, pad], axis=1)
    reps = dh // dqk
    return tuple(jnp.tile(t, (1, reps)) for t in (comp_cos, comp_sa, comp_sb))


def _tail_tile(state, width):
    nb, k, _ = state.shape
    return jnp.concatenate([jnp.zeros((nb, SUBLANES - k, width), F32), state.astype(F32)], axis=1)


def _pad_time(x, tp):
    nb, t = x.shape[:2]
    return jnp.pad(x, [(0, 0), (0, tp - t)] + [(0, 0)] * (x.ndim - 2))


def kernel(x_prompt, x_sample, c_prompt, c_sample, state_conv_a, state_gdn_conv, state_gdn, state_gla, cache_k, cache_v, page_table, w_ada, b_ada, w_in, conv_a_w, gdn_conv_w, gdn_a_log, gdn_dt_bias, gdn_norm_w, diff_lambda, diff_norm_w, gla_w_gk2, gla_b_gk, gla_norm_w, w_branch, w_out, final_norm_w):
    nbp, tpp, d = x_prompt.shape
    nbs, tvs, _ = x_sample.shape
    depth = w_in.shape[0]
    bw = d // 4
    gdn_h = gdn_a_log.shape[1]
    diff_h = cache_k.shape[3]
    dh = cache_k.shape[4]
    dqk = dh // 2
    gla_h = state_gla.shape[2]
    gla_dk = state_gla.shape[3]
    rank = gla_w_gk2.shape[1]
    conv_a_k = conv_a_w.shape[1]
    gdn_k = gdn_conv_w.shape[1]
    past_len = page_table.shape[1] * cache_k.shape[2]
    tps = SUBLANES
    assert tvs <= tps and 2 * gdn_h + rank <= LANES and bw // gdn_h == LANES and dh == LANES

    off_gdn = 4 * bw
    off_side = off_gdn + 4 * bw
    off_diff = off_side + 2 * gdn_h
    off_gla = off_diff + 4 * bw
    off_lr = off_gla + 3 * bw
    off_merge = off_lr + rank
    w_t = jnp.swapaxes(w_in, 1, 2)
    w_side = _side_weights(w_t, off_side, 2 * gdn_h, off_lr, rank)
    w_merge = _repack(w_t, off_merge, 4 * d)
    w_branch_b = w_branch.astype(BF16)
    w_out_b = w_out.astype(BF16)
    wgk_pad = jnp.concatenate(
        [jnp.zeros((depth, 2 * gdn_h, gla_h * gla_dk), F32), gla_w_gk2,
         jnp.zeros((depth, LANES - 2 * gdn_h - rank, gla_h * gla_dk), F32)], axis=1).astype(BF16)
    lane_pad = jnp.zeros((depth, LANES - gdn_h), F32)
    alog_pad = jnp.concatenate([gdn_a_log, lane_pad], axis=1)
    dtb_pad = jnp.concatenate([gdn_dt_bias, lane_pad], axis=1)

    pool, page = cache_k.shape[1:3]
    cache_k_rows = cache_k.reshape(depth, pool, page * diff_h, dh)
    cache_v_rows = cache_v.reshape(depth, pool, page * diff_h, dh)

    n_c = nbp + nbs
    c_rows = -(-n_c // SUBLANES) * SUBLANES
    c_all = jnp.concatenate([c_prompt, c_sample, jnp.zeros((c_rows - n_c, d), F32)], axis=0)
    mod = _ada(c_all, w_ada, b_ada)

    groups = []
    groups.append(dict(
        nb=nbp, tp=tpp, tv=tpp, x=x_prompt.reshape(nbp * tpp, d), mod_rows=slice(0, nbp), per_row=False,
        conv_a=jnp.zeros((depth, nbp, conv_a_k - 1, bw), F32),
        gdn_conv=jnp.zeros((depth, nbp, gdn_k - 1, 3 * bw), F32),
        gdn_s=jnp.zeros((depth, nbp) + state_gdn.shape[2:], F32),
        gla_s=jnp.zeros((depth, nbp) + state_gla.shape[2:], F32),
        tabs=_rope_tables(jnp.arange(tpp, dtype=jnp.int32), dh, dqk), paged=False))
    groups.append(dict(
        nb=nbs, tp=tps, tv=tvs, x=_pad_time(x_sample, tps).reshape(nbs * tps, d), mod_rows=slice(nbp, n_c),
        per_row=True, conv_a=state_conv_a, gdn_conv=state_gdn_conv, gdn_s=state_gdn, gla_s=state_gla,
        tabs=_rope_tables(past_len + jnp.arange(tps, dtype=jnp.int32), dh, dqk), paged=True))

    results = []
    for g in groups:
        nb, tp, tv = g["nb"], g["tp"], g["tv"]
        x = g["x"]
        outs = [[] for _ in range(4)]
        kv_stacks = None
        for l in range(depth):
            m_l = mod[l, g["mod_rows"]]
            if g["per_row"]:
                m_l = jnp.repeat(m_l, tp, axis=0).reshape(1, nb * tp, 3 * d)
            else:
                m_l = m_l.reshape(nb, 1, 3 * d)
            shift, scale, gate = m_l[..., :d], m_l[..., d:2 * d], m_l[..., 2 * d:]
            h = _prenorm(x, scale, shift, tp)
            z_ab = _matmul_t(h, w_t, l, 0, off_side)
            z_c = _matmul_t(h, w_t, l, off_diff, 4 * bw)
            z_d = _matmul_t(h, w_t, l, off_gla, 3 * bw)
            zs = _matmul(h, w_side, l, tn_pref=LANES)

            out_a, tail_a = _conva(z_ab, _tail_tile(g["conv_a"][l], bw), conv_a_w[l], nb, tp, bw)
            out_b, tail_b, s_gdn = _gdn(
                z_ab, zs, _tail_tile(g["gdn_conv"][l], 3 * bw), gdn_conv_w[l], alog_pad[l:l + 1],
                dtb_pad[l:l + 1], gdn_norm_w[l:l + 1], g["gdn_s"][l].astype(F32), nb, tp, tv, bw, gdn_h, 1)
            q_r, k_r, v_r, *kv_stacks = _rope(z_c, g["tabs"], nb, tp, bw, diff_h, 0, l, depth, kv_stacks)
            lam_init = 0.8 - 0.6 * math.exp(-0.3 * l)
            if g["paged"]:
                out_c = _decode(q_r, k_r, v_r, z_c, cache_k_rows, cache_v_rows, page_table, l, diff_lambda[l],
                                diff_norm_w[l:l + 1], nb, tv, bw, diff_h, lam_init, 3)
            else:
                out_c = _flash(q_r, k_r, v_r, z_c, diff_lambda[l], diff_norm_w[l:l + 1], nb, tp, bw, diff_h,
                               lam_init, 3 * diff_h)
            out_d, s_gla_t = _gla(
                z_d, zs, wgk_pad[l], gla_b_gk[l:l + 1], gla_norm_w[l:l + 1],
                jnp.swapaxes(g["gla_s"][l].astype(F32), -1, -2), nb, tp, tv, bw, gla_h, 0)

            merged = _merge(h, (out_a, out_b, out_c, out_d), w_merge, w_branch_b, l)
            x = _outproj(merged, w_out_b, l, x, gate, tp)

            last = tv - (tp - SUBLANES)
            outs[0].append(tail_a[:, last - (conv_a_k - 1):last])
            outs[1].append(tail_b[:, last - (gdn_k - 1):last])
            outs[2].append(s_gdn)
            outs[3].append(jnp.swapaxes(s_gla_t, -1, -2))
        y = _final_norm(x, final_norm_w).reshape(nb, tp, d)[:, :tv]
        new_k, new_v = (s.reshape(depth, nb, tp, diff_h, dh)[:, :, :tv] for s in kv_stacks)
        results.append((y, *[jnp.stack(o) for o in outs], new_k, new_v))

    (yp, *sp), (ys, *ss) = results
    return (yp, ys, *sp, *ss)
```

```python
import functools
import math

import jax
import jax.numpy as jnp
import numpy as np
from jax import lax
from jax.experimental import pallas as pl
from jax.experimental.pallas import tpu as pltpu

F32 = jnp.float32
BF16 = jnp.bfloat16
EPS = 1e-6
GDN_CHUNK = 64
GDN_CHUNKS_PER_STEP = 4
DECODE_PAGES_PER_STEP = 16
FLASH_BLOCK = 256
FLASH_KV_BLOCKS = 2
FLASH_HEADS = 4
GLA_CHUNK = 32
GLA_CHUNKS_PER_STEP = 8
GLA_NORMALIZER = 16.0
ROPE_THETA = 500000.0
SUBLANES = 8
LANES = 128
VMEM_LIMIT = 56 * 1024 * 1024
HIGHEST = lax.Precision.HIGHEST

_NT = (((1,), (1,)), ((), ()))
_TN = (((0,), (0,)), ((), ()))
_BNN = (((2,), (1,)), ((0,), (0,)))
_BNT = (((2,), (2,)), ((0,), (0,)))
_BTN = (((1,), (1,)), ((0,), (0,)))


def _params(*sem):
    return pltpu.CompilerParams(dimension_semantics=sem, vmem_limit_bytes=VMEM_LIMIT)


def _silu(x):
    return x * jax.nn.sigmoid(x)


def _softplus(x):
    return jnp.maximum(x, 0.0) + jnp.log(1.0 + jnp.exp(-jnp.abs(x)))


def _dot(a, b):
    return jnp.dot(a.astype(BF16), b.astype(BF16), preferred_element_type=F32)


def _dot_nt(a, b):
    return lax.dot_general(a.astype(BF16), b.astype(BF16), _NT, preferred_element_type=F32)


def _dot_tn(a, b):
    return lax.dot_general(a.astype(BF16), b.astype(BF16), _TN, preferred_element_type=F32)


def _bdot(a, b, dims=_BNN):
    return lax.dot_general(a.astype(BF16), b.astype(BF16), dims, preferred_element_type=F32)


def _tile(n, pref):
    if n <= pref:
        return n
    t = pref
    while n % t:
        t //= 2
    return t


def _ada_kernel(c_ref, w_ref, b_ref, o_ref):
    c = c_ref[...]
    o_ref[0] = _dot(_silu(c), w_ref[0]) + b_ref[0]


def _ada(c_all, w_ada, b_ada):
    depth, d, n = w_ada.shape
    rows = c_all.shape[0]
    tn = _tile(n, 1024)
    return pl.pallas_call(
        _ada_kernel,
        grid=(depth, n // tn),
        in_specs=[
            pl.BlockSpec((rows, d), lambda l, j: (0, 0)),
            pl.BlockSpec((1, d, tn), lambda l, j: (l, 0, j)),
            pl.BlockSpec((1, 1, tn), lambda l, j: (l, 0, j)),
        ],
        out_specs=pl.BlockSpec((1, rows, tn), lambda l, j: (l, 0, j)),
        out_shape=jax.ShapeDtypeStruct((depth, rows, n), F32),
        compiler_params=_params("parallel", "parallel"),
        name="ada_mod",
    )(c_all, w_ada, b_ada.reshape(depth, 1, n))


def _repack_kernel(a_ref, o_ref):
    o_ref[...] = a_ref[0].T.astype(BF16)


def _repack(w_t, start, width):
    depth, _, k = w_t.shape
    tn = _tile(width, 1024)
    tk = _tile(k, 2048)
    return pl.pallas_call(
        _repack_kernel,
        grid=(depth, k // tk, width // tn),
        in_specs=[pl.BlockSpec((pl.Element(1), pl.Element(tn), pl.Element(tk)),
                               lambda l, i, j: (l, pl.multiple_of(start + j * tn, SUBLANES),
                                                pl.multiple_of(i * tk, LANES)))],
        out_specs=pl.BlockSpec((None, tk, tn), lambda l, i, j: (l, i, j)),
        out_shape=jax.ShapeDtypeStruct((depth, k, width), BF16),
        compiler_params=_params("parallel", "parallel", "parallel"),
        name="repack",
    )(w_t)


def _side_kernel(a_ref, b_ref, o_ref, *, n_a, n_b):
    lane = lax.broadcasted_iota(jnp.int32, o_ref.shape, 1)
    o_ref[...] = jnp.where(lane < n_a, a_ref[0].T, jnp.where(lane < n_a + n_b, b_ref[0].T, 0.0)).astype(BF16)


def _side_weights(w_t, start_a, n_a, start_b, n_b):
    depth, _, k = w_t.shape
    assert n_a + n_b <= LANES
    tk = _tile(k, 512)
    return pl.pallas_call(
        functools.partial(_side_kernel, n_a=n_a, n_b=n_b),
        grid=(depth, k // tk),
        in_specs=[
            pl.BlockSpec((pl.Element(1), pl.Element(LANES), pl.Element(tk)), lambda l, i: (l, start_a, i * tk)),
            pl.BlockSpec((pl.Element(1), pl.Element(LANES), pl.Element(tk)),
                         lambda l, i: (l, start_b - n_a, i * tk)),
        ],
        out_specs=pl.BlockSpec((None, tk, LANES), lambda l, i: (l, i, 0)),
        out_shape=jax.ShapeDtypeStruct((depth, k, LANES), BF16),
        compiler_params=_params("parallel", "parallel"),
        name="repack_side",
    )(w_t, w_t)


def _prenorm_kernel(x_ref, sc_ref, sh_ref, h_ref):
    x = x_ref[...]
    y = x * lax.rsqrt(jnp.mean(x * x, axis=-1, keepdims=True) + EPS)
    h_ref[...] = (y * (1.0 + sc_ref[0]) + sh_ref[0]).astype(BF16)


def _prenorm(x, scale, shift, rows_per_mod):
    m, d = x.shape
    r = scale.shape[1]
    tr = r if r > 1 else _tile(rows_per_mod, 512)
    per = rows_per_mod // tr if r == 1 else 1
    return pl.pallas_call(
        _prenorm_kernel,
        grid=(m // tr,),
        in_specs=[
            pl.BlockSpec((tr, d), lambda i: (i, 0)),
            pl.BlockSpec((1, r, d), lambda i: (i // per, 0, 0)),
            pl.BlockSpec((1, r, d), lambda i: (i // per, 0, 0)),
        ],
        out_specs=pl.BlockSpec((tr, d), lambda i: (i, 0)),
        out_shape=jax.ShapeDtypeStruct((m, d), BF16),
        compiler_params=_params("parallel"),
        name="prenorm",
    )(x, scale, shift)


def _mm_kernel(a_ref, w_ref, o_ref):
    o_ref[...] = jnp.dot(a_ref[...], w_ref[...], preferred_element_type=F32).astype(o_ref.dtype)


def _matmul(a, w, layer, col0=0, n=None, out_dtype=F32, tm_pref=1024, tn_pref=1024):
    m, k = a.shape
    n = w.shape[2] if n is None else n
    tm = _tile(m, tm_pref)
    tn = _tile(math.gcd(n, col0) if col0 else n, tn_pref)
    return pl.pallas_call(
        _mm_kernel,
        grid=(m // tm, n // tn),
        in_specs=[
            pl.BlockSpec((tm, k), lambda i, j: (i, 0)),
            pl.BlockSpec((None, k, tn), lambda i, j: (layer, 0, col0 // tn + j)),
        ],
        out_specs=pl.BlockSpec((tm, tn), lambda i, j: (i, j)),
        out_shape=jax.ShapeDtypeStruct((m, n), out_dtype),
        compiler_params=_params("parallel", "arbitrary"),
        name="in_proj",
    )(a, w)


def _causal_conv(carry_ref, st_ref, w_ref, u):
    n = u.shape[0]

    @pl.when(pl.program_id(1) == 0)
    def _():
        carry_ref[...] = st_ref[0]

    ext = jnp.concatenate([carry_ref[...], u], axis=0)
    w = w_ref[...]
    taps = w.shape[0]
    y = w[taps - 1:taps] * u
    for k in range(1, taps):
        y = y + w[taps - 1 - k:taps - k] * pltpu.roll(ext, k, 0)[SUBLANES:SUBLANES + n]
    tail = u[n - SUBLANES:n]
    carry_ref[...] = tail
    return y, tail


def _valid_rows(c, t_valid):
    t = pl.program_id(1)
    row = lax.broadcasted_iota(jnp.int32, (c, 1), 0) + t * c
    return row < t_valid


def _tri(c):
    row = lax.broadcasted_iota(jnp.int32, (c, c), 0)
    col = lax.broadcasted_iota(jnp.int32, (c, c), 1)
    return row, col


def _conva_kernel(z_ref, st_ref, w_ref, o_ref, tail_ref, carry, *, bw):
    z = z_ref[...]
    a_h, a_b, a_c, a_z = (z[:, i * bw:(i + 1) * bw] for i in range(4))
    y, tail = _causal_conv(carry, st_ref, w_ref, a_c * a_h)
    o_ref[...] = (a_b * y * _silu(a_z)).astype(BF16)
    tail_ref[0] = tail


def _conva(z, state_tile, w, nb, tp, bw):
    tt = _tile(tp, 512)
    nt = tp // tt
    return pl.pallas_call(
        functools.partial(_conva_kernel, bw=bw),
        grid=(nb, nt),
        in_specs=[
            pl.BlockSpec((tt, 4 * bw), lambda b, t: (b * nt + t, 0)),
            pl.BlockSpec((1, SUBLANES, bw), lambda b, t: (b, 0, 0)),
            pl.BlockSpec(w.shape, lambda b, t: (0, 0)),
        ],
        out_specs=[
            pl.BlockSpec((tt, bw), lambda b, t: (b * nt + t, 0)),
            pl.BlockSpec((1, SUBLANES, bw), lambda b, t: (b, 0, 0)),
        ],
        out_shape=[
            jax.ShapeDtypeStruct((nb * tp, bw), BF16),
            jax.ShapeDtypeStruct((nb, SUBLANES, bw), F32),
        ],
        scratch_shapes=[pltpu.VMEM((SUBLANES, bw), F32)],
        compiler_params=_params("parallel", "arbitrary"),
        name="branch_conv",
    )(z, state_tile, w)


INVERSE_BASE = 8


def _inverse_masks(row, col, c):
    base = min(INVERSE_BASE, c)
    masks = [(row // base) == (col // base)]
    b = base
    while b < c:
        masks.append(((row // (2 * b)) == (col // (2 * b))) & ((row // b) != (col // b)))
        b *= 2
    return masks


def _unit_lower_inverse(lmat, eye, masks, mm):
    base = min(INVERSE_BASE, lmat.shape[-1])
    p = jnp.where(masks[0], lmat, 0.0)
    x = eye - p
    for _ in range(max(0, int(math.log2(base)) - 1)):
        p = mm(p, p)
        x = x + mm(x, p)
    for m in masks[1:]:
        x = x - mm(mm(x, jnp.where(m, lmat, 0.0)), x)
    return x


def _gdn_kernel(z_ref, zs_ref, cst_ref, cw_ref, alog_ref, dtb_ref, nw_ref, s0_ref,
                o_ref, tail_ref, sout_ref, carry, s_scr, *, bw, heads, t_valid, c):
    t = pl.program_id(1)
    nt = pl.num_programs(1)

    @pl.when(t == 0)
    def _():
        s_scr[...] = s0_ref[0]

    z = z_ref[...]
    r = z.shape[0]
    n_chunks = r // c
    dk = bw // heads
    gz = z[:, 3 * bw:4 * bw]
    y, tail = _causal_conv(carry, cst_ref, cw_ref, z[:, :3 * bw])
    tail_ref[0] = tail
    qkv = _silu(y)

    valid = _valid_rows(r, t_valid).astype(F32)
    zs = zs_ref[...]
    g = valid * (-jnp.exp(alog_ref[...]) * _softplus(zs + dtb_ref[...]))
    beta = valid * jax.nn.sigmoid(zs)
    row_r, col_r = _tri(r)
    same_chunk = (row_r // c) == (col_r // c)
    gsum = jnp.dot(((row_r >= col_r) & same_chunk).astype(F32), g, precision=HIGHEST,
                   preferred_element_type=F32)
    row, col = _tri(c)
    incl = row >= col
    strict = row > col
    eye = (row == col).astype(F32)
    inv_masks = _inverse_masks(row, col, c)
    r128, c128 = _tri(LANES)
    eye128 = (r128 == c128).astype(F32)
    rows = [slice(ci * c, (ci + 1) * c) for ci in range(n_chunks)]
    gsum_t = [lax.dot_general(eye128, gsum[rs], _NT, precision=HIGHEST, preferred_element_type=F32)
              for rs in rows]

    def per(fn):
        return jnp.stack([fn(ci, rows[ci], h) for ci in range(n_chunks) for h in range(heads)])

    q = per(lambda ci, rs, h: qkv[rs, h * dk:(h + 1) * dk])
    k = per(lambda ci, rs, h: qkv[rs, bw + h * dk:bw + (h + 1) * dk])
    v = per(lambda ci, rs, h: qkv[rs, 2 * bw + h * dk:2 * bw + (h + 1) * dk])
    gcol = per(lambda ci, rs, h: jnp.broadcast_to(gsum[rs, h:h + 1], (c, dk)))
    bcol = per(lambda ci, rs, h: jnp.broadcast_to(beta[rs, heads + h:heads + h + 1], (c, dk)))
    vcol = per(lambda ci, rs, h: jnp.broadcast_to(valid[rs], (c, dk)))
    grow = per(lambda ci, rs, h: gsum_t[ci][h:h + 1, :])

    qn = q * lax.rsqrt(jnp.sum(q * q, axis=-1, keepdims=True) + EPS) * (dk ** -0.5)
    kn = vcol * (k * lax.rsqrt(jnp.sum(k * k, axis=-1, keepdims=True) + EPS))
    eg = jnp.exp(gcol)
    decay = jnp.exp(jnp.where(incl, gcol[:, :, :c] - grow, 0.0))
    kb = kn * bcol
    kq = _bdot(jnp.concatenate([kb, qn], axis=1), kn, _BNT)
    lmat = jnp.where(strict, kq[:, :c] * decay, 0.0)
    qk = jnp.where(incl, kq[:, c:] * decay, 0.0)
    tinv = _unit_lower_inverse(lmat, eye, inv_masks, _bdot)
    sol = _bdot(tinv, jnp.concatenate([v * bcol, kb * eg], axis=2))
    u = sol[:, :, :dk]
    wq = jnp.concatenate([sol[:, :, dk:], qn * eg], axis=1)
    g_last = gcol[:, c - 1:c, :]
    k_dec = kn * jnp.exp(g_last - gcol)
    s_decay = jnp.exp(g_last)

    s = s_scr[...]
    for ci in range(n_chunks):
        sel = slice(ci * heads, (ci + 1) * heads)
        ws = _bdot(wq[sel], s)
        v_new = u[sel] - ws[:, :c]
        o = ws[:, c:] + _bdot(qk[sel], v_new)
        s = s * s_decay[sel] + _bdot(k_dec[sel], v_new, _BTN)
        on = o * lax.rsqrt(jnp.mean(o * o, axis=-1, keepdims=True) + EPS) * nw_ref[...]
        for h in range(heads):
            sl = slice(h * dk, (h + 1) * dk)
            o_ref[rows[ci], sl] = (on[h] * _silu(gz[rows[ci], sl])).astype(BF16)
    s_scr[...] = s

    @pl.when(t == nt - 1)
    def _():
        sout_ref[0] = s


def _gdn(z, zs, conv_tile, conv_w, alog, dtb, norm_w, s0, nb, tp, t_valid, bw, heads, col_block):
    c = min(GDN_CHUNK, tp)
    r = min(GDN_CHUNK * GDN_CHUNKS_PER_STEP, tp)
    nt = tp // r
    dk = bw // heads
    return pl.pallas_call(
        functools.partial(_gdn_kernel, bw=bw, heads=heads, t_valid=t_valid, c=c),
        grid=(nb, nt),
        in_specs=[
            pl.BlockSpec((r, 4 * bw), lambda b, t: (b * nt + t, col_block)),
            pl.BlockSpec((r, LANES), lambda b, t: (b * nt + t, 0)),
            pl.BlockSpec((1, SUBLANES, 3 * bw), lambda b, t: (b, 0, 0)),
            pl.BlockSpec(conv_w.shape, lambda b, t: (0, 0)),
            pl.BlockSpec((1, LANES), lambda b, t: (0, 0)),
            pl.BlockSpec((1, LANES), lambda b, t: (0, 0)),
            pl.BlockSpec((1, dk), lambda b, t: (0, 0)),
            pl.BlockSpec((1, heads, dk, dk), lambda b, t: (b, 0, 0, 0)),
        ],
        out_specs=[
            pl.BlockSpec((r, bw), lambda b, t: (b * nt + t, 0)),
            pl.BlockSpec((1, SUBLANES, 3 * bw), lambda b, t: (b, 0, 0)),
            pl.BlockSpec((1, heads, dk, dk), lambda b, t: (b, 0, 0, 0)),
        ],
        out_shape=[
            jax.ShapeDtypeStruct((nb * tp, bw), BF16),
            jax.ShapeDtypeStruct((nb, SUBLANES, 3 * bw), F32),
            jax.ShapeDtypeStruct((nb, heads, dk, dk), F32),
        ],
        scratch_shapes=[pltpu.VMEM((SUBLANES, 3 * bw), F32), pltpu.VMEM((heads, dk, dk), F32)],
        compiler_params=_params("parallel", "arbitrary"),
        name="branch_gdn",
    )(z, zs, conv_tile, conv_w, alog, dtb, norm_w, s0)


def _gla_kernel(z_ref, zs_ref, wgk_ref, bgk_ref, nw_ref, s0_ref, o_ref, sout_ref, s_scr,
                *, bw, heads, t_valid, c):
    t = pl.program_id(1)
    nt = pl.num_programs(1)

    @pl.when(t == 0)
    def _():
        s_scr[...] = s0_ref[0]

    z = z_ref[...]
    r = z.shape[0]
    n_chunks = r // c
    dk = bw // (2 * heads)
    dv = bw // heads
    valid = _valid_rows(r, t_valid).astype(F32)
    gk = valid * (-_softplus(-(_dot(zs_ref[...], wgk_ref[...]) + bgk_ref[...])) / GLA_NORMALIZER)
    row_r, col_r = _tri(r)
    same_chunk = (row_r // c) == (col_r // c)
    gsum = jnp.dot(((row_r >= col_r) & same_chunk).astype(F32), gk, precision=HIGHEST,
                   preferred_element_type=F32)
    row, col = _tri(c)
    incl = row >= col
    lz = z[:, 2 * bw:3 * bw]
    rows = [slice(ci * c, (ci + 1) * c) for ci in range(n_chunks)]

    def per(fn):
        return jnp.stack([fn(rows[ci], h) for ci in range(n_chunks) for h in range(heads)])

    q = per(lambda rs, h: z[rs, h * dk:(h + 1) * dk]) * (dk ** -0.5)
    k = per(lambda rs, h: z[rs, bw // 2 + h * dk:bw // 2 + (h + 1) * dk] * valid[rs])
    v = per(lambda rs, h: z[rs, bw + h * dv:bw + (h + 1) * dv])
    g = per(lambda rs, h: gsum[rs, h * dk:(h + 1) * dk])
    q_dec = q * jnp.exp(g)
    attn = jnp.where(incl, _bdot(q_dec, k * jnp.exp(-g), _BNT), 0.0)
    o_intra = _bdot(attn, v)
    g_last = g[:, c - 1:c, :]
    k_dec = k * jnp.exp(g_last - g)
    s_decay = jnp.exp(g_last)

    s_t = s_scr[...]
    for ci in range(n_chunks):
        sel = slice(ci * heads, (ci + 1) * heads)
        o = o_intra[sel] + _bdot(q_dec[sel], s_t, _BNT)
        s_t = s_t * s_decay[sel] + _bdot(v[sel], k_dec[sel], _BTN)
        on = o * lax.rsqrt(jnp.mean(o * o, axis=-1, keepdims=True) + EPS) * nw_ref[...]
        for h in range(heads):
            sl = slice(h * dv, (h + 1) * dv)
            o_ref[rows[ci], sl] = (on[h] * _silu(lz[rows[ci], sl])).astype(BF16)
    s_scr[...] = s_t

    @pl.when(t == nt - 1)
    def _():
        sout_ref[0] = s_t


def _gla(z, zs, wgk_pad, bgk, norm_w, s0_t, nb, tp, t_valid, bw, heads, col_block):
    c = min(GLA_CHUNK, tp)
    r = min(GLA_CHUNK * GLA_CHUNKS_PER_STEP, tp)
    nt = tp // r
    dk = bw // (2 * heads)
    dv = bw // heads
    return pl.pallas_call(
        functools.partial(_gla_kernel, bw=bw, heads=heads, t_valid=t_valid, c=c),
        grid=(nb, nt),
        in_specs=[
            pl.BlockSpec((r, 3 * bw), lambda b, t: (b * nt + t, col_block)),
            pl.BlockSpec((r, LANES), lambda b, t: (b * nt + t, 0)),
            pl.BlockSpec(wgk_pad.shape, lambda b, t: (0, 0)),
            pl.BlockSpec(bgk.shape, lambda b, t: (0, 0)),
            pl.BlockSpec((1, dv), lambda b, t: (0, 0)),
            pl.BlockSpec((1, heads, dv, dk), lambda b, t: (b, 0, 0, 0)),
        ],
        out_specs=[
            pl.BlockSpec((r, bw), lambda b, t: (b * nt + t, 0)),
            pl.BlockSpec((1, heads, dv, dk), lambda b, t: (b, 0, 0, 0)),
        ],
        out_shape=[
            jax.ShapeDtypeStruct((nb * tp, bw), BF16),
            jax.ShapeDtypeStruct((nb, heads, dv, dk), F32),
        ],
        scratch_shapes=[pltpu.VMEM((heads, dv, dk), F32)],
        compiler_params=_params("parallel", "arbitrary"),
        name="branch_gla",
    )(z, zs, wgk_pad, bgk, norm_w, s0_t)


def _rope_kernel(z_ref, cos_ref, sa_ref, sb_ref, *refs, bw, heads, half, scale):
    q_ref, k_ref, v_ref, nk_ref, nv_ref = refs[-5:]
    z = z_ref[...]
    cos = cos_ref[...]
    sa = sa_ref[...]
    sb = sb_ref[...]
    dh = bw // heads

    def rope(x):
        return x * cos + pltpu.roll(x, dh - half, 1) * sa + pltpu.roll(x, half, 1) * sb

    for h in range(heads):
        sl = slice(h * dh, (h + 1) * dh)
        q = rope(z[:, sl])
        k = rope(z[:, bw + h * dh:bw + (h + 1) * dh])
        q_ref[:, sl] = (q * scale).astype(BF16)
        k_ref[:, sl] = k.astype(BF16)
        nk_ref[:, sl] = k
    v = z[:, 2 * bw:3 * bw]
    v_ref[...] = v.astype(BF16)
    nv_ref[...] = v


def _rope(z, tabs, nb, tp, bw, heads, col_block, layer, depth, stacks):
    tt = _tile(tp, 512)
    nt = tp // tt
    dh = bw // heads
    dqk = dh // 2
    kern = functools.partial(_rope_kernel, bw=bw, heads=heads, half=dqk // 8, scale=dqk ** -0.5)
    rows = nb * tp
    spec = pl.BlockSpec((tt, bw), lambda i: (i, 0))
    tspec = pl.BlockSpec((tt, dh), lambda i: (i % nt, 0))
    return pl.pallas_call(
        kern,
        grid=(rows // tt,),
        in_specs=[pl.BlockSpec((tt, 4 * bw), lambda i: (i, col_block)), tspec, tspec, tspec]
        + ([pl.BlockSpec(memory_space=pl.ANY)] * 2 if stacks else []),
        out_specs=[spec] * 3 + [pl.BlockSpec((None, tt, bw), lambda i: (layer, i, 0))] * 2,
        out_shape=[jax.ShapeDtypeStruct((rows, bw), BF16)] * 3
        + [jax.ShapeDtypeStruct((depth, rows, bw), F32)] * 2,
        input_output_aliases={4: 3, 5: 4} if stacks else {},
        compiler_params=_params("parallel"),
        name="diff_rope",
    )(z, *tabs, *(stacks or ()))


def _lambda(lam_ref, lam_init):
    lf = lam_ref[...]
    a = jnp.sum(lf[0:1] * lf[1:2], axis=-1, keepdims=True)
    b = jnp.sum(lf[2:3] * lf[3:4], axis=-1, keepdims=True)
    return jnp.exp(a) - jnp.exp(b) + lam_init


def _stack_components(q, dqk):
    lane = lax.broadcasted_iota(jnp.int32, q.shape, 1)
    zero = jnp.zeros_like(q)
    return jnp.concatenate([jnp.where(lane < dqk, q, zero), jnp.where(lane >= dqk, q, zero)], axis=0)


def _online_update(s, v, m, l, acc, mm=_dot):
    m_new = jnp.maximum(m, jnp.max(s, axis=-1, keepdims=True))
    alpha = jnp.exp(m - m_new)
    p = jnp.exp(s - m_new)
    return m_new, alpha * l + jnp.sum(p, axis=-1, keepdims=True), alpha * acc + mm(p, v)


def _attn_finish(acc, l, n, lam, lam_init, nw, dz):
    o = acc[:n] / l[:n] - lam * (acc[n:] / l[n:])
    on = o * lax.rsqrt(jnp.mean(o * o, axis=-1, keepdims=True) + EPS) * nw * (1.0 - lam_init)
    return (on * _silu(dz)).astype(BF16)


def _flash_kernel(q_ref, k_ref, v_ref, dz_ref, lam_ref, nw_ref, o_ref, *, dqk, lam_init, hb, tk):
    i = pl.program_id(2)
    tq = q_ref.shape[0]
    dh = 2 * dqk
    heads = [slice(h * dh, (h + 1) * dh) for h in range(hb)]
    qs = jnp.stack([_stack_components(q_ref[:, sl], dqk) for sl in heads])

    def block(j):
        start = pl.multiple_of(j * tk, tk)
        kb = k_ref[pl.ds(start, tk), :]
        vb = v_ref[pl.ds(start, tk), :]
        s = _bdot(qs, jnp.stack([kb[:, sl] for sl in heads]), _BNT)
        return s, jnp.stack([vb[:, sl] for sl in heads])

    def body(j, carry):
        s, v = block(j)
        return _online_update(s, v, *carry, mm=_bdot)

    init = (jnp.full((hb, 2 * tq, 1), -jnp.inf, F32), jnp.zeros((hb, 2 * tq, 1), F32),
            jnp.zeros((hb, 2 * tq, dh), F32))
    n_past = (i * tq) // tk
    carry = lax.fori_loop(0, n_past, body, init)
    s, v = block(n_past)
    row = i * tq + lax.broadcasted_iota(jnp.int32, (1,) + s.shape[1:], 1) % tq
    col = n_past * tk + lax.broadcasted_iota(jnp.int32, (1,) + s.shape[1:], 2)
    m, l, acc = _online_update(jnp.where(col <= row, s, -jnp.inf), v, *carry, mm=_bdot)
    lam = _lambda(lam_ref, lam_init)
    for h, sl in enumerate(heads):
        o_ref[:, sl] = _attn_finish(acc[h], l[h], tq, lam, lam_init, nw_ref[...], dz_ref[:, sl])


def _flash(q, k, v, z, lam, norm_w, nb, tp, bw, heads, lam_init, dz_col0):
    dh = bw // heads
    hb = _tile(heads, FLASH_HEADS)
    tq = _tile(tp, FLASH_BLOCK)
    nq = tp // tq
    tk = FLASH_KV_BLOCKS * tq if tp % (FLASH_KV_BLOCKS * tq) == 0 else tq
    kern = functools.partial(_flash_kernel, dqk=dh // 2, lam_init=lam_init, hb=hb, tk=tk)
    return pl.pallas_call(
        kern,
        grid=(nb, heads // hb, nq),
        in_specs=[
            pl.BlockSpec((tq, hb * dh), lambda b, h, i: (b * nq + i, h)),
            pl.BlockSpec((tp, hb * dh), lambda b, h, i: (b, h)),
            pl.BlockSpec((tp, hb * dh), lambda b, h, i: (b, h)),
            pl.BlockSpec((tq, hb * dh), lambda b, h, i: (b * nq + i, dz_col0 // hb + h)),
            pl.BlockSpec(lam.shape, lambda b, h, i: (0, 0)),
            pl.BlockSpec((1, dh), lambda b, h, i: (0, 0)),
        ],
        out_specs=pl.BlockSpec((tq, hb * dh), lambda b, h, i: (b * nq + i, h)),
        out_shape=jax.ShapeDtypeStruct((nb * tp, bw), BF16),
        compiler_params=_params("parallel", "parallel", "arbitrary"),
        name="diff_flash",
    )(q, k, v, z, lam, norm_w)


def _decode_kernel(pt_ref, q_ref, *refs, heads, dqk, t_valid, lam_init, pps):
    kc_refs, vc_refs = refs[:pps], refs[pps:2 * pps]
    kn_ref, vn_ref, dz_ref, lam_ref, nw_ref, o_ref, m_scr, l_scr, acc_scr = refs[2 * pps:]
    j = pl.program_id(1)
    nj = pl.num_programs(1)
    n = q_ref.shape[0]
    dh = 2 * dqk
    rows = 2 * n * heads
    head_lanes = [slice(h * dh, (h + 1) * dh) for h in range(heads)]

    @pl.when(j == 0)
    def _():
        m_scr[...] = jnp.full(m_scr.shape, -jnp.inf, F32)
        l_scr[...] = jnp.zeros(l_scr.shape, F32)
        acc_scr[...] = jnp.zeros(acc_scr.shape, F32)

    q_all = jnp.concatenate([_stack_components(q_ref[:, sl].astype(F32), dqk) for sl in head_lanes], axis=0)
    keys = kc_refs[0].shape[0]
    row_head = lax.broadcasted_iota(jnp.int32, (rows, keys), 0) // (2 * n)
    col_head = lax.broadcasted_iota(jnp.int32, (rows, keys), 1) % heads
    same_head = row_head == col_head
    m, l, acc = m_scr[...], l_scr[...], acc_scr[...]
    scores = [jnp.where(same_head, _dot_nt(q_all, kc_ref[...]), -jnp.inf) for kc_ref in kc_refs]
    m_new = m
    for s in scores:
        m_new = jnp.maximum(m_new, jnp.max(s, axis=-1, keepdims=True))
    alpha = jnp.exp(m - m_new)
    l = alpha * l
    acc = alpha * acc
    for s, vc_ref in zip(scores, vc_refs):
        p = jnp.exp(s - m_new)
        l = l + jnp.sum(p, axis=-1, keepdims=True)
        acc = acc + _dot(p, vc_ref[...])
    state = (m_new, l, acc)
    m_scr[...], l_scr[...], acc_scr[...] = state

    @pl.when(j == nj - 1)
    def _():
        k_new = jnp.concatenate([kn_ref[:, sl].astype(F32) for sl in head_lanes], axis=0)
        v_new = jnp.concatenate([vn_ref[:, sl].astype(F32) for sl in head_lanes], axis=0)
        r2 = lax.broadcasted_iota(jnp.int32, (rows, n * heads), 0)
        c2 = lax.broadcasted_iota(jnp.int32, (rows, n * heads), 1)
        visible = ((r2 // (2 * n)) == (c2 // n)) & ((c2 % n) <= (r2 % n)) & ((c2 % n) < t_valid)
        s = jnp.where(visible, _dot_nt(q_all, k_new), -jnp.inf)
        m, l, acc = _online_update(s, v_new, *state)
        lam = _lambda(lam_ref, lam_init)
        for h, sl in enumerate(head_lanes):
            blk = slice(h * 2 * n, (h + 1) * 2 * n)
            o_ref[:, sl] = _attn_finish(acc[blk], l[blk], n, lam, lam_init, nw_ref[...], dz_ref[:, sl])


def _decode(q, k_new, v_new, z, cache_k, cache_v, page_table, layer, lam, norm_w, nb, t_valid, bw, heads,
            lam_init, dz_col_block):
    n_pages = page_table.shape[1]
    dh = bw // heads
    pps = _tile(n_pages, DECODE_PAGES_PER_STEP)
    kern = functools.partial(_decode_kernel, heads=heads, dqk=dh // 2, t_valid=t_valid, lam_init=lam_init, pps=pps)
    row_spec = pl.BlockSpec((SUBLANES, bw), lambda b, j, pt: (b, 0))
    cache_specs = [
        pl.BlockSpec((None, None, cache_k.shape[2], dh),
                     lambda b, j, pt, i=i: (layer, pt[b * n_pages + j * pps + i], 0, 0))
        for i in range(pps)
    ]
    rows = 2 * SUBLANES * heads
    grid_spec = pltpu.PrefetchScalarGridSpec(
        num_scalar_prefetch=1,
        grid=(nb, n_pages // pps),
        in_specs=[row_spec] + cache_specs + cache_specs + [
            row_spec, row_spec,
            pl.BlockSpec((SUBLANES, bw), lambda b, j, pt: (b, dz_col_block)),
            pl.BlockSpec(lam.shape, lambda b, j, pt: (0, 0)),
            pl.BlockSpec((1, dh), lambda b, j, pt: (0, 0)),
        ],
        out_specs=row_spec,
        scratch_shapes=[pltpu.VMEM((rows, 1), F32), pltpu.VMEM((rows, 1), F32), pltpu.VMEM((rows, dh), F32)],
    )
    return pl.pallas_call(
        kern,
        grid_spec=grid_spec,
        out_shape=jax.ShapeDtypeStruct((nb * SUBLANES, bw), BF16),
        compiler_params=_params("parallel", "arbitrary"),
        name="diff_decode",
    )(page_table.reshape(-1), q, *([cache_k] * pps), *([cache_v] * pps), k_new, v_new, z, lam, norm_w)


def _merge_kernel(h_ref, a_ref, b_ref, c_ref, d_ref, m0_ref, m1_ref, m2_ref, m3_ref, wb_ref, o_ref):
    h = h_ref[...]
    acc = None
    for n, (br, wm) in enumerate(zip((a_ref, b_ref, c_ref, d_ref), (m0_ref, m1_ref, m2_ref, m3_ref))):
        gate = jax.nn.sigmoid(jnp.dot(h, wm[...], preferred_element_type=F32))
        term = gate * jnp.dot(br[...], wb_ref[n], preferred_element_type=F32)
        acc = term if acc is None else acc + term
    o_ref[...] = acc.astype(BF16)


def _merge(h, branches, w_merge, w_branch, layer):
    m, d = h.shape
    bw = branches[0].shape[1]
    tm = _tile(m, 512)
    tn = _tile(d, 256)
    nj = d // tn
    br_spec = pl.BlockSpec((tm, bw), lambda i, j: (i, 0))
    return pl.pallas_call(
        _merge_kernel,
        grid=(m // tm, nj),
        in_specs=[pl.BlockSpec((tm, d), lambda i, j: (i, 0))] + [br_spec] * 4 + [
            pl.BlockSpec((None, d, tn), lambda i, j, n=n: (layer, 0, n * nj + j)) for n in range(4)
        ] + [pl.BlockSpec((None, 4, bw, tn), lambda i, j: (layer, 0, 0, j))],
        out_specs=pl.BlockSpec((tm, tn), lambda i, j: (i, j)),
        out_shape=jax.ShapeDtypeStruct((m, d), BF16),
        compiler_params=_params("parallel", "arbitrary"),
        name="merge",
    )(h, *branches, w_merge, w_merge, w_merge, w_merge, w_branch)


def _outproj_kernel(a_ref, w_ref, x_ref, g_ref, o_ref):
    y = jnp.dot(a_ref[...], w_ref[...], preferred_element_type=F32)
    o_ref[...] = x_ref[...] + g_ref[0] * y


def _outproj(a, w, layer, x, gate, rows_per_mod):
    m, d = a.shape
    r = gate.shape[1]
    tm = r if r > 1 else _tile(rows_per_mod, 1024)
    per = rows_per_mod // tm if r == 1 else 1
    tn = _tile(d, 1024)
    return pl.pallas_call(
        _outproj_kernel,
        grid=(m // tm, d // tn),
        in_specs=[
            pl.BlockSpec((tm, d), lambda i, j: (i, 0)),
            pl.BlockSpec((None, d, tn), lambda i, j: (layer, 0, j)),
            pl.BlockSpec((tm, tn), lambda i, j: (i, j)),
            pl.BlockSpec((1, r, tn), lambda i, j: (i // per, 0, j)),
        ],
        out_specs=pl.BlockSpec((tm, tn), lambda i, j: (i, j)),
        out_shape=jax.ShapeDtypeStruct((m, d), F32),
        compiler_params=_params("parallel", "arbitrary"),
        name="out_proj",
    )(a, w, x, gate)


def _final_norm_kernel(x_ref, w_ref, o_ref):
    x = x_ref[...]
    o_ref[...] = x * lax.rsqrt(jnp.mean(x * x, axis=-1, keepdims=True) + EPS) * w_ref[...]


def _final_norm(x, w):
    m, d = x.shape
    tr = _tile(m, 256)
    return pl.pallas_call(
        _final_norm_kernel,
        grid=(m // tr,),
        in_specs=[pl.BlockSpec((tr, d), lambda i: (i, 0)), pl.BlockSpec((1, d), lambda i: (0, 0))],
        out_specs=pl.BlockSpec((tr, d), lambda i: (i, 0)),
        out_shape=jax.ShapeDtypeStruct((m, d), F32),
        compiler_params=_params("parallel"),
        name="final_norm",
    )(x, w.reshape(1, d))


def _rope_tables(pos, dh, dqk):
    rope_dim = dqk // 4
    half = rope_dim // 2
    inv_freq = ROPE_THETA ** (-jnp.arange(half, dtype=F32) * (2.0 / rope_dim))
    ang = pos.astype(F32)[:, None] * inv_freq[None, :]
    cos, sin = jnp.cos(ang), jnp.sin(ang)
    n = pos.shape[0]
    pad = jnp.zeros((n, dqk - rope_dim), F32)
    comp_cos = jnp.concatenate([cos, cos, pad + 1.0], axis=1)
    comp_sa = jnp.concatenate([-sin, jnp.zeros_like(sin), pad], axis=1)
    comp_sb = jnp.concatenate([jnp.zeros_like(sin), sin, pad], axis=1)
    reps = dh // dqk
    return tuple(jnp.tile(t, (1, reps)) for t in (comp_cos, comp_sa, comp_sb))


def _tail_tile(state, width):
    nb, k, _ = state.shape
    return jnp.concatenate([jnp.zeros((nb, SUBLANES - k, width), F32), state.astype(F32)], axis=1)


def _pad_time(x, tp):
    nb, t = x.shape[:2]
    return jnp.pad(x, [(0, 0), (0, tp - t)] + [(0, 0)] * (x.ndim - 2))


def kernel(x_prompt, x_sample, c_prompt, c_sample, state_conv_a, state_gdn_conv, state_gdn, state_gla, cache_k, cache_v, page_table, w_ada, b_ada, w_in, conv_a_w, gdn_conv_w, gdn_a_log, gdn_dt_bias, gdn_norm_w, diff_lambda, diff_norm_w, gla_w_gk2, gla_b_gk, gla_norm_w, w_branch, w_out, final_norm_w):
    nbp, tpp, d = x_prompt.shape
    nbs, tvs, _ = x_sample.shape
    depth = w_in.shape[0]
    bw = d // 4
    gdn_h = gdn_a_log.shape[1]
    diff_h = cache_k.shape[3]
    dh = cache_k.shape[4]
    dqk = dh // 2
    gla_h = state_gla.shape[2]
    gla_dk = state_gla.shape[3]
    rank = gla_w_gk2.shape[1]
    conv_a_k = conv_a_w.shape[1]
    gdn_k = gdn_conv_w.shape[1]
    past_len = page_table.shape[1] * cache_k.shape[2]
    tps = SUBLANES
    assert tvs <= tps and 2 * gdn_h + rank <= LANES and bw // gdn_h == LANES and dh == LANES

    off_gdn = 4 * bw
    off_side = off_gdn + 4 * bw
    off_diff = off_side + 2 * gdn_h
    off_gla = off_diff + 4 * bw
    off_lr = off_gla + 3 * bw
    off_merge = off_lr + rank
    w_t = jnp.swapaxes(w_in, 1, 2)
    w_ab = _repack(w_t, 0, off_side)
    w_cd = _repack(w_t, off_diff, off_lr - off_diff)
    w_side = _side_weights(w_t, off_side, 2 * gdn_h, off_lr, rank)
    w_merge = _repack(w_t, off_merge, 4 * d)
    w_branch_b = w_branch.astype(BF16)
    w_out_b = w_out.astype(BF16)
    wgk_pad = jnp.concatenate(
        [jnp.zeros((depth, 2 * gdn_h, gla_h * gla_dk), F32), gla_w_gk2,
         jnp.zeros((depth, LANES - 2 * gdn_h - rank, gla_h * gla_dk), F32)], axis=1).astype(BF16)
    lane_pad = jnp.zeros((depth, LANES - gdn_h), F32)
    alog_pad = jnp.concatenate([gdn_a_log, lane_pad], axis=1)
    dtb_pad = jnp.concatenate([gdn_dt_bias, lane_pad], axis=1)

    pool, page = cache_k.shape[1:3]
    cache_k_rows = cache_k.reshape(depth, pool, page * diff_h, dh)
    cache_v_rows = cache_v.reshape(depth, pool, page * diff_h, dh)

    n_c = nbp + nbs
    c_rows = -(-n_c // SUBLANES) * SUBLANES
    c_all = jnp.concatenate([c_prompt, c_sample, jnp.zeros((c_rows - n_c, d), F32)], axis=0)
    mod = _ada(c_all, w_ada, b_ada)

    groups = []
    groups.append(dict(
        nb=nbp, tp=tpp, tv=tpp, x=x_prompt.reshape(nbp * tpp, d), mod_rows=slice(0, nbp), per_row=False,
        conv_a=jnp.zeros((depth, nbp, conv_a_k - 1, bw), F32),
        gdn_conv=jnp.zeros((depth, nbp, gdn_k - 1, 3 * bw), F32),
        gdn_s=jnp.zeros((depth, nbp) + state_gdn.shape[2:], F32),
        gla_s=jnp.zeros((depth, nbp) + state_gla.shape[2:], F32),
        tabs=_rope_tables(jnp.arange(tpp, dtype=jnp.int32), dh, dqk), paged=False))
    groups.append(dict(
        nb=nbs, tp=tps, tv=tvs, x=_pad_time(x_sample, tps).reshape(nbs * tps, d), mod_rows=slice(nbp, n_c),
        per_row=True, conv_a=state_conv_a, gdn_conv=state_gdn_conv, gdn_s=state_gdn, gla_s=state_gla,
        tabs=_rope_tables(past_len + jnp.arange(tps, dtype=jnp.int32), dh, dqk), paged=True))

    results = []
    for g in groups:
        nb, tp, tv = g["nb"], g["tp"], g["tv"]
        x = g["x"]
        outs = [[] for _ in range(4)]
        kv_stacks = None
        for l in range(depth):
            m_l = mod[l, g["mod_rows"]]
            if g["per_row"]:
                m_l = jnp.repeat(m_l, tp, axis=0).reshape(1, nb * tp, 3 * d)
            else:
                m_l = m_l.reshape(nb, 1, 3 * d)
            shift, scale, gate = m_l[..., :d], m_l[..., d:2 * d], m_l[..., 2 * d:]
            h = _prenorm(x, scale, shift, tp)
            z_ab = _matmul(h, w_ab, l)
            z_c = _matmul(h, w_cd, l, 0, 4 * bw)
            z_d = _matmul(h, w_cd, l, 4 * bw, 3 * bw)
            zs = _matmul(h, w_side, l, tn_pref=LANES)

            out_a, tail_a = _conva(z_ab, _tail_tile(g["conv_a"][l], bw), conv_a_w[l], nb, tp, bw)
            out_b, tail_b, s_gdn = _gdn(
                z_ab, zs, _tail_tile(g["gdn_conv"][l], 3 * bw), gdn_conv_w[l], alog_pad[l:l + 1],
                dtb_pad[l:l + 1], gdn_norm_w[l:l + 1], g["gdn_s"][l].astype(F32), nb, tp, tv, bw, gdn_h, 1)
            q_r, k_r, v_r, *kv_stacks = _rope(z_c, g["tabs"], nb, tp, bw, diff_h, 0, l, depth, kv_stacks)
            lam_init = 0.8 - 0.6 * math.exp(-0.3 * l)
            if g["paged"]:
                out_c = _decode(q_r, k_r, v_r, z_c, cache_k_rows, cache_v_rows, page_table, l, diff_lambda[l],
                                diff_norm_w[l:l + 1], nb, tv, bw, diff_h, lam_init, 3)
            else:
                out_c = _flash(q_r, k_r, v_r, z_c, diff_lambda[l], diff_norm_w[l:l + 1], nb, tp, bw, diff_h,
                               lam_init, 3 * diff_h)
            out_d, s_gla_t = _gla(
                z_d, zs, wgk_pad[l], gla_b_gk[l:l + 1], gla_norm_w[l:l + 1],
                jnp.swapaxes(g["gla_s"][l].astype(F32), -1, -2), nb, tp, tv, bw, gla_h, 0)

            merged = _merge(h, (out_a, out_b, out_c, out_d), w_merge, w_branch_b, l)
            x = _outproj(merged, w_out_b, l, x, gate, tp)

            last = tv - (tp - SUBLANES)
            outs[0].append(tail_a[:, last - (conv_a_k - 1):last])
            outs[1].append(tail_b[:, last - (gdn_k - 1):last])
            outs[2].append(s_gdn)
            outs[3].append(jnp.swapaxes(s_gla_t, -1, -2))
        y = _final_norm(x, final_norm_w).reshape(nb, tp, d)[:, :tv]
        new_k, new_v = (s.reshape(depth, nb, tp, diff_h, dh)[:, :, :tv] for s in kv_stacks)
        results.append((y, *[jnp.stack(o) for o in outs], new_k, new_v))

    (yp, *sp), (ys, *ss) = results
    return (yp, ys, *sp, *ss)
```
